```python
import math
import jax, jax.numpy as jnp
from jax import lax
import numpy as np

D_MODEL = 4096
BATCH = 4
SEQ = 4096
DEPTH = 1
DEC_BATCH = 2
DEC_SEQ = 8192
PAST_LEN = 128

H_DIFF = D_MODEL // 256
DH_DIFF = 64
V_DIFF = 2 * DH_DIFF
H_MLA = D_MODEL // 256
Q_LORA = 1024
KV_LORA = 512
QK_NOPE = 128
QK_ROPE = 64
V_MLA = 128
ROPE_THETA = 10000.0
N_BUCKETS = 32
MAX_DISTANCE = 128
D_FF = 11008
CONV_WIDTH = 3
Q_BLOCK = 128
EPS = 1e-6

DIFF_Q_COLS = H_DIFF * 2 * DH_DIFF
DIFF_K_COLS = H_DIFF * 2 * DH_DIFF
DIFF_V_COLS = H_DIFF * V_DIFF
MLA_V_COLS = H_MLA * V_MLA
GATE_COLS = 2 * D_MODEL
IN_COLS = DIFF_Q_COLS + DIFF_K_COLS + DIFF_V_COLS + Q_LORA + KV_LORA + QK_ROPE + GATE_COLS
IN_SPLITS = (
    DIFF_Q_COLS,
    DIFF_Q_COLS + DIFF_K_COLS,
    DIFF_Q_COLS + DIFF_K_COLS + DIFF_V_COLS,
    DIFF_Q_COLS + DIFF_K_COLS + DIFF_V_COLS + Q_LORA,
    DIFF_Q_COLS + DIFF_K_COLS + DIFF_V_COLS + Q_LORA + KV_LORA,
    DIFF_Q_COLS + DIFF_K_COLS + DIFF_V_COLS + Q_LORA + KV_LORA + QK_ROPE,
)

kernel_name = 'hybrid_diffattn_mla_convffn_encoder'


def rmsnorm(x, g):
    xf = x.astype(jnp.float32)
    y = xf * lax.rsqrt(jnp.mean(xf * xf, axis=-1, keepdims=True) + EPS) * g.astype(jnp.float32)
    return y.astype(x.dtype)


def t5_bucket(rel):
    nb = N_BUCKETS // 2
    max_exact = nb // 2
    ret = (rel > 0).astype(jnp.int32) * nb
    n = jnp.abs(rel)
    nf = jnp.maximum(n, max_exact).astype(jnp.float32)
    large = max_exact + (jnp.log(nf / max_exact) / math.log(MAX_DISTANCE / max_exact)
                         * (nb - max_exact)).astype(jnp.int32)
    large = jnp.minimum(large, nb - 1)
    return ret + jnp.where(n < max_exact, n, large)


def to_blocks(t):
    b, s = t.shape[0], t.shape[1]
    return t.reshape((b, s // Q_BLOCK, Q_BLOCK) + t.shape[2:]).swapaxes(0, 1)


def from_blocks(t):
    n, b, q = t.shape[0], t.shape[1], t.shape[2]
    return t.swapaxes(0, 1).reshape((b, n * q) + t.shape[3:])


def rope(x, cos, sin):
    xf = x.astype(jnp.float32)
    x1, x2 = jnp.split(xf, 2, axis=-1)
    return jnp.concatenate([x1 * cos - x2 * sin, x1 * sin + x2 * cos], axis=-1).astype(x.dtype)


def diff_attention(q, k, v, lam, rel_bias):
    s_len = k.shape[1]
    scale = DH_DIFF ** -0.5
    kpos = jnp.arange(s_len, dtype=jnp.int32)
    qb = to_blocks(q)
    nblk = qb.shape[0]

    def one(args):
        qi, bi = args
        qpos = bi * Q_BLOCK + jnp.arange(Q_BLOCK, dtype=jnp.int32)
        bias = rel_bias[t5_bucket(kpos[None, :] - qpos[:, None])]
        bias = bias.astype(jnp.float32).transpose(2, 0, 1)
        s = jnp.einsum('bqhcd,bkhcd->bhcqk', qi, k).astype(jnp.float32) * scale
        p = jax.nn.softmax(s + bias[None, :, None], axis=-1)
        a = p[:, :, 0] - lam * p[:, :, 1]
        return jnp.einsum('bhqk,bkhe->bqhe', a.astype(v.dtype), v)

    out = lax.map(one, (qb, jnp.arange(nblk, dtype=jnp.int32)))
    return from_blocks(out)


def mla_attention(q_nope, q_pe, k_nope, k_pe, v):
    scale = (QK_NOPE + QK_ROPE) ** -0.5

    def one(args):
        qn, qp = args
        s = (jnp.einsum('bqhd,bkhd->bhqk', qn, k_nope)
             + jnp.einsum('bqhr,bkr->bhqk', qp, k_pe)).astype(jnp.float32) * scale
        p = jax.nn.softmax(s, axis=-1)
        return jnp.einsum('bhqk,bkhe->bqhe', p.astype(v.dtype), v)

    out = lax.map(one, (to_blocks(q_nope), to_blocks(q_pe)))
    return from_blocks(out)


def depthwise_conv3(h, w, b):
    hp = jnp.pad(h, ((0, 0), (1, 1), (0, 0)))
    return hp[:, :-2] * w[0] + hp[:, 1:-1] * w[1] + hp[:, 2:] * w[2] + b


def encoder_layer(x, layer_idx, rel_bias, rms_attn_g, w_in, lambda_q1, lambda_k1, lambda_q2,
                  lambda_k2, diff_subln_g, mla_q_norm_g, w_mla_q_up, mla_kv_norm_g, w_mla_kv_up,
                  w_branch_a, w_branch_b, w_out, rms_ffn_g, w_ffn_up, conv_w, conv_b, w_ffn_down):
    b, s_len, _ = x.shape
    h = rmsnorm(x, rms_attn_g)
    proj = h @ w_in
    dq, dk, dv, q_lat, kv_lat, k_pe, gates = jnp.split(proj, IN_SPLITS, axis=-1)

    lambda_init = 0.8 - 0.6 * math.exp(-0.3 * layer_idx)
    f32 = jnp.float32
    lam = (jnp.exp(jnp.sum(lambda_q1.astype(f32) * lambda_k1.astype(f32)))
           - jnp.exp(jnp.sum(lambda_q2.astype(f32) * lambda_k2.astype(f32))) + lambda_init)
    a_out = diff_attention(dq.reshape(b, s_len, H_DIFF, 2, DH_DIFF),
                           dk.reshape(b, s_len, H_DIFF, 2, DH_DIFF),
                           dv.reshape(b, s_len, H_DIFF, V_DIFF), lam, rel_bias)
    a_out = rmsnorm(a_out, diff_subln_g) * (1.0 - lambda_init)
    a_out = a_out.astype(x.dtype).reshape(b, s_len, DIFF_V_COLS)

    q = (rmsnorm(q_lat, mla_q_norm_g) @ w_mla_q_up).reshape(b, s_len, H_MLA, QK_NOPE + QK_ROPE)
    q_nope, q_pe = q[..., :QK_NOPE], q[..., QK_NOPE:]
    kv = (rmsnorm(kv_lat, mla_kv_norm_g) @ w_mla_kv_up).reshape(b, s_len, H_MLA, QK_NOPE + V_MLA)
    k_nope, v_mla = kv[..., :QK_NOPE], kv[..., QK_NOPE:]
    pos = jnp.arange(s_len, dtype=jnp.float32)
    inv_freq = ROPE_THETA ** (-jnp.arange(0, QK_ROPE, 2, dtype=jnp.float32) / QK_ROPE)
    ang = pos[:, None] * inv_freq[None, :]
    cos, sin = jnp.cos(ang), jnp.sin(ang)
    q_pe = rope(q_pe, cos[None, :, None, :], sin[None, :, None, :])
    k_pe = rope(k_pe, cos[None], sin[None])
    b_out = mla_attention(q_nope, q_pe, k_nope, k_pe, v_mla).reshape(b, s_len, MLA_V_COLS)

    g_a, g_b = jnp.split(gates, 2, axis=-1)
    merged = jax.nn.sigmoid(g_a) * (a_out @ w_branch_a) + jax.nn.sigmoid(g_b) * (b_out @ w_branch_b)
    x = x + merged @ w_out

    h2 = rmsnorm(x, rms_ffn_g)
    u = depthwise_conv3(h2 @ w_ffn_up, conv_w, conv_b)
    gate, up = jnp.split(u, 2, axis=-1)
    x = x + (jax.nn.silu(gate) * up) @ w_ffn_down
    return x


def setup_inputs(seed: int = 0) -> dict:
    key = jax.random.key(seed)
    ks = jax.random.split(key, 24)
    f32 = jnp.float32
    L = DEPTH

    def nrm(k, shape, fan_in):
        return jax.random.normal(k, shape, f32) * (fan_in ** -0.5)

    def gain(k, shape):
        return 1.0 + 0.02 * jax.random.normal(k, shape, f32)

    return {
        'x_prompt': jax.random.normal(ks[0], (BATCH, SEQ, D_MODEL), f32),
        'x_sample': jax.random.normal(ks[1], (DEC_BATCH, DEC_SEQ, D_MODEL), f32),
        'rel_bias': 0.5 * jax.random.normal(ks[2], (N_BUCKETS, H_DIFF), f32),
        'final_norm_g': gain(ks[3], (D_MODEL,)),
        'rms_attn_g': gain(ks[4], (L, D_MODEL)),
        'w_in': nrm(ks[5], (L, D_MODEL, IN_COLS), D_MODEL),
        'lambda_q1': 0.1 * jax.random.normal(ks[6], (L, DH_DIFF), f32),
        'lambda_k1': 0.1 * jax.random.normal(ks[7], (L, DH_DIFF), f32),
        'lambda_q2': 0.1 * jax.random.normal(ks[8], (L, DH_DIFF), f32),
        'lambda_k2': 0.1 * jax.random.normal(ks[9], (L, DH_DIFF), f32),
        'diff_subln_g': gain(ks[10], (L, V_DIFF)),
        'mla_q_norm_g': gain(ks[11], (L, Q_LORA)),
        'w_mla_q_up': nrm(ks[12], (L, Q_LORA, H_MLA * (QK_NOPE + QK_ROPE)), Q_LORA),
        'mla_kv_norm_g': gain(ks[13], (L, KV_LORA)),
        'w_mla_kv_up': nrm(ks[14], (L, KV_LORA, H_MLA * (QK_NOPE + V_MLA)), KV_LORA),
        'w_branch_a': nrm(ks[15], (L, DIFF_V_COLS, D_MODEL), DIFF_V_COLS),
        'w_branch_b': nrm(ks[16], (L, MLA_V_COLS, D_MODEL), MLA_V_COLS),
        'w_out': nrm(ks[17], (L, D_MODEL, D_MODEL), D_MODEL),
        'rms_ffn_g': gain(ks[18], (L, D_MODEL)),
        'w_ffn_up': nrm(ks[19], (L, D_MODEL, 2 * D_FF), D_MODEL),
        'conv_w': nrm(ks[20], (L, CONV_WIDTH, 2 * D_FF), CONV_WIDTH),
        'conv_b': 0.02 * jax.random.normal(ks[21], (L, 2 * D_FF), f32),
        'w_ffn_down': nrm(ks[22], (L, D_FF, D_MODEL), D_FF),
    }


def reference(x_prompt, x_sample, rel_bias, final_norm_g, rms_attn_g, w_in, lambda_q1, lambda_k1,
              lambda_q2, lambda_k2, diff_subln_g, mla_q_norm_g, w_mla_q_up, mla_kv_norm_g,
              w_mla_kv_up, w_branch_a, w_branch_b, w_out, rms_ffn_g, w_ffn_up, conv_w, conv_b,
              w_ffn_down):
    def trunk(x):
        for l in range(DEPTH):
            x = encoder_layer(x, l, rel_bias, rms_attn_g[l], w_in[l], lambda_q1[l], lambda_k1[l],
                              lambda_q2[l], lambda_k2[l], diff_subln_g[l], mla_q_norm_g[l],
                              w_mla_q_up[l], mla_kv_norm_g[l], w_mla_kv_up[l], w_branch_a[l],
                              w_branch_b[l], w_out[l], rms_ffn_g[l], w_ffn_up[l], conv_w[l],
                              conv_b[l], w_ffn_down[l])
        return rmsnorm(x, final_norm_g)

    y_prompt = trunk(x_prompt)
    y_sample = trunk(x_sample)
    return (y_prompt, y_sample)
```

```python
import functools
import math

import jax
import jax.numpy as jnp
from jax import lax
from jax.experimental import pallas as pl
from jax.experimental.pallas import tpu as pltpu

DH_DIFF = 64
V_DIFF = 2 * DH_DIFF
QK_NOPE = 128
QK_ROPE = 64
V_MLA = 128
ROPE_THETA = 10000.0
N_BUCKETS = 32
MAX_DISTANCE = 128
EPS = 1e-6

LANE = 128
BF16_SUBLANE = 16
VMEM_LIMIT_BYTES = 56 * 1024 * 1024

MLA_QK_PAD = 2 * LANE
HALO = BF16_SUBLANE

TILES = dict(
    norm_tm=512,
    mm_tm=1024, mm_tn=512,
    lat_tm=512,
    qup_tm=1024, qup_tn=1024,
    kvup_tm=1024, kvup_tn=1024,
    ffn_tm=1024, ffn_tn=256,
    down_tm=512, down_tn=256,
    diff_tq=256, diff_tk=512,
    mla_tq=512, mla_tk=512,
)

F32 = jnp.float32
BF16 = jnp.bfloat16


def _tile(name, dim):
    t = min(TILES[name], dim)
    assert dim % t == 0, (name, dim, t)
    return t


def _params(*sem):
    return pltpu.CompilerParams(dimension_semantics=sem, vmem_limit_bytes=VMEM_LIMIT_BYTES)


def _rms(x, g):
    return x * lax.rsqrt(jnp.mean(x * x, axis=-1, keepdims=True) + EPS) * g


def _rope128(p, c, s1, s2):
    return (p * c + pltpu.roll(p, LANE - QK_ROPE // 2, 1) * s1
            + pltpu.roll(p, QK_ROPE // 2, 1) * s2)


def _rmsnorm_kernel(x_ref, g_ref, o_ref):
    o_ref[...] = _rms(x_ref[...].astype(F32), g_ref[...]).astype(o_ref.dtype)


def rmsnorm(x, g, out_dtype):
    m, d = x.shape
    tm = _tile("norm_tm", m)
    return pl.pallas_call(
        _rmsnorm_kernel,
        grid=(m // tm,),
        in_specs=[pl.BlockSpec((tm, d), lambda i: (i, 0)),
                  pl.BlockSpec((1, d), lambda i: (0, 0))],
        out_specs=pl.BlockSpec((tm, d), lambda i: (i, 0)),
        out_shape=jax.ShapeDtypeStruct((m, d), out_dtype),
        compiler_params=_params("parallel"),
        name="rmsnorm",
    )(x, g.reshape(1, d).astype(F32))


def _mm_kernel(a_ref, b_ref, o_ref):
    o_ref[...] = jnp.dot(a_ref[...], b_ref[...], preferred_element_type=F32).astype(o_ref.dtype)


def matmul(a, b, out_dtype, tm, tn, name):
    m, k = a.shape
    n = b.shape[1]
    return pl.pallas_call(
        _mm_kernel,
        grid=(m // tm, n // tn),
        in_specs=[pl.BlockSpec((tm, k), lambda i, j: (i, 0)),
                  pl.BlockSpec((k, tn), lambda i, j: (0, j))],
        out_specs=pl.BlockSpec((tm, tn), lambda i, j: (i, j)),
        out_shape=jax.ShapeDtypeStruct((m, n), out_dtype),
        compiler_params=_params("parallel", "arbitrary"),
        name=name,
    )(a, b)


def _mm_res_kernel(a_ref, b_ref, r_ref, o_ref):
    o_ref[...] = r_ref[...] + jnp.dot(a_ref[...], b_ref[...], preferred_element_type=F32)


def matmul_residual(a, b, res, tm, tn, name):
    m, k = a.shape
    n = b.shape[1]
    return pl.pallas_call(
        _mm_res_kernel,
        grid=(m // tm, n // tn),
        in_specs=[pl.BlockSpec((tm, k), lambda i, j: (i, 0)),
                  pl.BlockSpec((k, tn), lambda i, j: (0, j)),
                  pl.BlockSpec((tm, tn), lambda i, j: (i, j))],
        out_specs=pl.BlockSpec((tm, tn), lambda i, j: (i, j)),
        out_shape=jax.ShapeDtypeStruct((m, n), F32),
        compiler_params=_params("parallel", "arbitrary"),
        name=name,
    )(a, b, res)


def _merge_kernel(a_ref, b_ref, wa_ref, wb_ref, ga_ref, gb_ref, o_ref):
    ya = jnp.dot(a_ref[...], wa_ref[...], preferred_element_type=F32)
    yb = jnp.dot(b_ref[...], wb_ref[...], preferred_element_type=F32)
    sa = 1.0 / (1.0 + jnp.exp(-ga_ref[...]))
    sb = 1.0 / (1.0 + jnp.exp(-gb_ref[...]))
    o_ref[...] = (sa * ya + sb * yb).astype(o_ref.dtype)


def gated_merge(a, b, wa, wb, gates, tm, tn):
    m, ka = a.shape
    kb = b.shape[1]
    n = wa.shape[1]
    nb = n // tn
    return pl.pallas_call(
        _merge_kernel,
        grid=(m // tm, nb),
        in_specs=[pl.BlockSpec((tm, ka), lambda i, j: (i, 0)),
                  pl.BlockSpec((tm, kb), lambda i, j: (i, 0)),
                  pl.BlockSpec((ka, tn), lambda i, j: (0, j)),
                  pl.BlockSpec((kb, tn), lambda i, j: (0, j)),
                  pl.BlockSpec((tm, tn), lambda i, j: (i, j)),
                  pl.BlockSpec((tm, tn), lambda i, j: (i, j + nb))],
        out_specs=pl.BlockSpec((tm, tn), lambda i, j: (i, j)),
        out_shape=jax.ShapeDtypeStruct((m, n), BF16),
        compiler_params=_params("parallel", "arbitrary"),
        name="gated_merge",
    )(a, b, wa, wb, gates, gates)


def _mla_q_kernel(lat_ref, g_ref, w_ref, c_ref, s1_ref, s2_ref, o_ref, a_sc, *, heads, scale):
    @pl.when(pl.program_id(1) == 0)
    def _():
        a_sc[...] = _rms(lat_ref[...], g_ref[...]).astype(BF16)

    acc = jnp.dot(a_sc[...], w_ref[...], preferred_element_type=F32)
    c, s1, s2 = c_ref[...], s1_ref[...], s2_ref[...]
    for h in range(heads):
        lo = h * MLA_QK_PAD
        o_ref[:, lo:lo + QK_NOPE] = (acc[:, lo:lo + QK_NOPE] * scale).astype(o_ref.dtype)
        rot = _rope128(acc[:, lo + QK_NOPE:lo + MLA_QK_PAD], c, s1, s2)
        o_ref[:, lo + QK_NOPE:lo + MLA_QK_PAD] = (rot * scale).astype(o_ref.dtype)


def mla_q_up(lat, g, w, rope_tabs, seq, q_lora):
    m = lat.shape[0]
    n = w.shape[1]
    tm = _tile("qup_tm", seq)
    tn = _tile("qup_tn", n)
    pos_blocks = seq // tm
    tab_spec = pl.BlockSpec((tm, LANE), lambda i, j: (i % pos_blocks, 0))
    kern = functools.partial(_mla_q_kernel, heads=tn // MLA_QK_PAD,
                             scale=(QK_NOPE + QK_ROPE) ** -0.5)
    return pl.pallas_call(
        kern,
        grid=(m // tm, n // tn),
        in_specs=[pl.BlockSpec((tm, q_lora), lambda i, j: (i, 0)),
                  pl.BlockSpec((1, q_lora), lambda i, j: (0, 0)),
                  pl.BlockSpec((q_lora, tn), lambda i, j: (0, j)),
                  tab_spec, tab_spec, tab_spec],
        out_specs=pl.BlockSpec((tm, tn), lambda i, j: (i, j)),
        out_shape=jax.ShapeDtypeStruct((m, n), BF16),
        scratch_shapes=[pltpu.VMEM((tm, q_lora), BF16)],
        compiler_params=_params("parallel", "arbitrary"),
        name="mla_q_up",
    )(lat, g.reshape(1, q_lora).astype(F32), w, *rope_tabs)


def _mla_kv_kernel(kv_ref, pe_ref, g_ref, w_ref, c_ref, s1_ref, s2_ref, o_ref, a_sc, *, kv_lora):
    @pl.when(pl.program_id(1) == 0)
    def _():
        a_sc[:, :kv_lora] = _rms(kv_ref[...], g_ref[...]).astype(BF16)
        rot = _rope128(pe_ref[...], c_ref[...], s1_ref[...], s2_ref[...])
        a_sc[:, kv_lora:] = rot.astype(BF16)

    o_ref[...] = jnp.dot(a_sc[...], w_ref[...], preferred_element_type=F32).astype(o_ref.dtype)


def mla_kv_up(lat, g, w_ext, rope_tabs, seq, q_lora, kv_lora):
    m = lat.shape[0]
    n = w_ext.shape[1]
    tm = _tile("kvup_tm", seq)
    tn = _tile("kvup_tn", n)
    pos_blocks = seq // tm
    tab_spec = pl.BlockSpec((tm, LANE), lambda i, j: (i % pos_blocks, 0))
    kv_blk = q_lora // kv_lora
    pe_blk = (q_lora + kv_lora) // LANE
    return pl.pallas_call(
        functools.partial(_mla_kv_kernel, kv_lora=kv_lora),
        grid=(m // tm, n // tn),
        in_specs=[pl.BlockSpec((tm, kv_lora), lambda i, j: (i, kv_blk)),
                  pl.BlockSpec((tm, LANE), lambda i, j: (i, pe_blk)),
                  pl.BlockSpec((1, kv_lora), lambda i, j: (0, 0)),
                  pl.BlockSpec((kv_lora + LANE, tn), lambda i, j: (0, j)),
                  tab_spec, tab_spec, tab_spec],
        out_specs=pl.BlockSpec((tm, tn), lambda i, j: (i, j)),
        out_shape=jax.ShapeDtypeStruct((m, n), BF16),
        scratch_shapes=[pltpu.VMEM((tm, kv_lora + LANE), BF16)],
        compiler_params=_params("parallel", "arbitrary"),
        name="mla_kv_up",
    )(lat, lat, g.reshape(1, kv_lora).astype(F32), w_ext, *rope_tabs)


def _softmax_init(m_sc, l_sc, acc_sc):
    m_sc[...] = jnp.full(m_sc.shape, -jnp.inf, F32)
    l_sc[...] = jnp.zeros(l_sc.shape, F32)
    acc_sc[...] = jnp.zeros(acc_sc.shape, F32)


def _softmax_update(s, v, m_sc, l_sc, acc_sc):
    tk = s.shape[1]
    m_prev = m_sc[...]
    m_new = jnp.maximum(m_prev, jnp.max(s, axis=1, keepdims=True))
    alpha = jnp.exp(m_prev - m_new)
    p = jnp.exp(s - pltpu.repeat(m_new, tk // LANE, axis=1))
    l_sc[...] = alpha * l_sc[...] + jnp.sum(p, axis=1, keepdims=True)
    acc_sc[...] = alpha * acc_sc[...] + jnp.dot(p.astype(BF16), v, preferred_element_type=F32)
    m_sc[...] = m_new


def _qk(q, k):
    return lax.dot_general(q, k, (((1,), (1,)), ((), ())), preferred_element_type=F32)


def _mla_attn_kernel(q_ref, k_ref, v_ref, o_ref, m_sc, l_sc, acc_sc):
    j = pl.program_id(3)

    @pl.when(j == 0)
    def _():
        _softmax_init(m_sc, l_sc, acc_sc)

    _softmax_update(_qk(q_ref[...], k_ref[...]), v_ref[...], m_sc, l_sc, acc_sc)

    @pl.when(j == pl.num_programs(3) - 1)
    def _():
        o_ref[...] = (acc_sc[...] / l_sc[...]).astype(o_ref.dtype)


def mla_attention(q, kv, batch, seq, heads):
    tq = _tile("mla_tq", seq)
    tk = _tile("mla_tk", seq)
    nq, nk = seq // tq, seq // tk
    v_blk0 = heads * MLA_QK_PAD // V_MLA
    return pl.pallas_call(
        _mla_attn_kernel,
        grid=(batch, heads, nq, nk),
        in_specs=[pl.BlockSpec((tq, MLA_QK_PAD), lambda b, h, i, j: (b * nq + i, h)),
                  pl.BlockSpec((tk, MLA_QK_PAD), lambda b, h, i, j: (b * nk + j, h)),
                  pl.BlockSpec((tk, V_MLA), lambda b, h, i, j: (b * nk + j, v_blk0 + h))],
        out_specs=pl.BlockSpec((tq, V_MLA), lambda b, h, i, j: (b * nq + i, h)),
        out_shape=jax.ShapeDtypeStruct((batch * seq, heads * V_MLA), BF16),
        scratch_shapes=[pltpu.VMEM((tq, LANE), F32), pltpu.VMEM((tq, LANE), F32),
                        pltpu.VMEM((tq, V_MLA), F32)],
        compiler_params=_params("parallel", "parallel", "parallel", "arbitrary"),
        name="mla_attention",
    )(q, kv, kv)


def _diff_attn_kernel(tab_ref, ids_ref, q_ref, k_ref, v_ref, lq1_ref, lk1_ref, lq2_ref, lk2_ref,
                      g_ref, o_ref, q2_sc, bias_sc, m_sc, l_sc, acc_sc, *, tq, tk, lambda_init):
    h, i, j = pl.program_id(1), pl.program_id(2), pl.program_id(3)

    @pl.when(j == 0)
    def _():
        q = q_ref[...]
        lane = lax.broadcasted_iota(jnp.int32, q.shape, 1)
        zero = jnp.zeros_like(q)
        q2_sc[:tq] = jnp.where(lane < DH_DIFF, q, zero)
        q2_sc[tq:] = jnp.where(lane >= DH_DIFF, q, zero)
        _softmax_init(m_sc, l_sc, acc_sc)

    ids = ids_ref[...]
    f = jnp.zeros(ids.shape, F32)
    for b in range(N_BUCKETS):
        f = jnp.where(ids == b, tab_ref[h, b], f)
    off = j * tk - i * tq
    near = jnp.logical_and(off + tk - 1 > -MAX_DISTANCE, off - tq + 1 < MAX_DISTANCE)

    @pl.when(near)
    def _():
        t = pltpu.roll(jnp.broadcast_to(f, (tq, tq + tk)), tk + 1, 1, stride=1, stride_axis=0)
        bias_sc[...] = t[:, :tk]

    @pl.when(jnp.logical_not(near))
    def _():
        bias_sc[...] = jnp.broadcast_to(f[:, :tk], (tq, tk))

    s = _qk(q2_sc[...], k_ref[...])
    bias = bias_sc[...]
    s = s + jnp.concatenate([bias, bias], axis=0)
    _softmax_update(s, v_ref[...], m_sc, l_sc, acc_sc)

    @pl.when(j == pl.num_programs(3) - 1)
    def _():
        o = acc_sc[...] / l_sc[...]
        lam = (jnp.exp(jnp.sum(lq1_ref[...] * lk1_ref[...], axis=1, keepdims=True))
               - jnp.exp(jnp.sum(lq2_ref[...] * lk2_ref[...], axis=1, keepdims=True)) + lambda_init)
        a = o[:tq] - lam * o[tq:]
        o_ref[...] = (_rms(a, g_ref[...]) * (1.0 - lambda_init)).astype(o_ref.dtype)


def _t5_bucket(rel):
    nb = N_BUCKETS // 2
    max_exact = nb // 2
    ret = (rel > 0).astype(jnp.int32) * nb
    n = jnp.abs(rel)
    nf = jnp.maximum(n, max_exact).astype(F32)
    large = max_exact + (jnp.log(nf / max_exact) / math.log(MAX_DISTANCE / max_exact)
                         * (nb - max_exact)).astype(jnp.int32)
    large = jnp.minimum(large, nb - 1)
    return ret + jnp.where(n < max_exact, n, large)


def diff_attention(qkv, rel_bias, lams, subln_g, lambda_init, batch, seq, heads):
    tq = _tile("diff_tq", seq)
    tk = _tile("diff_tk", seq)
    nq, nk = seq // tq, seq // tk
    step = math.gcd(tq, tk)
    offs = jnp.arange(-(seq - tq), seq - tk + 1, step, dtype=jnp.int32)
    win = offs[:, None] - (tq - 1) + jnp.arange(tq + tk, dtype=jnp.int32)[None, :]
    ids = _t5_bucket(win).reshape(offs.shape[0], 1, tq + tk)
    vec = lambda: pl.BlockSpec((1, DH_DIFF), lambda b, h, i, j: (0, 0))
    kern = functools.partial(_diff_attn_kernel, tq=tq, tk=tk, lambda_init=lambda_init)
    return pl.pallas_call(
        kern,
        grid=(batch, heads, nq, nk),
        in_specs=[pl.BlockSpec(memory_space=pltpu.SMEM),
                  pl.BlockSpec((None, 1, tq + tk),
                               lambda b, h, i, j: ((j * tk - i * tq + seq - tq) // step, 0, 0)),
                  pl.BlockSpec((tq, V_DIFF), lambda b, h, i, j: (b * nq + i, h)),
                  pl.BlockSpec((tk, V_DIFF), lambda b, h, i, j: (b * nk + j, heads + h)),
                  pl.BlockSpec((tk, V_DIFF), lambda b, h, i, j: (b * nk + j, 2 * heads + h)),
                  vec(), vec(), vec(), vec(),
                  pl.BlockSpec((1, V_DIFF), lambda b, h, i, j: (0, 0))],
        out_specs=pl.BlockSpec((tq, V_DIFF), lambda b, h, i, j: (b * nq + i, h)),
        out_shape=jax.ShapeDtypeStruct((batch * seq, heads * V_DIFF), BF16),
        scratch_shapes=[pltpu.VMEM((2 * tq, V_DIFF), BF16), pltpu.VMEM((tq, tk), F32),
                        pltpu.VMEM((2 * tq, LANE), F32), pltpu.VMEM((2 * tq, LANE), F32),
                        pltpu.VMEM((2 * tq, V_DIFF), F32)],
        compiler_params=_params("parallel", "parallel", "parallel", "arbitrary"),
        name="diff_attention",
    )(rel_bias.T.astype(F32), ids, qkv, qkv, qkv,
      *[v.reshape(1, DH_DIFF).astype(F32) for v in lams], subln_g.reshape(1, V_DIFF).astype(F32))


def _ffn_up_kernel(a_ref, prev_ref, next_ref, wg_ref, wu_ref, cwg_ref, cwu_ref, cbg_ref, cbu_ref,
                   o_ref, a_sc, *, tm, tiles_per_seq):
    i = pl.program_id(0)

    @pl.when(pl.program_id(1) == 0)
    def _():
        pos = i % tiles_per_seq
        prev = prev_ref[...]
        nxt = next_ref[...]
        a_sc[:HALO] = jnp.where(pos == 0, jnp.zeros_like(prev), prev)
        a_sc[HALO:HALO + tm] = a_ref[...]
        a_sc[HALO + tm:] = jnp.where(pos == tiles_per_seq - 1, jnp.zeros_like(nxt), nxt)

    def conv(w_ref, cw_ref, cb_ref):
        r = jnp.dot(a_sc[...], w_ref[...], preferred_element_type=F32)
        rows = r.shape[0]
        r_prev = pltpu.roll(r, 1, 0)[HALO:HALO + tm]
        r_next = pltpu.roll(r, rows - 1, 0)[HALO:HALO + tm]
        cw = cw_ref[...]
        return (r_prev * cw[0:1] + r[HALO:HALO + tm] * cw[1:2] + r_next * cw[2:3]) + cb_ref[...]

    gate = conv(wg_ref, cwg_ref, cbg_ref)
    up = conv(wu_ref, cwu_ref, cbu_ref)
    o_ref[...] = (gate * (1.0 / (1.0 + jnp.exp(-gate))) * up).astype(o_ref.dtype)


def ffn_up(h, w_up, conv_w, conv_b, seq):
    m, d = h.shape
    d_ff = w_up.shape[1] // 2
    tm = _tile("ffn_tm", seq)
    tn = _tile("ffn_tn", d_ff)
    nb = d_ff // tn
    hb = tm // HALO
    last_hb = m // HALO - 1
    kern = functools.partial(_ffn_up_kernel, tm=tm, tiles_per_seq=seq // tm)
    return pl.pallas_call(
        kern,
        grid=(m // tm, nb),
        in_specs=[pl.BlockSpec((tm, d), lambda i, j: (i, 0)),
                  pl.BlockSpec((HALO, d), lambda i, j: (jnp.maximum(i * hb - 1, 0), 0)),
                  pl.BlockSpec((HALO, d), lambda i, j: (jnp.minimum((i + 1) * hb, last_hb), 0)),
                  pl.BlockSpec((d, tn), lambda i, j: (0, j)),
                  pl.BlockSpec((d, tn), lambda i, j: (0, j + nb)),
                  pl.BlockSpec((3, tn), lambda i, j: (0, j)),
                  pl.BlockSpec((3, tn), lambda i, j: (0, j + nb)),
                  pl.BlockSpec((1, tn), lambda i, j: (0, j)),
                  pl.BlockSpec((1, tn), lambda i, j: (0, j + nb))],
        out_specs=pl.BlockSpec((tm, tn), lambda i, j: (i, j)),
        out_shape=jax.ShapeDtypeStruct((m, d_ff), BF16),
        scratch_shapes=[pltpu.VMEM((tm + 2 * HALO, d), BF16)],
        compiler_params=_params("parallel", "arbitrary"),
        name="ffn_up_conv_silu",
    )(h, h, h, w_up, w_up, conv_w, conv_w, conv_b, conv_b)


def _prep_w_in(w_in, d_model, heads):
    dq = heads * 2 * DH_DIFF
    dv = heads * V_DIFF
    qkv_cols = 2 * dq + dv
    lat_cols = w_in.shape[1] - qkv_cols - 2 * d_model
    lat_pad = -lat_cols % LANE
    col_scale = jnp.concatenate([jnp.full((dq,), DH_DIFF ** -0.5, F32), jnp.ones((dq + dv,), F32)])
    w_qkv = (w_in[:, :qkv_cols] * col_scale).astype(BF16)
    w_lat = jnp.pad(w_in[:, qkv_cols:qkv_cols + lat_cols], ((0, 0), (0, lat_pad))).astype(BF16)
    w_gate = w_in[:, qkv_cols + lat_cols:].astype(BF16)
    return w_qkv, w_lat, w_gate


def _prep_w_q_up(w, heads):
    k = w.shape[0]
    w = w.reshape(k, heads, QK_NOPE + QK_ROPE)
    w = jnp.pad(w, ((0, 0), (0, 0), (0, MLA_QK_PAD - QK_NOPE - QK_ROPE)))
    return w.reshape(k, heads * MLA_QK_PAD).astype(BF16)


def _prep_w_kv_up(w, heads):
    kv = w.shape[0]
    w = w.reshape(kv, heads, QK_NOPE + V_MLA)
    top_k = jnp.pad(w[:, :, :QK_NOPE], ((0, 0), (0, 0), (0, MLA_QK_PAD - QK_NOPE)))
    eye = jnp.pad(jnp.eye(QK_ROPE, dtype=F32), ((0, LANE - QK_ROPE), (QK_NOPE, MLA_QK_PAD - QK_NOPE - QK_ROPE)))
    bot_k = jnp.broadcast_to(eye[:, None, :], (LANE, heads, MLA_QK_PAD))
    w_k = jnp.concatenate([top_k, bot_k], axis=0).reshape(kv + LANE, heads * MLA_QK_PAD)
    w_v = jnp.pad(w[:, :, QK_NOPE:].reshape(kv, heads * V_MLA), ((0, LANE), (0, 0)))
    return jnp.concatenate([w_k, w_v], axis=1).astype(BF16)


def _rope_tables(seq):
    pos = jnp.arange(seq, dtype=F32)
    inv_freq = ROPE_THETA ** (-jnp.arange(0, QK_ROPE, 2, dtype=F32) / QK_ROPE)
    ang = pos[:, None] * inv_freq[None, :]
    cos, sin = jnp.cos(ang), jnp.sin(ang)
    half = QK_ROPE // 2
    z = lambda n: jnp.zeros((seq, n), F32)
    c = jnp.concatenate([cos, cos, z(LANE - QK_ROPE)], axis=1)
    s1 = jnp.concatenate([-sin, z(LANE - half)], axis=1)
    s2 = jnp.concatenate([z(half), sin, z(LANE - QK_ROPE)], axis=1)
    return c, s1, s2


def _encoder_layer(x, batch, seq, layer_idx, rel_bias, wts):
    (rms_attn_g, w_qkv, w_lat, w_gate, lams, diff_subln_g, mla_q_norm_g, w_q_up, mla_kv_norm_g,
     w_kv_ext, w_branch_a, w_branch_b, w_out, rms_ffn_g, w_ffn_up, conv_w, conv_b, w_ffn_down) = wts
    m, d_model = x.shape
    heads = w_branch_a.shape[0] // V_DIFF
    q_lora = mla_q_norm_g.shape[0]
    kv_lora = mla_kv_norm_g.shape[0]
    assert q_lora % kv_lora == 0 and (q_lora + kv_lora) % LANE == 0
    lambda_init = 0.8 - 0.6 * math.exp(-0.3 * layer_idx)
    tm = _tile("mm_tm", seq)

    h = rmsnorm(x, rms_attn_g, BF16)
    qkv = matmul(h, w_qkv, BF16, tm, _tile("mm_tn", w_qkv.shape[1]), "in_proj_qkv")
    lat = matmul(h, w_lat, F32, _tile("lat_tm", seq), w_lat.shape[1], "in_proj_latent")
    gates = matmul(h, w_gate, F32, tm, _tile("mm_tn", d_model), "in_proj_gates")

    a_out = diff_attention(qkv, rel_bias, lams, diff_subln_g, lambda_init, batch, seq, heads)

    rope_tabs = _rope_tables(seq)
    q_mla = mla_q_up(lat, mla_q_norm_g, w_q_up, rope_tabs, seq, q_lora)
    kv_mla = mla_kv_up(lat, mla_kv_norm_g, w_kv_ext, rope_tabs, seq, q_lora, kv_lora)
    b_out = mla_attention(q_mla, kv_mla, batch, seq, heads)

    merged = gated_merge(a_out, b_out, w_branch_a, w_branch_b, gates, tm, _tile("mm_tn", d_model))
    x = matmul_residual(merged, w_out, x, tm, _tile("mm_tn", d_model), "out_proj")

    h2 = rmsnorm(x, rms_ffn_g, BF16)
    act = ffn_up(h2, w_ffn_up, conv_w, conv_b, seq)
    return matmul_residual(act, w_ffn_down, x, _tile("down_tm", seq), _tile("down_tn", d_model),
                           "ffn_down")


def kernel(x_prompt, x_sample, rel_bias, final_norm_g, rms_attn_g, w_in, lambda_q1, lambda_k1,
           lambda_q2, lambda_k2, diff_subln_g, mla_q_norm_g, w_mla_q_up, mla_kv_norm_g,
           w_mla_kv_up, w_branch_a, w_branch_b, w_out, rms_ffn_g, w_ffn_up, conv_w, conv_b,
           w_ffn_down):
    depth = w_in.shape[0]
    d_model = x_prompt.shape[-1]
    heads = w_branch_a.shape[1] // V_DIFF

    layers = []
    for l in range(depth):
        w_qkv, w_lat, w_gate = _prep_w_in(w_in[l], d_model, heads)
        layers.append((
            rms_attn_g[l], w_qkv, w_lat, w_gate,
            (lambda_q1[l], lambda_k1[l], lambda_q2[l], lambda_k2[l]), diff_subln_g[l],
            mla_q_norm_g[l], _prep_w_q_up(w_mla_q_up[l], heads),
            mla_kv_norm_g[l], _prep_w_kv_up(w_mla_kv_up[l], heads),
            w_branch_a[l].astype(BF16), w_branch_b[l].astype(BF16), w_out[l].astype(BF16),
            rms_ffn_g[l], w_ffn_up[l].astype(BF16), conv_w[l].astype(F32),
            conv_b[l].reshape(1, -1).astype(F32), w_ffn_down[l].astype(BF16)))

    def trunk(x):
        batch, seq, _ = x.shape
        y = x.reshape(batch * seq, d_model)
        for l in range(depth):
            y = _encoder_layer(y, batch, seq, l, rel_bias, layers[l])
        return rmsnorm(y, final_norm_g, x.dtype).reshape(x.shape)

    return (trunk(x_prompt), trunk(x_sample))
```

```python
import functools
import math

import jax
import jax.numpy as jnp
from jax import lax
from jax.experimental import pallas as pl
from jax.experimental.pallas import tpu as pltpu

DH_DIFF = 64
V_DIFF = 2 * DH_DIFF
QK_NOPE = 128
QK_ROPE = 64
V_MLA = 128
ROPE_THETA = 10000.0
N_BUCKETS = 32
MAX_DISTANCE = 128
EPS = 1e-6
LOG2E = math.log2(math.e)

LANE = 128
BF16_SUBLANE = 16
VMEM_LIMIT_BYTES = 56 * 1024 * 1024

MLA_QK_PAD = 2 * LANE
HALO = BF16_SUBLANE

TILES = dict(
    norm_tm=512,
    mm_tm=1024, mm_tn=512,
    lat_tm=512,
    qup_tm=1024, qup_tn=1024,
    kvup_tm=1024, kvup_tn=1024,
    ffn_tm=1024, ffn_tn=256,
    down_tm=512, down_tn=256,
    diff_tq=512, diff_tk=1024,
    mla_tq=1024, mla_tk=1024,
)

F32 = jnp.float32
BF16 = jnp.bfloat16


def _tile(name, dim):
    t = min(TILES[name], dim)
    assert dim % t == 0, (name, dim, t)
    return t


def _params(*sem):
    return pltpu.CompilerParams(dimension_semantics=sem, vmem_limit_bytes=VMEM_LIMIT_BYTES)


def _rms(x, g):
    return x * lax.rsqrt(jnp.mean(x * x, axis=-1, keepdims=True) + EPS) * g


def _rope128(p, c, s1, s2):
    return (p * c + pltpu.roll(p, LANE - QK_ROPE // 2, 1) * s1
            + pltpu.roll(p, QK_ROPE // 2, 1) * s2)


def _rmsnorm_kernel(x_ref, g_ref, o_ref):
    o_ref[...] = _rms(x_ref[...].astype(F32), g_ref[...]).astype(o_ref.dtype)


def rmsnorm(x, g, out_dtype):
    m, d = x.shape
    tm = _tile("norm_tm", m)
    return pl.pallas_call(
        _rmsnorm_kernel,
        grid=(m // tm,),
        in_specs=[pl.BlockSpec((tm, d), lambda i: (i, 0)),
                  pl.BlockSpec((1, d), lambda i: (0, 0))],
        out_specs=pl.BlockSpec((tm, d), lambda i: (i, 0)),
        out_shape=jax.ShapeDtypeStruct((m, d), out_dtype),
        compiler_params=_params("parallel"),
        name="rmsnorm",
    )(x, g.reshape(1, d).astype(F32))


def _mm_kernel(a_ref, b_ref, o_ref):
    o_ref[...] = jnp.dot(a_ref[...], b_ref[...], preferred_element_type=F32).astype(o_ref.dtype)


def matmul(a, b, out_dtype, tm, tn, name):
    m, k = a.shape
    n = b.shape[1]
    return pl.pallas_call(
        _mm_kernel,
        grid=(m // tm, n // tn),
        in_specs=[pl.BlockSpec((tm, k), lambda i, j: (i, 0)),
                  pl.BlockSpec((k, tn), lambda i, j: (0, j))],
        out_specs=pl.BlockSpec((tm, tn), lambda i, j: (i, j)),
        out_shape=jax.ShapeDtypeStruct((m, n), out_dtype),
        compiler_params=_params("parallel", "arbitrary"),
        name=name,
    )(a, b)


def _mm_res_kernel(a_ref, b_ref, r_ref, o_ref):
    o_ref[...] = r_ref[...] + jnp.dot(a_ref[...], b_ref[...], preferred_element_type=F32)


def matmul_residual(a, b, res, tm, tn, name):
    m, k = a.shape
    n = b.shape[1]
    return pl.pallas_call(
        _mm_res_kernel,
        grid=(m // tm, n // tn),
        in_specs=[pl.BlockSpec((tm, k), lambda i, j: (i, 0)),
                  pl.BlockSpec((k, tn), lambda i, j: (0, j)),
                  pl.BlockSpec((tm, tn), lambda i, j: (i, j))],
        out_specs=pl.BlockSpec((tm, tn), lambda i, j: (i, j)),
        out_shape=jax.ShapeDtypeStruct((m, n), F32),
        compiler_params=_params("parallel", "arbitrary"),
        name=name,
    )(a, b, res)


def _merge_kernel(a_ref, b_ref, wa_ref, wb_ref, ga_ref, gb_ref, o_ref):
    ya = jnp.dot(a_ref[...], wa_ref[...], preferred_element_type=F32)
    yb = jnp.dot(b_ref[...], wb_ref[...], preferred_element_type=F32)
    sa = 1.0 / (1.0 + jnp.exp(-ga_ref[...]))
    sb = 1.0 / (1.0 + jnp.exp(-gb_ref[...]))
    o_ref[...] = (sa * ya + sb * yb).astype(o_ref.dtype)


def gated_merge(a, b, wa, wb, gates, tm, tn):
    m, ka = a.shape
    kb = b.shape[1]
    n = wa.shape[1]
    nb = n // tn
    return pl.pallas_call(
        _merge_kernel,
        grid=(m // tm, nb),
        in_specs=[pl.BlockSpec((tm, ka), lambda i, j: (i, 0)),
                  pl.BlockSpec((tm, kb), lambda i, j: (i, 0)),
                  pl.BlockSpec((ka, tn), lambda i, j: (0, j)),
                  pl.BlockSpec((kb, tn), lambda i, j: (0, j)),
                  pl.BlockSpec((tm, tn), lambda i, j: (i, j)),
                  pl.BlockSpec((tm, tn), lambda i, j: (i, j + nb))],
        out_specs=pl.BlockSpec((tm, tn), lambda i, j: (i, j)),
        out_shape=jax.ShapeDtypeStruct((m, n), BF16),
        compiler_params=_params("parallel", "arbitrary"),
        name="gated_merge",
    )(a, b, wa, wb, gates, gates)


def _mla_q_kernel(lat_ref, g_ref, w_ref, c_ref, s1_ref, s2_ref, o_ref, a_sc, *, heads, scale):
    @pl.when(pl.program_id(1) == 0)
    def _():
        a_sc[...] = _rms(lat_ref[...], g_ref[...]).astype(BF16)

    acc = jnp.dot(a_sc[...], w_ref[...], preferred_element_type=F32)
    c, s1, s2 = c_ref[...], s1_ref[...], s2_ref[...]
    for h in range(heads):
        lo = h * MLA_QK_PAD
        o_ref[:, lo:lo + QK_NOPE] = (acc[:, lo:lo + QK_NOPE] * scale).astype(o_ref.dtype)
        rot = _rope128(acc[:, lo + QK_NOPE:lo + MLA_QK_PAD], c, s1, s2)
        o_ref[:, lo + QK_NOPE:lo + MLA_QK_PAD] = (rot * scale).astype(o_ref.dtype)


def mla_q_up(lat, g, w, rope_tabs, seq, q_lora):
    m = lat.shape[0]
    n = w.shape[1]
    tm = _tile("qup_tm", seq)
    tn = _tile("qup_tn", n)
    pos_blocks = seq // tm
    tab_spec = pl.BlockSpec((tm, LANE), lambda i, j: (i % pos_blocks, 0))
    kern = functools.partial(_mla_q_kernel, heads=tn // MLA_QK_PAD,
                             scale=(QK_NOPE + QK_ROPE) ** -0.5 * LOG2E)
    return pl.pallas_call(
        kern,
        grid=(m // tm, n // tn),
        in_specs=[pl.BlockSpec((tm, q_lora), lambda i, j: (i, 0)),
                  pl.BlockSpec((1, q_lora), lambda i, j: (0, 0)),
                  pl.BlockSpec((q_lora, tn), lambda i, j: (0, j)),
                  tab_spec, tab_spec, tab_spec],
        out_specs=pl.BlockSpec((tm, tn), lambda i, j: (i, j)),
        out_shape=jax.ShapeDtypeStruct((m, n), BF16),
        scratch_shapes=[pltpu.VMEM((tm, q_lora), BF16)],
        compiler_params=_params("parallel", "arbitrary"),
        name="mla_q_up",
    )(lat, g.reshape(1, q_lora).astype(F32), w, *rope_tabs)


def _mla_kv_kernel(kv_ref, pe_ref, g_ref, w_ref, c_ref, s1_ref, s2_ref, o_ref, a_sc, *, kv_lora):
    @pl.when(pl.program_id(1) == 0)
    def _():
        a_sc[:, :kv_lora] = _rms(kv_ref[...], g_ref[...]).astype(BF16)
        rot = _rope128(pe_ref[...], c_ref[...], s1_ref[...], s2_ref[...])
        a_sc[:, kv_lora:] = rot.astype(BF16)

    o_ref[...] = jnp.dot(a_sc[...], w_ref[...], preferred_element_type=F32).astype(o_ref.dtype)


def mla_kv_up(lat, g, w_ext, rope_tabs, seq, q_lora, kv_lora):
    m = lat.shape[0]
    n = w_ext.shape[1]
    tm = _tile("kvup_tm", seq)
    tn = _tile("kvup_tn", n)
    pos_blocks = seq // tm
    tab_spec = pl.BlockSpec((tm, LANE), lambda i, j: (i % pos_blocks, 0))
    kv_blk = q_lora // kv_lora
    pe_blk = (q_lora + kv_lora) // LANE
    return pl.pallas_call(
        functools.partial(_mla_kv_kernel, kv_lora=kv_lora),
        grid=(m // tm, n // tn),
        in_specs=[pl.BlockSpec((tm, kv_lora), lambda i, j: (i, kv_blk)),
                  pl.BlockSpec((tm, LANE), lambda i, j: (i, pe_blk)),
                  pl.BlockSpec((1, kv_lora), lambda i, j: (0, 0)),
                  pl.BlockSpec((kv_lora + LANE, tn), lambda i, j: (0, j)),
                  tab_spec, tab_spec, tab_spec],
        out_specs=pl.BlockSpec((tm, tn), lambda i, j: (i, j)),
        out_shape=jax.ShapeDtypeStruct((m, n), BF16),
        scratch_shapes=[pltpu.VMEM((tm, kv_lora + LANE), BF16)],
        compiler_params=_params("parallel", "arbitrary"),
        name="mla_kv_up",
    )(lat, lat, g.reshape(1, kv_lora).astype(F32), w_ext, *rope_tabs)


def _softmax_init(m_sc, acc_sc):
    m_sc[...] = jnp.full(m_sc.shape, -jnp.inf, F32)
    acc_sc[...] = jnp.zeros(acc_sc.shape, F32)


def _softmax_update(s, vx, m_sc, acc_sc, c=None):
    tk = s.shape[1]
    m_prev = m_sc[...]
    m_cur = jnp.max(s, axis=1, keepdims=True)
    if c is not None:
        m_cur = m_cur + c
    m_new = jnp.maximum(m_prev, m_cur)
    alpha = jnp.exp2(m_prev - m_new)
    shift = m_new if c is None else m_new - c
    p = jnp.exp2(s - pltpu.repeat(shift, tk // LANE, axis=1))
    acc_sc[...] = (pltpu.repeat(alpha, 2, axis=1) * acc_sc[...]
                   + jnp.dot(p.astype(BF16), vx, preferred_element_type=F32))
    m_sc[...] = m_new


def _extend_v(v_ref, vx_sc, tk):
    width = v_ref.shape[1]

    def body(j, carry):
        r = pl.ds(pl.multiple_of(j * tk, tk), tk)
        vx_sc[r, :width] = v_ref[r, :]
        vx_sc[r, width:] = jnp.ones((tk, vx_sc.shape[1] - width), vx_sc.dtype)
        return carry

    lax.fori_loop(0, v_ref.shape[0] // tk, body, 0)


def _qk(q, k):
    return lax.dot_general(q, k, (((1,), (1,)), ((), ())), preferred_element_type=F32)


def _chunk_loop(lo, hi, n_chunks, scores, s_bufs, consume):
    def body(j, carry):
        jn = jnp.minimum(j + 1, n_chunks - 1)
        for parity in (0, 1):
            @pl.when((j & 1) == parity)
            def _():
                s_bufs[1 - parity][...] = scores(jn)
                consume(j, s_bufs[parity])
        return carry

    lax.fori_loop(lo, hi, body, 0)


def _mla_attn_kernel(q_ref, k_ref, v_ref, o_ref, vx_sc, sa_sc, sb_sc, m_sc, acc_sc, *, tk):
    @pl.when(pl.program_id(2) == 0)
    def _():
        _extend_v(v_ref, vx_sc, tk)

    _softmax_init(m_sc, acc_sc)
    rows = lambda j: pl.ds(pl.multiple_of(j * tk, tk), tk)
    scores = lambda j: _qk(q_ref[...], k_ref[rows(j), :])

    def consume(j, s_ref):
        _softmax_update(s_ref[...], vx_sc[rows(j), :], m_sc, acc_sc)

    nk = k_ref.shape[0] // tk
    sa_sc[...] = scores(0)
    _chunk_loop(0, nk, nk, scores, (sa_sc, sb_sc), consume)
    acc = acc_sc[...]
    o_ref[...] = (acc[:, :V_MLA] / acc[:, V_MLA:]).astype(o_ref.dtype)


def mla_attention(q, kv, batch, seq, heads):
    tq = _tile("mla_tq", seq)
    tk = _tile("mla_tk", seq)
    nq = seq // tq
    v_blk0 = heads * MLA_QK_PAD // V_MLA
    return pl.pallas_call(
        functools.partial(_mla_attn_kernel, tk=tk),
        grid=(batch, heads, nq),
        in_specs=[pl.BlockSpec((tq, MLA_QK_PAD), lambda b, h, i: (b * nq + i, h)),
                  pl.BlockSpec((seq, MLA_QK_PAD), lambda b, h, i: (b, h)),
                  pl.BlockSpec((seq, V_MLA), lambda b, h, i: (b, v_blk0 + h))],
        out_specs=pl.BlockSpec((tq, V_MLA), lambda b, h, i: (b * nq + i, h)),
        out_shape=jax.ShapeDtypeStruct((batch * seq, heads * V_MLA), BF16),
        scratch_shapes=[pltpu.VMEM((seq, 2 * V_MLA), BF16),
                        pltpu.VMEM((tq, tk), F32), pltpu.VMEM((tq, tk), F32),
                        pltpu.VMEM((tq, LANE), F32), pltpu.VMEM((tq, 2 * V_MLA), F32)],
        compiler_params=_params("parallel", "parallel", "arbitrary"),
        name="mla_attention",
    )(q, kv, kv)


def _diff_attn_kernel(tab_ref, far_ref, ids_ref, q_ref, k_ref, v_ref, lq1_ref, lk1_ref, lq2_ref,
                      lk2_ref, g_ref, o_ref, q2_sc, vx_sc, bias_sc, sa_sc, sb_sc, m_sc, acc_sc, *, tq, tk,
                      step, near_off0, lambda_init):
    h, i = pl.program_id(1), pl.program_id(2)
    seq = k_ref.shape[0]
    nk = seq // tk

    @pl.when(i == 0)
    def _():
        _extend_v(v_ref, vx_sc, tk)
        for t in range(bias_sc.shape[0]):
            ids = ids_ref[t]
            f = jnp.zeros(ids.shape, F32)
            for b in range(N_BUCKETS):
                f = jnp.where(ids == b, tab_ref[h, b], f)
            f = f * LOG2E
            bias_sc[t] = pltpu.roll(jnp.broadcast_to(f, (tq, tq + tk)), tk + 1, 1,
                                    stride=1, stride_axis=0)[:, :tk]

    q = q_ref[...]
    lane = lax.broadcasted_iota(jnp.int32, q.shape, 1)
    zero = jnp.zeros_like(q)
    q2_sc[:tq] = jnp.where(lane < DH_DIFF, q, zero)
    q2_sc[tq:] = jnp.where(lane >= DH_DIFF, q, zero)
    _softmax_init(m_sc, acc_sc)

    rows = lambda j: pl.ds(pl.multiple_of(j * tk, tk), tk)
    scores = lambda j: _qk(q2_sc[...], k_ref[rows(j), :])

    def far(c):
        def consume(j, s_ref):
            _softmax_update(s_ref[...], vx_sc[rows(j), :], m_sc, acc_sc, c)
        return consume

    def near(j, s_ref):
        bias = bias_sc[(j * tk - i * tq - near_off0) // step]
        s = s_ref[...] + jnp.concatenate([bias, bias], axis=0)
        _softmax_update(s, vx_sc[rows(j), :], m_sc, acc_sc)

    j_near = jnp.maximum(i * tq - (MAX_DISTANCE - 1), 0) // tk
    j_right = jnp.minimum((i * tq + tq + MAX_DISTANCE - 1 + tk - 1) // tk, nk)
    s_bufs = (sa_sc, sb_sc)
    sa_sc[...] = scores(0)
    _chunk_loop(0, j_near, nk, scores, s_bufs, far(tab_ref[h, far_ref[0]] * LOG2E))
    _chunk_loop(j_near, j_right, nk, scores, s_bufs, near)
    _chunk_loop(j_right, nk, nk, scores, s_bufs, far(tab_ref[h, far_ref[1]] * LOG2E))

    acc = acc_sc[...]
    o = acc[:, :V_DIFF] / acc[:, V_DIFF:]
    lam = (jnp.exp(jnp.sum(lq1_ref[...] * lk1_ref[...], axis=1, keepdims=True))
           - jnp.exp(jnp.sum(lq2_ref[...] * lk2_ref[...], axis=1, keepdims=True)) + lambda_init)
    a = o[:tq] - lam * o[tq:]
    o_ref[...] = (_rms(a, g_ref[...]) * (1.0 - lambda_init)).astype(o_ref.dtype)


def _t5_bucket(rel):
    nb = N_BUCKETS // 2
    max_exact = nb // 2
    ret = (rel > 0).astype(jnp.int32) * nb
    n = jnp.abs(rel)
    nf = jnp.maximum(n, max_exact).astype(F32)
    large = max_exact + (jnp.log(nf / max_exact) / math.log(MAX_DISTANCE / max_exact)
                         * (nb - max_exact)).astype(jnp.int32)
    large = jnp.minimum(large, nb - 1)
    return ret + jnp.where(n < max_exact, n, large)


def diff_attention(qkv, rel_bias, lams, subln_g, lambda_init, batch, seq, heads):
    tq = _tile("diff_tq", seq)
    tk = _tile("diff_tk", seq)
    nq = seq // tq
    step = math.gcd(tq, tk)
    lo = -(tk - 1 + MAX_DISTANCE)
    near_off0 = (lo // step + 1) * step
    near_offs = list(range(near_off0, tq - 1 + MAX_DISTANCE, step))
    win = (jnp.array(near_offs, jnp.int32)[:, None] - (tq - 1)
           + jnp.arange(tq + tk, dtype=jnp.int32)[None, :])
    ids = _t5_bucket(win).reshape(len(near_offs), 1, tq + tk)
    far_ids = _t5_bucket(jnp.array([-MAX_DISTANCE, MAX_DISTANCE], jnp.int32))
    vec = lambda: pl.BlockSpec((1, DH_DIFF), lambda b, h, i: (0, 0))
    kern = functools.partial(_diff_attn_kernel, tq=tq, tk=tk, step=step, near_off0=near_off0,
                             lambda_init=lambda_init)
    return pl.pallas_call(
        kern,
        grid=(batch, heads, nq),
        in_specs=[pl.BlockSpec(memory_space=pltpu.SMEM),
                  pl.BlockSpec(memory_space=pltpu.SMEM),
                  pl.BlockSpec(ids.shape, lambda b, h, i: (0, 0, 0)),
                  pl.BlockSpec((tq, V_DIFF), lambda b, h, i: (b * nq + i, h)),
                  pl.BlockSpec((seq, V_DIFF), lambda b, h, i: (b, heads + h)),
                  pl.BlockSpec((seq, V_DIFF), lambda b, h, i: (b, 2 * heads + h)),
                  vec(), vec(), vec(), vec(),
                  pl.BlockSpec((1, V_DIFF), lambda b, h, i: (0, 0))],
        out_specs=pl.BlockSpec((tq, V_DIFF), lambda b, h, i: (b * nq + i, h)),
        out_shape=jax.ShapeDtypeStruct((batch * seq, heads * V_DIFF), BF16),
        scratch_shapes=[pltpu.VMEM((2 * tq, V_DIFF), BF16), pltpu.VMEM((seq, 2 * V_DIFF), BF16),
                        pltpu.VMEM((len(near_offs), tq, tk), F32),
                        pltpu.VMEM((2 * tq, tk), F32), pltpu.VMEM((2 * tq, tk), F32),
                        pltpu.VMEM((2 * tq, LANE), F32), pltpu.VMEM((2 * tq, 2 * V_DIFF), F32)],
        compiler_params=_params("parallel", "parallel", "arbitrary"),
        name="diff_attention",
    )(rel_bias.T.astype(F32), far_ids, ids, qkv, qkv, qkv,
      *[v.reshape(1, DH_DIFF).astype(F32) for v in lams], subln_g.reshape(1, V_DIFF).astype(F32))


def _ffn_up_kernel(a_ref, prev_ref, next_ref, wg_ref, wu_ref, cwg_ref, cwu_ref, cbg_ref, cbu_ref,
                   o_ref, a_sc, *, tm, tiles_per_seq):
    i = pl.program_id(0)

    @pl.when(pl.program_id(1) == 0)
    def _():
        pos = i % tiles_per_seq
        prev = prev_ref[...]
        nxt = next_ref[...]
        a_sc[:HALO] = jnp.where(pos == 0, jnp.zeros_like(prev), prev)
        a_sc[HALO:HALO + tm] = a_ref[...]
        a_sc[HALO + tm:] = jnp.where(pos == tiles_per_seq - 1, jnp.zeros_like(nxt), nxt)

    def conv(w_ref, cw_ref, cb_ref):
        r = jnp.dot(a_sc[...], w_ref[...], preferred_element_type=F32)
        rows = r.shape[0]
        r_prev = pltpu.roll(r, 1, 0)[HALO:HALO + tm]
        r_next = pltpu.roll(r, rows - 1, 0)[HALO:HALO + tm]
        cw = cw_ref[...]
        return (r_prev * cw[0:1] + r[HALO:HALO + tm] * cw[1:2] + r_next * cw[2:3]) + cb_ref[...]

    gate = conv(wg_ref, cwg_ref, cbg_ref)
    up = conv(wu_ref, cwu_ref, cbu_ref)
    o_ref[...] = (gate * (1.0 / (1.0 + jnp.exp(-gate))) * up).astype(o_ref.dtype)


def ffn_up(h, w_up, conv_w, conv_b, seq):
    m, d = h.shape
    d_ff = w_up.shape[1] // 2
    tm = _tile("ffn_tm", seq)
    tn = _tile("ffn_tn", d_ff)
    nb = d_ff // tn
    hb = tm // HALO
    last_hb = m // HALO - 1
    kern = functools.partial(_ffn_up_kernel, tm=tm, tiles_per_seq=seq // tm)
    return pl.pallas_call(
        kern,
        grid=(m // tm, nb),
        in_specs=[pl.BlockSpec((tm, d), lambda i, j: (i, 0)),
                  pl.BlockSpec((HALO, d), lambda i, j: (jnp.maximum(i * hb - 1, 0), 0)),
                  pl.BlockSpec((HALO, d), lambda i, j: (jnp.minimum((i + 1) * hb, last_hb), 0)),
                  pl.BlockSpec((d, tn), lambda i, j: (0, j)),
                  pl.BlockSpec((d, tn), lambda i, j: (0, j + nb)),
                  pl.BlockSpec((3, tn), lambda i, j: (0, j)),
                  pl.BlockSpec((3, tn), lambda i, j: (0, j + nb)),
                  pl.BlockSpec((1, tn), lambda i, j: (0, j)),
                  pl.BlockSpec((1, tn), lambda i, j: (0, j + nb))],
        out_specs=pl.BlockSpec((tm, tn), lambda i, j: (i, j)),
        out_shape=jax.ShapeDtypeStruct((m, d_ff), BF16),
        scratch_shapes=[pltpu.VMEM((tm + 2 * HALO, d), BF16)],
        compiler_params=_params("parallel", "arbitrary"),
        name="ffn_up_conv_silu",
    )(h, h, h, w_up, w_up, conv_w, conv_w, conv_b, conv_b)


def _prep_w_in(w_in, d_model, heads):
    dq = heads * 2 * DH_DIFF
    dv = heads * V_DIFF
    qkv_cols = 2 * dq + dv
    lat_cols = w_in.shape[1] - qkv_cols - 2 * d_model
    lat_pad = -lat_cols % LANE
    col_scale = jnp.concatenate([jnp.full((dq,), DH_DIFF ** -0.5 * LOG2E, F32),
                                 jnp.ones((dq + dv,), F32)])
    w_qkv = (w_in[:, :qkv_cols] * col_scale).astype(BF16)
    w_lat = jnp.pad(w_in[:, qkv_cols:qkv_cols + lat_cols], ((0, 0), (0, lat_pad))).astype(BF16)
    w_gate = w_in[:, qkv_cols + lat_cols:].astype(BF16)
    return w_qkv, w_lat, w_gate


def _prep_w_q_up(w, heads):
    k = w.shape[0]
    w = w.reshape(k, heads, QK_NOPE + QK_ROPE)
    w = jnp.pad(w, ((0, 0), (0, 0), (0, MLA_QK_PAD - QK_NOPE - QK_ROPE)))
    return w.reshape(k, heads * MLA_QK_PAD).astype(BF16)


def _prep_w_kv_up(w, heads):
    kv = w.shape[0]
    w = w.reshape(kv, heads, QK_NOPE + V_MLA)
    top_k = jnp.pad(w[:, :, :QK_NOPE], ((0, 0), (0, 0), (0, MLA_QK_PAD - QK_NOPE)))
    eye = jnp.pad(jnp.eye(QK_ROPE, dtype=F32), ((0, LANE - QK_ROPE), (QK_NOPE, MLA_QK_PAD - QK_NOPE - QK_ROPE)))
    bot_k = jnp.broadcast_to(eye[:, None, :], (LANE, heads, MLA_QK_PAD))
    w_k = jnp.concatenate([top_k, bot_k], axis=0).reshape(kv + LANE, heads * MLA_QK_PAD)
    w_v = jnp.pad(w[:, :, QK_NOPE:].reshape(kv, heads * V_MLA), ((0, LANE), (0, 0)))
    return jnp.concatenate([w_k, w_v], axis=1).astype(BF16)


def _rope_tables(seq):
    pos = jnp.arange(seq, dtype=F32)
    inv_freq = ROPE_THETA ** (-jnp.arange(0, QK_ROPE, 2, dtype=F32) / QK_ROPE)
    ang = pos[:, None] * inv_freq[None, :]
    cos, sin = jnp.cos(ang), jnp.sin(ang)
    half = QK_ROPE // 2
    z = lambda n: jnp.zeros((seq, n), F32)
    c = jnp.concatenate([cos, cos, z(LANE - QK_ROPE)], axis=1)
    s1 = jnp.concatenate([-sin, z(LANE - half)], axis=1)
    s2 = jnp.concatenate([z(half), sin, z(LANE - QK_ROPE)], axis=1)
    return c, s1, s2


def _encoder_layer(x, batch, seq, layer_idx, rel_bias, wts):
    (rms_attn_g, w_qkv, w_lat, w_gate, lams, diff_subln_g, mla_q_norm_g, w_q_up, mla_kv_norm_g,
     w_kv_ext, w_branch_a, w_branch_b, w_out, rms_ffn_g, w_ffn_up, conv_w, conv_b, w_ffn_down) = wts
    m, d_model = x.shape
    heads = w_branch_a.shape[0] // V_DIFF
    q_lora = mla_q_norm_g.shape[0]
    kv_lora = mla_kv_norm_g.shape[0]
    assert q_lora % kv_lora == 0 and (q_lora + kv_lora) % LANE == 0
    lambda_init = 0.8 - 0.6 * math.exp(-0.3 * layer_idx)
    tm = _tile("mm_tm", seq)

    h = rmsnorm(x, rms_attn_g, BF16)
    qkv = matmul(h, w_qkv, BF16, tm, _tile("mm_tn", w_qkv.shape[1]), "in_proj_qkv")
    lat = matmul(h, w_lat, F32, _tile("lat_tm", seq), w_lat.shape[1], "in_proj_latent")
    gates = matmul(h, w_gate, F32, tm, _tile("mm_tn", d_model), "in_proj_gates")

    a_out = diff_attention(qkv, rel_bias, lams, diff_subln_g, lambda_init, batch, seq, heads)

    rope_tabs = _rope_tables(seq)
    q_mla = mla_q_up(lat, mla_q_norm_g, w_q_up, rope_tabs, seq, q_lora)
    kv_mla = mla_kv_up(lat, mla_kv_norm_g, w_kv_ext, rope_tabs, seq, q_lora, kv_lora)
    b_out = mla_attention(q_mla, kv_mla, batch, seq, heads)

    merged = gated_merge(a_out, b_out, w_branch_a, w_branch_b, gates, tm, _tile("mm_tn", d_model))
    x = matmul_residual(merged, w_out, x, tm, _tile("mm_tn", d_model), "out_proj")

    h2 = rmsnorm(x, rms_ffn_g, BF16)
    act = ffn_up(h2, w_ffn_up, conv_w, conv_b, seq)
    return matmul_residual(act, w_ffn_down, x, _tile("down_tm", seq), _tile("down_tn", d_model),
                           "ffn_down")


def kernel(x_prompt, x_sample, rel_bias, final_norm_g, rms_attn_g, w_in, lambda_q1, lambda_k1,
           lambda_q2, lambda_k2, diff_subln_g, mla_q_norm_g, w_mla_q_up, mla_kv_norm_g,
           w_mla_kv_up, w_branch_a, w_branch_b, w_out, rms_ffn_g, w_ffn_up, conv_w, conv_b,
           w_ffn_down):
    depth = w_in.shape[0]
    d_model = x_prompt.shape[-1]
    heads = w_branch_a.shape[1] // V_DIFF

    layers = []
    for l in range(depth):
        w_qkv, w_lat, w_gate = _prep_w_in(w_in[l], d_model, heads)
        layers.append((
            rms_attn_g[l], w_qkv, w_lat, w_gate,
            (lambda_q1[l], lambda_k1[l], lambda_q2[l], lambda_k2[l]), diff_subln_g[l],
            mla_q_norm_g[l], _prep_w_q_up(w_mla_q_up[l], heads),
            mla_kv_norm_g[l], _prep_w_kv_up(w_mla_kv_up[l], heads),
            w_branch_a[l].astype(BF16), w_branch_b[l].astype(BF16), w_out[l].astype(BF16),
            rms_ffn_g[l], w_ffn_up[l].astype(BF16), conv_w[l].astype(F32),
            conv_b[l].reshape(1, -1).astype(F32), w_ffn_down[l].astype(BF16)))

    def trunk(x):
        batch, seq, _ = x.shape
        y = x.reshape(batch * seq, d_model)
        for l in range(depth):
            y = _encoder_layer(y, batch, seq, l, rel_bias, layers[l])
        return rmsnorm(y, final_norm_g, x.dtype).reshape(x.shape)

    return (trunk(x_prompt), trunk(x_sample))
```

```python
import functools
import math

import jax
import jax.numpy as jnp
from jax import lax
from jax.experimental import pallas as pl
from jax.experimental.pallas import tpu as pltpu

DH_DIFF = 64
V_DIFF = 2 * DH_DIFF
QK_NOPE = 128
QK_ROPE = 64
V_MLA = 128
ROPE_THETA = 10000.0
N_BUCKETS = 32
MAX_DISTANCE = 128
EPS = 1e-6
LOG2E = math.log2(math.e)

LANE = 128
BF16_SUBLANE = 16
VMEM_LIMIT_BYTES = 56 * 1024 * 1024

MLA_QK_PAD = 2 * LANE
HALO = BF16_SUBLANE
CONV_ROWS = 64

TILES = dict(
    norm_tm=512,
    mm_tm=1024, mm_tn=512,
    lat_tm=512,
    qup_tm=1024, qup_tn=1024,
    kvup_tm=1024, kvup_tn=1024,
    ffn_tm=1024, ffn_tn=256,
    down_tm=512, down_tn=256,
    diff_tq=512, diff_tk=1024,
    mla_tq=2048, mla_tk=1024,
)

F32 = jnp.float32
BF16 = jnp.bfloat16


def _tile(name, dim):
    t = min(TILES[name], dim)
    assert dim % t == 0, (name, dim, t)
    return t


def _params(*sem):
    return pltpu.CompilerParams(dimension_semantics=sem, vmem_limit_bytes=VMEM_LIMIT_BYTES)


def _rms(x, g):
    return x * lax.rsqrt(jnp.mean(x * x, axis=-1, keepdims=True) + EPS) * g


def _rope128(p, c, s1, s2):
    return (p * c + pltpu.roll(p, LANE - QK_ROPE // 2, 1) * s1
            + pltpu.roll(p, QK_ROPE // 2, 1) * s2)


def _rmsnorm_kernel(x_ref, g_ref, o_ref):
    o_ref[...] = _rms(x_ref[...].astype(F32), g_ref[...]).astype(o_ref.dtype)


def rmsnorm(x, g, out_dtype):
    m, d = x.shape
    tm = _tile("norm_tm", m)
    return pl.pallas_call(
        _rmsnorm_kernel,
        grid=(m // tm,),
        in_specs=[pl.BlockSpec((tm, d), lambda i: (i, 0)),
                  pl.BlockSpec((1, d), lambda i: (0, 0))],
        out_specs=pl.BlockSpec((tm, d), lambda i: (i, 0)),
        out_shape=jax.ShapeDtypeStruct((m, d), out_dtype),
        compiler_params=_params("parallel"),
        name="rmsnorm",
    )(x, g.reshape(1, d).astype(F32))


def _mm_kernel(a_ref, b_ref, o_ref):
    o_ref[...] = jnp.dot(a_ref[...], b_ref[...], preferred_element_type=F32).astype(o_ref.dtype)


def matmul(a, b, out_dtype, tm, tn, name):
    m, k = a.shape
    n = b.shape[1]
    return pl.pallas_call(
        _mm_kernel,
        grid=(m // tm, n // tn),
        in_specs=[pl.BlockSpec((tm, k), lambda i, j: (i, 0)),
                  pl.BlockSpec((k, tn), lambda i, j: (0, j))],
        out_specs=pl.BlockSpec((tm, tn), lambda i, j: (i, j)),
        out_shape=jax.ShapeDtypeStruct((m, n), out_dtype),
        compiler_params=_params("parallel", "arbitrary"),
        name=name,
    )(a, b)


def _mm_res_kernel(a_ref, b_ref, r_ref, o_ref):
    o_ref[...] = r_ref[...] + jnp.dot(a_ref[...], b_ref[...], preferred_element_type=F32)


def matmul_residual(a, b, res, tm, tn, name):
    m, k = a.shape
    n = b.shape[1]
    return pl.pallas_call(
        _mm_res_kernel,
        grid=(m // tm, n // tn),
        in_specs=[pl.BlockSpec((tm, k), lambda i, j: (i, 0)),
                  pl.BlockSpec((k, tn), lambda i, j: (0, j)),
                  pl.BlockSpec((tm, tn), lambda i, j: (i, j))],
        out_specs=pl.BlockSpec((tm, tn), lambda i, j: (i, j)),
        out_shape=jax.ShapeDtypeStruct((m, n), F32),
        compiler_params=_params("parallel", "arbitrary"),
        name=name,
    )(a, b, res)


def _merge_kernel(a_ref, b_ref, wa_ref, wb_ref, ga_ref, gb_ref, o_ref):
    ya = jnp.dot(a_ref[...], wa_ref[...], preferred_element_type=F32)
    yb = jnp.dot(b_ref[...], wb_ref[...], preferred_element_type=F32)
    sa = 1.0 / (1.0 + jnp.exp(-ga_ref[...]))
    sb = 1.0 / (1.0 + jnp.exp(-gb_ref[...]))
    o_ref[...] = (sa * ya + sb * yb).astype(o_ref.dtype)


def gated_merge(a, b, wa, wb, gates, tm, tn):
    m, ka = a.shape
    kb = b.shape[1]
    n = wa.shape[1]
    nb = n // tn
    return pl.pallas_call(
        _merge_kernel,
        grid=(m // tm, nb),
        in_specs=[pl.BlockSpec((tm, ka), lambda i, j: (i, 0)),
                  pl.BlockSpec((tm, kb), lambda i, j: (i, 0)),
                  pl.BlockSpec((ka, tn), lambda i, j: (0, j)),
                  pl.BlockSpec((kb, tn), lambda i, j: (0, j)),
                  pl.BlockSpec((tm, tn), lambda i, j: (i, j)),
                  pl.BlockSpec((tm, tn), lambda i, j: (i, j + nb))],
        out_specs=pl.BlockSpec((tm, tn), lambda i, j: (i, j)),
        out_shape=jax.ShapeDtypeStruct((m, n), BF16),
        compiler_params=_params("parallel", "arbitrary"),
        name="gated_merge",
    )(a, b, wa, wb, gates, gates)


def _mla_q_kernel(lat_ref, g_ref, w_ref, c_ref, s1_ref, s2_ref, o_ref, a_sc, *, heads, scale):
    @pl.when(pl.program_id(1) == 0)
    def _():
        a_sc[...] = _rms(lat_ref[...], g_ref[...]).astype(BF16)

    acc = jnp.dot(a_sc[...], w_ref[...], preferred_element_type=F32)
    c, s1, s2 = c_ref[...], s1_ref[...], s2_ref[...]
    for h in range(heads):
        lo = h * MLA_QK_PAD
        o_ref[:, lo:lo + QK_NOPE] = (acc[:, lo:lo + QK_NOPE] * scale).astype(o_ref.dtype)
        rot = _rope128(acc[:, lo + QK_NOPE:lo + MLA_QK_PAD], c, s1, s2)
        o_ref[:, lo + QK_NOPE:lo + MLA_QK_PAD] = (rot * scale).astype(o_ref.dtype)


def mla_q_up(lat, g, w, rope_tabs, seq, q_lora):
    m = lat.shape[0]
    n = w.shape[1]
    tm = _tile("qup_tm", seq)
    tn = _tile("qup_tn", n)
    pos_blocks = seq // tm
    tab_spec = pl.BlockSpec((tm, LANE), lambda i, j: (i % pos_blocks, 0))
    kern = functools.partial(_mla_q_kernel, heads=tn // MLA_QK_PAD,
                             scale=(QK_NOPE + QK_ROPE) ** -0.5 * LOG2E)
    return pl.pallas_call(
        kern,
        grid=(m // tm, n // tn),
        in_specs=[pl.BlockSpec((tm, q_lora), lambda i, j: (i, 0)),
                  pl.BlockSpec((1, q_lora), lambda i, j: (0, 0)),
                  pl.BlockSpec((q_lora, tn), lambda i, j: (0, j)),
                  tab_spec, tab_spec, tab_spec],
        out_specs=pl.BlockSpec((tm, tn), lambda i, j: (i, j)),
        out_shape=jax.ShapeDtypeStruct((m, n), BF16),
        scratch_shapes=[pltpu.VMEM((tm, q_lora), BF16)],
        compiler_params=_params("parallel", "arbitrary"),
        name="mla_q_up",
    )(lat, g.reshape(1, q_lora).astype(F32), w, *rope_tabs)


def _mla_kv_kernel(kv_ref, pe_ref, g_ref, w_ref, c_ref, s1_ref, s2_ref, o_ref, a_sc, *, kv_lora):
    @pl.when(pl.program_id(1) == 0)
    def _():
        a_sc[:, :kv_lora] = _rms(kv_ref[...], g_ref[...]).astype(BF16)
        rot = _rope128(pe_ref[...], c_ref[...], s1_ref[...], s2_ref[...])
        a_sc[:, kv_lora:] = rot.astype(BF16)

    o_ref[...] = jnp.dot(a_sc[...], w_ref[...], preferred_element_type=F32).astype(o_ref.dtype)


def mla_kv_up(lat, g, w_ext, rope_tabs, seq, q_lora, kv_lora):
    m = lat.shape[0]
    n = w_ext.shape[1]
    tm = _tile("kvup_tm", seq)
    tn = _tile("kvup_tn", n)
    pos_blocks = seq // tm
    tab_spec = pl.BlockSpec((tm, LANE), lambda i, j: (i % pos_blocks, 0))
    kv_blk = q_lora // kv_lora
    pe_blk = (q_lora + kv_lora) // LANE
    return pl.pallas_call(
        functools.partial(_mla_kv_kernel, kv_lora=kv_lora),
        grid=(m // tm, n // tn),
        in_specs=[pl.BlockSpec((tm, kv_lora), lambda i, j: (i, kv_blk)),
                  pl.BlockSpec((tm, LANE), lambda i, j: (i, pe_blk)),
                  pl.BlockSpec((1, kv_lora), lambda i, j: (0, 0)),
                  pl.BlockSpec((kv_lora + LANE, tn), lambda i, j: (0, j)),
                  tab_spec, tab_spec, tab_spec],
        out_specs=pl.BlockSpec((tm, tn), lambda i, j: (i, j)),
        out_shape=jax.ShapeDtypeStruct((m, n), BF16),
        scratch_shapes=[pltpu.VMEM((tm, kv_lora + LANE), BF16)],
        compiler_params=_params("parallel", "arbitrary"),
        name="mla_kv_up",
    )(lat, lat, g.reshape(1, kv_lora).astype(F32), w_ext, *rope_tabs)


def _lane_tile(x, n):
    return jnp.concatenate([x] * n, axis=1)


def _softmax_init(m_sc, acc_sc):
    m_sc[...] = jnp.full(m_sc.shape, -jnp.inf, F32)
    acc_sc[...] = jnp.zeros(acc_sc.shape, F32)


def _softmax_update(s, vx, m_sc, acc_sc, c=None):
    tk = s.shape[1]
    m_prev = m_sc[...]
    m_cur = jnp.max(s, axis=1, keepdims=True)
    if c is not None:
        m_cur = m_cur + c
    m_new = jnp.maximum(m_prev, m_cur)
    alpha = jnp.exp2(m_prev - m_new)
    shift = m_new if c is None else m_new - c
    p = jnp.exp2(s - _lane_tile(shift, tk // LANE))
    acc_sc[...] = (_lane_tile(alpha, 2) * acc_sc[...]
                   + jnp.dot(p.astype(BF16), vx, preferred_element_type=F32))
    m_sc[...] = m_new


def _extend_v(v_ref, vx_sc, tk):
    width = v_ref.shape[1]

    def body(j, carry):
        r = pl.ds(pl.multiple_of(j * tk, tk), tk)
        vx_sc[r, :width] = v_ref[r, :]
        vx_sc[r, width:] = jnp.ones((tk, vx_sc.shape[1] - width), vx_sc.dtype)
        return carry

    lax.fori_loop(0, v_ref.shape[0] // tk, body, 0)


def _qk(q, k):
    return lax.dot_general(q, k, (((1,), (1,)), ((), ())), preferred_element_type=F32)


def _chunk_loop(lo, hi, n_chunks, scores, s_bufs, consume):
    last = n_chunks - 1

    def by_parity(j, fn):
        for parity in (0, 1):
            @pl.when((j & 1) == parity)
            def _():
                fn(j, s_bufs[parity], s_bufs[1 - parity])

    def one(j, cur, nxt):
        nxt[...] = scores(jnp.minimum(j + 1, last))
        consume(j, cur)

    def two(j, cur, nxt):
        nxt[...] = scores(j + 1)
        consume(j, cur)
        cur[...] = scores(jnp.minimum(j + 2, last))
        consume(j + 1, nxt)

    pairs = lax.shift_right_logical(jnp.maximum(hi - lo, 0), 1)

    def body(t, carry):
        by_parity(lo + 2 * t, two)
        return carry

    lax.fori_loop(0, pairs, body, 0)
    rest = lo + 2 * pairs

    @pl.when(rest < hi)
    def _():
        by_parity(rest, one)


def _mla_attn_kernel(q_ref, k_ref, v_ref, o_ref, vx_sc, sa_sc, sb_sc, m_sc, acc_sc, *, tk):
    @pl.when(pl.program_id(2) == 0)
    def _():
        _extend_v(v_ref, vx_sc, tk)

    _softmax_init(m_sc, acc_sc)
    rows = lambda j: pl.ds(pl.multiple_of(j * tk, tk), tk)
    scores = lambda j: _qk(q_ref[...], k_ref[rows(j), :])

    def consume(j, s_ref):
        _softmax_update(s_ref[...], vx_sc[rows(j), :], m_sc, acc_sc)

    nk = k_ref.shape[0] // tk
    sa_sc[...] = scores(0)
    _chunk_loop(0, nk, nk, scores, (sa_sc, sb_sc), consume)
    acc = acc_sc[...]
    o_ref[...] = (acc[:, :V_MLA] / acc[:, V_MLA:]).astype(o_ref.dtype)


def mla_attention(q, kv, batch, seq, heads):
    tq = _tile("mla_tq", seq)
    tk = _tile("mla_tk", seq)
    nq = seq // tq
    v_blk0 = heads * MLA_QK_PAD // V_MLA
    return pl.pallas_call(
        functools.partial(_mla_attn_kernel, tk=tk),
        grid=(batch, heads, nq),
        in_specs=[pl.BlockSpec((tq, MLA_QK_PAD), lambda b, h, i: (b * nq + i, h)),
                  pl.BlockSpec((seq, MLA_QK_PAD), lambda b, h, i: (b, h)),
                  pl.BlockSpec((seq, V_MLA), lambda b, h, i: (b, v_blk0 + h))],
        out_specs=pl.BlockSpec((tq, V_MLA), lambda b, h, i: (b * nq + i, h)),
        out_shape=jax.ShapeDtypeStruct((batch * seq, heads * V_MLA), BF16),
        scratch_shapes=[pltpu.VMEM((seq, 2 * V_MLA), BF16),
                        pltpu.VMEM((tq, tk), F32), pltpu.VMEM((tq, tk), F32),
                        pltpu.VMEM((tq, LANE), F32), pltpu.VMEM((tq, 2 * V_MLA), F32)],
        compiler_params=_params("parallel", "parallel", "arbitrary"),
        name="mla_attention",
    )(q, kv, kv)


def _diff_attn_kernel(tab_ref, far_ref, ids_ref, q_ref, k_ref, v_ref, lq1_ref, lk1_ref, lq2_ref,
                      lk2_ref, g_ref, o_ref, q2_sc, vx_sc, bias_sc, sa_sc, sb_sc, m_sc, acc_sc, *, tq, tk,
                      step, near_off0, lambda_init):
    h, i = pl.program_id(1), pl.program_id(2)
    seq = k_ref.shape[0]
    nk = seq // tk

    @pl.when(i == 0)
    def _():
        _extend_v(v_ref, vx_sc, tk)
        for t in range(bias_sc.shape[0]):
            ids = ids_ref[t]
            f = jnp.zeros(ids.shape, F32)
            for b in range(N_BUCKETS):
                f = jnp.where(ids == b, tab_ref[h, b], f)
            f = f * LOG2E
            bias_sc[t] = pltpu.roll(jnp.broadcast_to(f, (tq, tq + tk)), tk + 1, 1,
                                    stride=1, stride_axis=0)[:, :tk]

    q = q_ref[...]
    lane = lax.broadcasted_iota(jnp.int32, q.shape, 1)
    zero = jnp.zeros_like(q)
    q2_sc[:tq] = jnp.where(lane < DH_DIFF, q, zero)
    q2_sc[tq:] = jnp.where(lane >= DH_DIFF, q, zero)
    _softmax_init(m_sc, acc_sc)

    rows = lambda j: pl.ds(pl.multiple_of(j * tk, tk), tk)
    scores = lambda j: _qk(q2_sc[...], k_ref[rows(j), :])

    def far(c):
        def consume(j, s_ref):
            _softmax_update(s_ref[...], vx_sc[rows(j), :], m_sc, acc_sc, c)
        return consume

    def near(j, s_ref):
        bias = bias_sc[(j * tk - i * tq - near_off0) // step]
        s = s_ref[...] + jnp.concatenate([bias, bias], axis=0)
        _softmax_update(s, vx_sc[rows(j), :], m_sc, acc_sc)

    j_near = jnp.maximum(i * tq - (MAX_DISTANCE - 1), 0) // tk
    j_right = jnp.minimum((i * tq + tq + MAX_DISTANCE - 1 + tk - 1) // tk, nk)
    s_bufs = (sa_sc, sb_sc)
    sa_sc[...] = scores(0)
    _chunk_loop(0, j_near, nk, scores, s_bufs, far(tab_ref[h, far_ref[0]] * LOG2E))
    _chunk_loop(j_near, j_right, nk, scores, s_bufs, near)
    _chunk_loop(j_right, nk, nk, scores, s_bufs, far(tab_ref[h, far_ref[1]] * LOG2E))

    acc = acc_sc[...]
    o = acc[:, :V_DIFF] / acc[:, V_DIFF:]
    lam = (jnp.exp(jnp.sum(lq1_ref[...] * lk1_ref[...], axis=1, keepdims=True))
           - jnp.exp(jnp.sum(lq2_ref[...] * lk2_ref[...], axis=1, keepdims=True)) + lambda_init)
    a = o[:tq] - lam * o[tq:]
    o_ref[...] = (_rms(a, g_ref[...]) * (1.0 - lambda_init)).astype(o_ref.dtype)


def _t5_bucket(rel):
    nb = N_BUCKETS // 2
    max_exact = nb // 2
    ret = (rel > 0).astype(jnp.int32) * nb
    n = jnp.abs(rel)
    nf = jnp.maximum(n, max_exact).astype(F32)
    large = max_exact + (jnp.log(nf / max_exact) / math.log(MAX_DISTANCE / max_exact)
                         * (nb - max_exact)).astype(jnp.int32)
    large = jnp.minimum(large, nb - 1)
    return ret + jnp.where(n < max_exact, n, large)


def diff_attention(qkv, rel_bias, lams, subln_g, lambda_init, batch, seq, heads):
    tq = _tile("diff_tq", seq)
    tk = _tile("diff_tk", seq)
    nq = seq // tq
    step = math.gcd(tq, tk)
    lo = -(tk - 1 + MAX_DISTANCE)
    near_off0 = (lo // step + 1) * step
    near_offs = list(range(near_off0, tq - 1 + MAX_DISTANCE, step))
    win = (jnp.array(near_offs, jnp.int32)[:, None] - (tq - 1)
           + jnp.arange(tq + tk, dtype=jnp.int32)[None, :])
    ids = _t5_bucket(win).reshape(len(near_offs), 1, tq + tk)
    far_ids = _t5_bucket(jnp.array([-MAX_DISTANCE, MAX_DISTANCE], jnp.int32))
    vec = lambda: pl.BlockSpec((1, DH_DIFF), lambda b, h, i: (0, 0))
    kern = functools.partial(_diff_attn_kernel, tq=tq, tk=tk, step=step, near_off0=near_off0,
                             lambda_init=lambda_init)
    return pl.pallas_call(
        kern,
        grid=(batch, heads, nq),
        in_specs=[pl.BlockSpec(memory_space=pltpu.SMEM),
                  pl.BlockSpec(memory_space=pltpu.SMEM),
                  pl.BlockSpec(ids.shape, lambda b, h, i: (0, 0, 0)),
                  pl.BlockSpec((tq, V_DIFF), lambda b, h, i: (b * nq + i, h)),
                  pl.BlockSpec((seq, V_DIFF), lambda b, h, i: (b, heads + h)),
                  pl.BlockSpec((seq, V_DIFF), lambda b, h, i: (b, 2 * heads + h)),
                  vec(), vec(), vec(), vec(),
                  pl.BlockSpec((1, V_DIFF), lambda b, h, i: (0, 0))],
        out_specs=pl.BlockSpec((tq, V_DIFF), lambda b, h, i: (b * nq + i, h)),
        out_shape=jax.ShapeDtypeStruct((batch * seq, heads * V_DIFF), BF16),
        scratch_shapes=[pltpu.VMEM((2 * tq, V_DIFF), BF16), pltpu.VMEM((seq, 2 * V_DIFF), BF16),
                        pltpu.VMEM((len(near_offs), tq, tk), F32),
                        pltpu.VMEM((2 * tq, tk), F32), pltpu.VMEM((2 * tq, tk), F32),
                        pltpu.VMEM((2 * tq, LANE), F32), pltpu.VMEM((2 * tq, 2 * V_DIFF), F32)],
        compiler_params=_params("parallel", "parallel", "arbitrary"),
        name="diff_attention",
    )(rel_bias.T.astype(F32), far_ids, ids, qkv, qkv, qkv,
      *[v.reshape(1, DH_DIFF).astype(F32) for v in lams], subln_g.reshape(1, V_DIFF).astype(F32))


def _ffn_up_kernel(a_ref, prev_ref, next_ref, wg_ref, wu_ref, cwg_ref, cwu_ref, cbg_ref, cbu_ref,
                   o_ref, a_sc, rg0_sc, ru0_sc, rg1_sc, ru1_sc, *, tm, tiles_per_seq):
    i, j = pl.program_id(0), pl.program_id(1)

    @pl.when(j == 0)
    def _():
        pos = i % tiles_per_seq
        prev = prev_ref[...]
        nxt = next_ref[...]
        a_sc[:HALO] = jnp.where(pos == 0, jnp.zeros_like(prev), prev)
        a_sc[HALO:HALO + tm] = a_ref[...]
        a_sc[HALO + tm:] = jnp.where(pos == tiles_per_seq - 1, jnp.zeros_like(nxt), nxt)
        rg1_sc[...] = jnp.zeros(rg1_sc.shape, F32)
        ru1_sc[...] = jnp.zeros(ru1_sc.shape, F32)

    def conv(r_ref, cw, cb, lo):
        taps = [r_ref[pl.ds(HALO + lo + d, CONV_ROWS), :] for d in (-1, 0, 1)]
        return (taps[0] * cw[0:1] + taps[1] * cw[1:2] + taps[2] * cw[2:3]) + cb

    def step(rg_new, ru_new, rg_old, ru_old):
        cwg, cwu, cbg, cbu = cwg_ref[...], cwu_ref[...], cbg_ref[...], cbu_ref[...]
        for lo in range(0, tm, CONV_ROWS):
            gate = conv(rg_old, cwg, cbg, lo)
            up = conv(ru_old, cwu, cbu, lo)
            o_ref[lo:lo + CONV_ROWS, :] = (gate * (1.0 / (1.0 + jnp.exp(-gate))) * up).astype(o_ref.dtype)
        rg_new[...] = jnp.dot(a_sc[...], wg_ref[...], preferred_element_type=F32)
        ru_new[...] = jnp.dot(a_sc[...], wu_ref[...], preferred_element_type=F32)

    @pl.when((j & 1) == 0)
    def _():
        step(rg0_sc, ru0_sc, rg1_sc, ru1_sc)

    @pl.when((j & 1) == 1)
    def _():
        step(rg1_sc, ru1_sc, rg0_sc, ru0_sc)


def ffn_up(h, w_up, conv_w, conv_b, seq):
    m, d = h.shape
    d_ff = w_up.shape[1] // 2
    tm = _tile("ffn_tm", seq)
    tn = _tile("ffn_tn", d_ff)
    assert tm % CONV_ROWS == 0
    nb = d_ff // tn
    hb = tm // HALO
    last_hb = m // HALO - 1
    kern = functools.partial(_ffn_up_kernel, tm=tm, tiles_per_seq=seq // tm)
    mm = lambda j: jnp.minimum(j, nb - 1)
    ep = lambda j: jnp.maximum(j - 1, 0)
    r_buf = pltpu.VMEM((tm + 2 * HALO, tn), F32)
    return pl.pallas_call(
        kern,
        grid=(m // tm, nb + 1),
        in_specs=[pl.BlockSpec((tm, d), lambda i, j: (i, 0)),
                  pl.BlockSpec((HALO, d), lambda i, j: (jnp.maximum(i * hb - 1, 0), 0)),
                  pl.BlockSpec((HALO, d), lambda i, j: (jnp.minimum((i + 1) * hb, last_hb), 0)),
                  pl.BlockSpec((d, tn), lambda i, j: (0, mm(j))),
                  pl.BlockSpec((d, tn), lambda i, j: (0, mm(j) + nb)),
                  pl.BlockSpec((3, tn), lambda i, j: (0, ep(j))),
                  pl.BlockSpec((3, tn), lambda i, j: (0, ep(j) + nb)),
                  pl.BlockSpec((1, tn), lambda i, j: (0, ep(j))),
                  pl.BlockSpec((1, tn), lambda i, j: (0, ep(j) + nb))],
        out_specs=pl.BlockSpec((tm, tn), lambda i, j: (i, ep(j))),
        out_shape=jax.ShapeDtypeStruct((m, d_ff), BF16),
        scratch_shapes=[pltpu.VMEM((tm + 2 * HALO, d), BF16), r_buf, r_buf, r_buf, r_buf],
        compiler_params=_params("parallel", "arbitrary"),
        name="ffn_up_conv_silu",
    )(h, h, h, w_up, w_up, conv_w, conv_w, conv_b, conv_b)


def _prep_w_in(w_in, d_model, heads):
    dq = heads * 2 * DH_DIFF
    dv = heads * V_DIFF
    qkv_cols = 2 * dq + dv
    lat_cols = w_in.shape[1] - qkv_cols - 2 * d_model
    lat_pad = -lat_cols % LANE
    col_scale = jnp.concatenate([jnp.full((dq,), DH_DIFF ** -0.5 * LOG2E, F32),
                                 jnp.ones((dq + dv,), F32)])
    w_qkv = (w_in[:, :qkv_cols] * col_scale).astype(BF16)
    w_lat = jnp.pad(w_in[:, qkv_cols:qkv_cols + lat_cols], ((0, 0), (0, lat_pad))).astype(BF16)
    w_gate = w_in[:, qkv_cols + lat_cols:].astype(BF16)
    return w_qkv, w_lat, w_gate


def _prep_w_q_up(w, heads):
    k = w.shape[0]
    w = w.reshape(k, heads, QK_NOPE + QK_ROPE)
    w = jnp.pad(w, ((0, 0), (0, 0), (0, MLA_QK_PAD - QK_NOPE - QK_ROPE)))
    return w.reshape(k, heads * MLA_QK_PAD).astype(BF16)


def _prep_w_kv_up(w, heads):
    kv = w.shape[0]
    w = w.reshape(kv, heads, QK_NOPE + V_MLA)
    top_k = jnp.pad(w[:, :, :QK_NOPE], ((0, 0), (0, 0), (0, MLA_QK_PAD - QK_NOPE)))
    eye = jnp.pad(jnp.eye(QK_ROPE, dtype=F32), ((0, LANE - QK_ROPE), (QK_NOPE, MLA_QK_PAD - QK_NOPE - QK_ROPE)))
    bot_k = jnp.broadcast_to(eye[:, None, :], (LANE, heads, MLA_QK_PAD))
    w_k = jnp.concatenate([top_k, bot_k], axis=0).reshape(kv + LANE, heads * MLA_QK_PAD)
    w_v = jnp.pad(w[:, :, QK_NOPE:].reshape(kv, heads * V_MLA), ((0, LANE), (0, 0)))
    return jnp.concatenate([w_k, w_v], axis=1).astype(BF16)


def _rope_tables(seq):
    pos = jnp.arange(seq, dtype=F32)
    inv_freq = ROPE_THETA ** (-jnp.arange(0, QK_ROPE, 2, dtype=F32) / QK_ROPE)
    ang = pos[:, None] * inv_freq[None, :]
    cos, sin = jnp.cos(ang), jnp.sin(ang)
    half = QK_ROPE // 2
    z = lambda n: jnp.zeros((seq, n), F32)
    c = jnp.concatenate([cos, cos, z(LANE - QK_ROPE)], axis=1)
    s1 = jnp.concatenate([-sin, z(LANE - half)], axis=1)
    s2 = jnp.concatenate([z(half), sin, z(LANE - QK_ROPE)], axis=1)
    return c, s1, s2


def _encoder_layer(x, batch, seq, layer_idx, rel_bias, wts):
    (rms_attn_g, w_qkv, w_lat, w_gate, lams, diff_subln_g, mla_q_norm_g, w_q_up, mla_kv_norm_g,
     w_kv_ext, w_branch_a, w_branch_b, w_out, rms_ffn_g, w_ffn_up, conv_w, conv_b, w_ffn_down) = wts
    m, d_model = x.shape
    heads = w_branch_a.shape[0] // V_DIFF
    q_lora = mla_q_norm_g.shape[0]
    kv_lora = mla_kv_norm_g.shape[0]
    assert q_lora % kv_lora == 0 and (q_lora + kv_lora) % LANE == 0
    lambda_init = 0.8 - 0.6 * math.exp(-0.3 * layer_idx)
    tm = _tile("mm_tm", seq)

    h = rmsnorm(x, rms_attn_g, BF16)
    qkv = matmul(h, w_qkv, BF16, tm, _tile("mm_tn", w_qkv.shape[1]), "in_proj_qkv")
    lat = matmul(h, w_lat, F32, _tile("lat_tm", seq), w_lat.shape[1], "in_proj_latent")
    gates = matmul(h, w_gate, F32, tm, _tile("mm_tn", d_model), "in_proj_gates")

    a_out = diff_attention(qkv, rel_bias, lams, diff_subln_g, lambda_init, batch, seq, heads)

    rope_tabs = _rope_tables(seq)
    q_mla = mla_q_up(lat, mla_q_norm_g, w_q_up, rope_tabs, seq, q_lora)
    kv_mla = mla_kv_up(lat, mla_kv_norm_g, w_kv_ext, rope_tabs, seq, q_lora, kv_lora)
    b_out = mla_attention(q_mla, kv_mla, batch, seq, heads)

    merged = gated_merge(a_out, b_out, w_branch_a, w_branch_b, gates, tm, _tile("mm_tn", d_model))
    x = matmul_residual(merged, w_out, x, tm, _tile("mm_tn", d_model), "out_proj")

    h2 = rmsnorm(x, rms_ffn_g, BF16)
    act = ffn_up(h2, w_ffn_up, conv_w, conv_b, seq)
    return matmul_residual(act, w_ffn_down, x, _tile("down_tm", seq), _tile("down_tn", d_model),
                           "ffn_down")


def kernel(x_prompt, x_sample, rel_bias, final_norm_g, rms_attn_g, w_in, lambda_q1, lambda_k1,
           lambda_q2, lambda_k2, diff_subln_g, mla_q_norm_g, w_mla_q_up, mla_kv_norm_g,
           w_mla_kv_up, w_branch_a, w_branch_b, w_out, rms_ffn_g, w_ffn_up, conv_w, conv_b,
           w_ffn_down):
    depth = w_in.shape[0]
    d_model = x_prompt.shape[-1]
    heads = w_branch_a.shape[1] // V_DIFF

    layers = []
    for l in range(depth):
        w_qkv, w_lat, w_gate = _prep_w_in(w_in[l], d_model, heads)
        layers.append((
            rms_attn_g[l], w_qkv, w_lat, w_gate,
            (lambda_q1[l], lambda_k1[l], lambda_q2[l], lambda_k2[l]), diff_subln_g[l],
            mla_q_norm_g[l], _prep_w_q_up(w_mla_q_up[l], heads),
            mla_kv_norm_g[l], _prep_w_kv_up(w_mla_kv_up[l], heads),
            w_branch_a[l].astype(BF16), w_branch_b[l].astype(BF16), w_out[l].astype(BF16),
            rms_ffn_g[l], w_ffn_up[l].astype(BF16), conv_w[l].astype(F32),
            conv_b[l].reshape(1, -1).astype(F32), w_ffn_down[l].astype(BF16)))

    def trunk(x):
        batch, seq, _ = x.shape
        y = x.reshape(batch * seq, d_model)
        for l in range(depth):
            y = _encoder_layer(y, batch, seq, l, rel_bias, layers[l])
        return rmsnorm(y, final_norm_g, x.dtype).reshape(x.shape)

    return (trunk(x_prompt), trunk(x_sample))
```

```python
import functools
import math

import jax
import jax.numpy as jnp
from jax import lax
from jax.experimental import pallas as pl
from jax.experimental.pallas import tpu as pltpu

DH_DIFF = 64
V_DIFF = 2 * DH_DIFF
QK_NOPE = 128
QK_ROPE = 64
V_MLA = 128
ROPE_THETA = 10000.0
N_BUCKETS = 32
MAX_DISTANCE = 128
EPS = 1e-6
LOG2E = math.log2(math.e)

LANE = 128
BF16_SUBLANE = 16
VMEM_LIMIT_BYTES = 56 * 1024 * 1024

MLA_QK_PAD = 2 * LANE
HALO = BF16_SUBLANE
CONV_ROWS = 64

TILES = dict(
    norm_tm=512,
    mm_tm=1024, mm_tn=512,
    lat_tm=512,
    qup_tm=1024, qup_tn=1024,
    kvup_tm=1024, kvup_tn=1024,
    ffn_tm=1024, ffn_tn=256,
    down_tm=512, down_tn=256,
    diff_tq=512, diff_tk=1024,
    mla_tq=2048, mla_tk=1024,
)

F32 = jnp.float32
BF16 = jnp.bfloat16


def _tile(name, dim):
    t = min(TILES[name], dim)
    assert dim % t == 0, (name, dim, t)
    return t


def _params(*sem):
    return pltpu.CompilerParams(dimension_semantics=sem, vmem_limit_bytes=VMEM_LIMIT_BYTES)


def _rms(x, g):
    return x * lax.rsqrt(jnp.mean(x * x, axis=-1, keepdims=True) + EPS) * g


def _rope128(p, c, s1, s2):
    return (p * c + pltpu.roll(p, LANE - QK_ROPE // 2, 1) * s1
            + pltpu.roll(p, QK_ROPE // 2, 1) * s2)


def _rmsnorm_kernel(x_ref, g_ref, o_ref):
    o_ref[...] = _rms(x_ref[...].astype(F32), g_ref[...]).astype(o_ref.dtype)


def rmsnorm(x, g, out_dtype):
    m, d = x.shape
    tm = _tile("norm_tm", m)
    return pl.pallas_call(
        _rmsnorm_kernel,
        grid=(m // tm,),
        in_specs=[pl.BlockSpec((tm, d), lambda i: (i, 0)),
                  pl.BlockSpec((1, d), lambda i: (0, 0))],
        out_specs=pl.BlockSpec((tm, d), lambda i: (i, 0)),
        out_shape=jax.ShapeDtypeStruct((m, d), out_dtype),
        compiler_params=_params("parallel"),
        name="rmsnorm",
    )(x, g.reshape(1, d).astype(F32))


def _mm_kernel(a_ref, b_ref, o_ref):
    o_ref[...] = jnp.dot(a_ref[...], b_ref[...], preferred_element_type=F32).astype(o_ref.dtype)


def matmul(a, b, out_dtype, tm, tn, name):
    m, k = a.shape
    n = b.shape[1]
    return pl.pallas_call(
        _mm_kernel,
        grid=(m // tm, n // tn),
        in_specs=[pl.BlockSpec((tm, k), lambda i, j: (i, 0)),
                  pl.BlockSpec((k, tn), lambda i, j: (0, j))],
        out_specs=pl.BlockSpec((tm, tn), lambda i, j: (i, j)),
        out_shape=jax.ShapeDtypeStruct((m, n), out_dtype),
        compiler_params=_params("parallel", "arbitrary"),
        name=name,
    )(a, b)


def _mm_res_kernel(a_ref, b_ref, r_ref, o_ref):
    o_ref[...] = r_ref[...] + jnp.dot(a_ref[...], b_ref[...], preferred_element_type=F32)


def matmul_residual(a, b, res, tm, tn, name):
    m, k = a.shape
    n = b.shape[1]
    return pl.pallas_call(
        _mm_res_kernel,
        grid=(m // tm, n // tn),
        in_specs=[pl.BlockSpec((tm, k), lambda i, j: (i, 0)),
                  pl.BlockSpec((k, tn), lambda i, j: (0, j)),
                  pl.BlockSpec((tm, tn), lambda i, j: (i, j))],
        out_specs=pl.BlockSpec((tm, tn), lambda i, j: (i, j)),
        out_shape=jax.ShapeDtypeStruct((m, n), F32),
        compiler_params=_params("parallel", "arbitrary"),
        name=name,
    )(a, b, res)


def _merge_kernel(a_ref, b_ref, wa_ref, wb_ref, ga_ref, gb_ref, o_ref):
    ya = jnp.dot(a_ref[...], wa_ref[...], preferred_element_type=F32)
    yb = jnp.dot(b_ref[...], wb_ref[...], preferred_element_type=F32)
    sa = 1.0 / (1.0 + jnp.exp(-ga_ref[...]))
    sb = 1.0 / (1.0 + jnp.exp(-gb_ref[...]))
    o_ref[...] = (sa * ya + sb * yb).astype(o_ref.dtype)


def gated_merge(a, b, wa, wb, gates, tm, tn):
    m, ka = a.shape
    kb = b.shape[1]
    n = wa.shape[1]
    nb = n // tn
    return pl.pallas_call(
        _merge_kernel,
        grid=(m // tm, nb),
        in_specs=[pl.BlockSpec((tm, ka), lambda i, j: (i, 0)),
                  pl.BlockSpec((tm, kb), lambda i, j: (i, 0)),
                  pl.BlockSpec((ka, tn), lambda i, j: (0, j)),
                  pl.BlockSpec((kb, tn), lambda i, j: (0, j)),
                  pl.BlockSpec((tm, tn), lambda i, j: (i, j)),
                  pl.BlockSpec((tm, tn), lambda i, j: (i, j + nb))],
        out_specs=pl.BlockSpec((tm, tn), lambda i, j: (i, j)),
        out_shape=jax.ShapeDtypeStruct((m, n), BF16),
        compiler_params=_params("parallel", "arbitrary"),
        name="gated_merge",
    )(a, b, wa, wb, gates, gates)


def _mla_q_kernel(lat_ref, g_ref, w_ref, c_ref, s1_ref, s2_ref, o_ref, a_sc, *, heads, scale):
    @pl.when(pl.program_id(1) == 0)
    def _():
        a_sc[...] = _rms(lat_ref[...], g_ref[...]).astype(BF16)

    acc = jnp.dot(a_sc[...], w_ref[...], preferred_element_type=F32)
    c, s1, s2 = c_ref[...], s1_ref[...], s2_ref[...]
    for h in range(heads):
        lo = h * MLA_QK_PAD
        o_ref[:, lo:lo + QK_NOPE] = (acc[:, lo:lo + QK_NOPE] * scale).astype(o_ref.dtype)
        rot = _rope128(acc[:, lo + QK_NOPE:lo + MLA_QK_PAD], c, s1, s2)
        o_ref[:, lo + QK_NOPE:lo + MLA_QK_PAD] = (rot * scale).astype(o_ref.dtype)


def mla_q_up(lat, g, w, rope_tabs, seq, q_lora):
    m = lat.shape[0]
    n = w.shape[1]
    tm = _tile("qup_tm", seq)
    tn = _tile("qup_tn", n)
    pos_blocks = seq // tm
    tab_spec = pl.BlockSpec((tm, LANE), lambda i, j: (i % pos_blocks, 0))
    kern = functools.partial(_mla_q_kernel, heads=tn // MLA_QK_PAD,
                             scale=(QK_NOPE + QK_ROPE) ** -0.5 * LOG2E)
    return pl.pallas_call(
        kern,
        grid=(m // tm, n // tn),
        in_specs=[pl.BlockSpec((tm, q_lora), lambda i, j: (i, 0)),
                  pl.BlockSpec((1, q_lora), lambda i, j: (0, 0)),
                  pl.BlockSpec((q_lora, tn), lambda i, j: (0, j)),
                  tab_spec, tab_spec, tab_spec],
        out_specs=pl.BlockSpec((tm, tn), lambda i, j: (i, j)),
        out_shape=jax.ShapeDtypeStruct((m, n), BF16),
        scratch_shapes=[pltpu.VMEM((tm, q_lora), BF16)],
        compiler_params=_params("parallel", "arbitrary"),
        name="mla_q_up",
    )(lat, g.reshape(1, q_lora).astype(F32), w, *rope_tabs)


def _mla_kv_kernel(kv_ref, pe_ref, g_ref, w_ref, c_ref, s1_ref, s2_ref, o_ref, a_sc, *, kv_lora):
    @pl.when(pl.program_id(1) == 0)
    def _():
        a_sc[:, :kv_lora] = _rms(kv_ref[...], g_ref[...]).astype(BF16)
        rot = _rope128(pe_ref[...], c_ref[...], s1_ref[...], s2_ref[...])
        a_sc[:, kv_lora:] = rot.astype(BF16)

    o_ref[...] = jnp.dot(a_sc[...], w_ref[...], preferred_element_type=F32).astype(o_ref.dtype)


def mla_kv_up(lat, g, w_ext, rope_tabs, seq, q_lora, kv_lora):
    m = lat.shape[0]
    n = w_ext.shape[1]
    tm = _tile("kvup_tm", seq)
    tn = _tile("kvup_tn", n)
    pos_blocks = seq // tm
    tab_spec = pl.BlockSpec((tm, LANE), lambda i, j: (i % pos_blocks, 0))
    kv_blk = q_lora // kv_lora
    pe_blk = (q_lora + kv_lora) // LANE
    return pl.pallas_call(
        functools.partial(_mla_kv_kernel, kv_lora=kv_lora),
        grid=(m // tm, n // tn),
        in_specs=[pl.BlockSpec((tm, kv_lora), lambda i, j: (i, kv_blk)),
                  pl.BlockSpec((tm, LANE), lambda i, j: (i, pe_blk)),
                  pl.BlockSpec((1, kv_lora), lambda i, j: (0, 0)),
                  pl.BlockSpec((kv_lora + LANE, tn), lambda i, j: (0, j)),
                  tab_spec, tab_spec, tab_spec],
        out_specs=pl.BlockSpec((tm, tn), lambda i, j: (i, j)),
        out_shape=jax.ShapeDtypeStruct((m, n), BF16),
        scratch_shapes=[pltpu.VMEM((tm, kv_lora + LANE), BF16)],
        compiler_params=_params("parallel", "arbitrary"),
        name="mla_kv_up",
    )(lat, lat, g.reshape(1, kv_lora).astype(F32), w_ext, *rope_tabs)


def _lane_tile(x, n):
    return jnp.concatenate([x] * n, axis=1)


def _softmax_init(m_sc, acc_sc):
    m_sc[...] = jnp.full(m_sc.shape, -jnp.inf, F32)
    acc_sc[...] = jnp.zeros(acc_sc.shape, F32)


def _softmax_update(s, vx, m_sc, acc_sc, c=None):
    tk = s.shape[1]
    m_prev = m_sc[...]
    m_cur = jnp.max(s, axis=1, keepdims=True)
    if c is not None:
        m_cur = m_cur + c
    m_new = jnp.maximum(m_prev, m_cur)
    alpha = jnp.exp2(m_prev - m_new)
    shift = m_new if c is None else m_new - c
    p = jnp.exp2(s - _lane_tile(shift, tk // LANE))
    acc_sc[...] = (_lane_tile(alpha, 2) * acc_sc[...]
                   + jnp.dot(p.astype(BF16), vx, preferred_element_type=F32))
    m_sc[...] = m_new


def _extend_v(v_ref, vx_sc, tk):
    width = v_ref.shape[1]

    def body(j, carry):
        r = pl.ds(pl.multiple_of(j * tk, tk), tk)
        vx_sc[r, :width] = v_ref[r, :]
        vx_sc[r, width:] = jnp.ones((tk, vx_sc.shape[1] - width), vx_sc.dtype)
        return carry

    lax.fori_loop(0, v_ref.shape[0] // tk, body, 0)


def _chunk_rows(j, tk):
    start = j * tk
    return pl.ds(start if isinstance(j, int) else pl.multiple_of(start, tk), tk)


def _qk(q, k):
    return lax.dot_general(q, k, (((1,), (1,)), ((), ())), preferred_element_type=F32)


def _chunk_loop(lo, hi, scores, s_bufs, consume):
    def by_parity(j, fn):
        for parity in (0, 1):
            @pl.when((j & 1) == parity)
            def _():
                fn(j, s_bufs[parity], s_bufs[1 - parity])

    def one(j, cur, nxt):
        nxt[...] = scores(j + 1)
        consume(j, cur)

    def two(j, cur, nxt):
        nxt[...] = scores(j + 1)
        consume(j, cur)
        cur[...] = scores(j + 2)
        consume(j + 1, nxt)

    pairs = lax.shift_right_logical(jnp.maximum(hi - lo, 0), 1)

    def body(t, carry):
        by_parity(lo + 2 * t, two)
        return carry

    lax.fori_loop(0, pairs, body, 0)
    rest = lo + 2 * pairs

    @pl.when(rest < hi)
    def _():
        by_parity(rest, one)


def _first_chunk(s_bufs, scores):
    first = functools.reduce(jnp.logical_and, [pl.program_id(a) == 0 for a in range(3)])

    @pl.when(first)
    def _():
        s_bufs[0][...] = scores(0)


def _last_chunk(n_chunks, next_scores, s_bufs, consume):
    s_bufs[0][...] = next_scores()
    consume(n_chunks - 1, s_bufs[1])


def _next_step(b, h, i, batch, heads, nq):
    n = jnp.minimum((b * heads + h) * nq + i + 1, batch * heads * nq - 1)
    bh = n // nq
    return bh // heads, bh % heads, n % nq


def _mla_attn_kernel(q_ref, k_ref, v_ref, qn_ref, kn_ref, o_ref, vx_sc, sa_sc, sb_sc, m_sc, acc_sc,
                     *, tk):
    @pl.when(pl.program_id(2) == 0)
    def _():
        _extend_v(v_ref, vx_sc, tk)

    _softmax_init(m_sc, acc_sc)
    rows = lambda j: _chunk_rows(j, tk)
    scores = lambda j: _qk(q_ref[...], k_ref[rows(j), :])

    def consume(j, s_ref):
        _softmax_update(s_ref[...], vx_sc[rows(j), :], m_sc, acc_sc)

    nk = k_ref.shape[0] // tk
    s_bufs = (sa_sc, sb_sc)
    _first_chunk(s_bufs, scores)
    _chunk_loop(0, nk - 1, scores, s_bufs, consume)
    _last_chunk(nk, lambda: _qk(qn_ref[...], kn_ref[...]), s_bufs, consume)
    acc = acc_sc[...]
    o_ref[...] = (acc[:, :V_MLA] / acc[:, V_MLA:]).astype(o_ref.dtype)


def mla_attention(q, kv, batch, seq, heads):
    tq = _tile("mla_tq", seq)
    tk = _tile("mla_tk", seq)
    nq, nk = seq // tq, seq // tk
    assert nk % 2 == 0
    v_blk0 = heads * MLA_QK_PAD // V_MLA
    nxt = lambda b, h, i: _next_step(b, h, i, batch, heads, nq)

    def q_next(b, h, i):
        b2, h2, i2 = nxt(b, h, i)
        return b2 * nq + i2, h2

    def k_next(b, h, i):
        b2, h2, _ = nxt(b, h, i)
        return b2 * nk, h2

    return pl.pallas_call(
        functools.partial(_mla_attn_kernel, tk=tk),
        grid=(batch, heads, nq),
        in_specs=[pl.BlockSpec((tq, MLA_QK_PAD), lambda b, h, i: (b * nq + i, h)),
                  pl.BlockSpec((seq, MLA_QK_PAD), lambda b, h, i: (b, h)),
                  pl.BlockSpec((seq, V_MLA), lambda b, h, i: (b, v_blk0 + h)),
                  pl.BlockSpec((tq, MLA_QK_PAD), q_next),
                  pl.BlockSpec((tk, MLA_QK_PAD), k_next)],
        out_specs=pl.BlockSpec((tq, V_MLA), lambda b, h, i: (b * nq + i, h)),
        out_shape=jax.ShapeDtypeStruct((batch * seq, heads * V_MLA), BF16),
        scratch_shapes=[pltpu.VMEM((seq, 2 * V_MLA), BF16),
                        pltpu.VMEM((tq, tk), F32), pltpu.VMEM((tq, tk), F32),
                        pltpu.VMEM((tq, LANE), F32), pltpu.VMEM((tq, 2 * V_MLA), F32)],
        compiler_params=_params("arbitrary", "arbitrary", "arbitrary"),
        name="mla_attention",
    )(q, kv, kv, q, kv)


def _diff_attn_kernel(tab_ref, far_ref, ids_ref, q_ref, k_ref, v_ref, qn_ref, kn_ref, lq1_ref, lk1_ref, lq2_ref,
                      lk2_ref, g_ref, o_ref, q2_sc, vx_sc, bias_sc, sa_sc, sb_sc, m_sc, acc_sc, *, tq, tk,
                      step, near_off0, lambda_init):
    h, i = pl.program_id(1), pl.program_id(2)
    seq = k_ref.shape[0]
    nk = seq // tk

    @pl.when(i == 0)
    def _():
        _extend_v(v_ref, vx_sc, tk)
        for t in range(bias_sc.shape[0]):
            ids = ids_ref[t]
            f = jnp.zeros(ids.shape, F32)
            for b in range(N_BUCKETS):
                f = jnp.where(ids == b, tab_ref[h, b], f)
            f = f * LOG2E
            bias_sc[t] = pltpu.roll(jnp.broadcast_to(f, (tq, tq + tk)), tk + 1, 1,
                                    stride=1, stride_axis=0)[:, :tk]

    def stacked(q):
        lane = lax.broadcasted_iota(jnp.int32, q.shape, 1)
        zero = jnp.zeros_like(q)
        return jnp.concatenate([jnp.where(lane < DH_DIFF, q, zero),
                                jnp.where(lane >= DH_DIFF, q, zero)], axis=0)

    q2_sc[...] = stacked(q_ref[...])
    _softmax_init(m_sc, acc_sc)

    rows = lambda j: _chunk_rows(j, tk)
    scores = lambda j: _qk(q2_sc[...], k_ref[rows(j), :])

    def far(c):
        def consume(j, s_ref):
            _softmax_update(s_ref[...], vx_sc[rows(j), :], m_sc, acc_sc, c)
        return consume

    def near(j, s_ref):
        bias = bias_sc[(j * tk - i * tq - near_off0) // step]
        s = s_ref[...] + jnp.concatenate([bias, bias], axis=0)
        _softmax_update(s, vx_sc[rows(j), :], m_sc, acc_sc)

    j_near = jnp.maximum(i * tq - (MAX_DISTANCE - 1), 0) // tk
    j_right = jnp.minimum((i * tq + tq + MAX_DISTANCE - 1 + tk - 1) // tk, nk)
    far_left = far(tab_ref[h, far_ref[0]] * LOG2E)
    far_right = far(tab_ref[h, far_ref[1]] * LOG2E)
    s_bufs = (sa_sc, sb_sc)
    last = nk - 1
    _first_chunk(s_bufs, scores)
    _chunk_loop(0, j_near, scores, s_bufs, far_left)
    _chunk_loop(j_near, jnp.minimum(j_right, last), scores, s_bufs, near)
    _chunk_loop(j_right, last, scores, s_bufs, far_right)
    next_scores = lambda: _qk(stacked(qn_ref[...]), kn_ref[...])

    @pl.when(j_right > last)
    def _():
        _last_chunk(nk, next_scores, s_bufs, near)

    @pl.when(j_right <= last)
    def _():
        _last_chunk(nk, next_scores, s_bufs, far_right)

    acc = acc_sc[...]
    o = acc[:, :V_DIFF] / acc[:, V_DIFF:]
    lam = (jnp.exp(jnp.sum(lq1_ref[...] * lk1_ref[...], axis=1, keepdims=True))
           - jnp.exp(jnp.sum(lq2_ref[...] * lk2_ref[...], axis=1, keepdims=True)) + lambda_init)
    a = o[:tq] - lam * o[tq:]
    o_ref[...] = (_rms(a, g_ref[...]) * (1.0 - lambda_init)).astype(o_ref.dtype)


def _t5_bucket(rel):
    nb = N_BUCKETS // 2
    max_exact = nb // 2
    ret = (rel > 0).astype(jnp.int32) * nb
    n = jnp.abs(rel)
    nf = jnp.maximum(n, max_exact).astype(F32)
    large = max_exact + (jnp.log(nf / max_exact) / math.log(MAX_DISTANCE / max_exact)
                         * (nb - max_exact)).astype(jnp.int32)
    large = jnp.minimum(large, nb - 1)
    return ret + jnp.where(n < max_exact, n, large)


def diff_attention(qkv, rel_bias, lams, subln_g, lambda_init, batch, seq, heads):
    tq = _tile("diff_tq", seq)
    tk = _tile("diff_tk", seq)
    nq = seq // tq
    step = math.gcd(tq, tk)
    lo = -(tk - 1 + MAX_DISTANCE)
    near_off0 = (lo // step + 1) * step
    near_offs = list(range(near_off0, tq - 1 + MAX_DISTANCE, step))
    win = (jnp.array(near_offs, jnp.int32)[:, None] - (tq - 1)
           + jnp.arange(tq + tk, dtype=jnp.int32)[None, :])
    ids = _t5_bucket(win).reshape(len(near_offs), 1, tq + tk)
    far_ids = _t5_bucket(jnp.array([-MAX_DISTANCE, MAX_DISTANCE], jnp.int32))
    vec = lambda: pl.BlockSpec((1, DH_DIFF), lambda b, h, i: (0, 0))
    kern = functools.partial(_diff_attn_kernel, tq=tq, tk=tk, step=step, near_off0=near_off0,
                             lambda_init=lambda_init)
    nk = seq // tk
    assert nk % 2 == 0
    nxt = lambda b, h, i: _next_step(b, h, i, batch, heads, nq)

    def q_next(b, h, i):
        b2, h2, i2 = nxt(b, h, i)
        return b2 * nq + i2, h2

    def k_next(b, h, i):
        b2, h2, _ = nxt(b, h, i)
        return b2 * nk, heads + h2

    return pl.pallas_call(
        kern,
        grid=(batch, heads, nq),
        in_specs=[pl.BlockSpec(memory_space=pltpu.SMEM),
                  pl.BlockSpec(memory_space=pltpu.SMEM),
                  pl.BlockSpec(ids.shape, lambda b, h, i: (0, 0, 0)),
                  pl.BlockSpec((tq, V_DIFF), lambda b, h, i: (b * nq + i, h)),
                  pl.BlockSpec((seq, V_DIFF), lambda b, h, i: (b, heads + h)),
                  pl.BlockSpec((seq, V_DIFF), lambda b, h, i: (b, 2 * heads + h)),
                  pl.BlockSpec((tq, V_DIFF), q_next),
                  pl.BlockSpec((tk, V_DIFF), k_next),
                  vec(), vec(), vec(), vec(),
                  pl.BlockSpec((1, V_DIFF), lambda b, h, i: (0, 0))],
        out_specs=pl.BlockSpec((tq, V_DIFF), lambda b, h, i: (b * nq + i, h)),
        out_shape=jax.ShapeDtypeStruct((batch * seq, heads * V_DIFF), BF16),
        scratch_shapes=[pltpu.VMEM((2 * tq, V_DIFF), BF16), pltpu.VMEM((seq, 2 * V_DIFF), BF16),
                        pltpu.VMEM((len(near_offs), tq, tk), F32),
                        pltpu.VMEM((2 * tq, tk), F32), pltpu.VMEM((2 * tq, tk), F32),
                        pltpu.VMEM((2 * tq, LANE), F32), pltpu.VMEM((2 * tq, 2 * V_DIFF), F32)],
        compiler_params=_params("arbitrary", "arbitrary", "arbitrary"),
        name="diff_attention",
    )(rel_bias.T.astype(F32), far_ids, ids, qkv, qkv, qkv, qkv, qkv,
      *[v.reshape(1, DH_DIFF).astype(F32) for v in lams], subln_g.reshape(1, V_DIFF).astype(F32))


def _ffn_up_kernel(a_ref, prev_ref, next_ref, wg_ref, wu_ref, cwg_ref, cwu_ref, cbg_ref, cbu_ref,
                   o_ref, a_sc, rg0_sc, ru0_sc, rg1_sc, ru1_sc, *, tm, tiles_per_seq):
    i, j = pl.program_id(0), pl.program_id(1)

    @pl.when(j == 0)
    def _():
        pos = i % tiles_per_seq
        prev = prev_ref[...]
        nxt = next_ref[...]
        a_sc[:HALO] = jnp.where(pos == 0, jnp.zeros_like(prev), prev)
        a_sc[HALO:HALO + tm] = a_ref[...]
        a_sc[HALO + tm:] = jnp.where(pos == tiles_per_seq - 1, jnp.zeros_like(nxt), nxt)
        rg1_sc[...] = jnp.zeros(rg1_sc.shape, F32)
        ru1_sc[...] = jnp.zeros(ru1_sc.shape, F32)

    def conv(r_ref, cw, cb, lo):
        taps = [r_ref[pl.ds(HALO + lo + d, CONV_ROWS), :] for d in (-1, 0, 1)]
        return (taps[0] * cw[0:1] + taps[1] * cw[1:2] + taps[2] * cw[2:3]) + cb

    def step(rg_new, ru_new, rg_old, ru_old):
        cwg, cwu, cbg, cbu = cwg_ref[...], cwu_ref[...], cbg_ref[...], cbu_ref[...]
        for lo in range(0, tm, CONV_ROWS):
            gate = conv(rg_old, cwg, cbg, lo)
            up = conv(ru_old, cwu, cbu, lo)
            o_ref[lo:lo + CONV_ROWS, :] = (gate * (1.0 / (1.0 + jnp.exp(-gate))) * up).astype(o_ref.dtype)
        rg_new[...] = jnp.dot(a_sc[...], wg_ref[...], preferred_element_type=F32)
        ru_new[...] = jnp.dot(a_sc[...], wu_ref[...], preferred_element_type=F32)

    @pl.when((j & 1) == 0)
    def _():
        step(rg0_sc, ru0_sc, rg1_sc, ru1_sc)

    @pl.when((j & 1) == 1)
    def _():
        step(rg1_sc, ru1_sc, rg0_sc, ru0_sc)


def ffn_up(h, w_up, conv_w, conv_b, seq):
    m, d = h.shape
    d_ff = w_up.shape[1] // 2
    tm = _tile("ffn_tm", seq)
    tn = _tile("ffn_tn", d_ff)
    assert tm % CONV_ROWS == 0
    nb = d_ff // tn
    w_blocks = w_up.reshape(d, 2 * nb, tn).transpose(1, 0, 2)
    hb = tm // HALO
    last_hb = m // HALO - 1
    kern = functools.partial(_ffn_up_kernel, tm=tm, tiles_per_seq=seq // tm)
    mm = lambda j: jnp.minimum(j, nb - 1)
    ep = lambda j: jnp.maximum(j - 1, 0)
    r_buf = pltpu.VMEM((tm + 2 * HALO, tn), F32)
    return pl.pallas_call(
        kern,
        grid=(m // tm, nb + 1),
        in_specs=[pl.BlockSpec((tm, d), lambda i, j: (i, 0)),
                  pl.BlockSpec((HALO, d), lambda i, j: (jnp.maximum(i * hb - 1, 0), 0)),
                  pl.BlockSpec((HALO, d), lambda i, j: (jnp.minimum((i + 1) * hb, last_hb), 0)),
                  pl.BlockSpec((None, d, tn), lambda i, j: (mm(j), 0, 0)),
                  pl.BlockSpec((None, d, tn), lambda i, j: (mm(j) + nb, 0, 0)),
                  pl.BlockSpec((3, tn), lambda i, j: (0, ep(j))),
                  pl.BlockSpec((3, tn), lambda i, j: (0, ep(j) + nb)),
                  pl.BlockSpec((1, tn), lambda i, j: (0, ep(j))),
                  pl.BlockSpec((1, tn), lambda i, j: (0, ep(j) + nb))],
        out_specs=pl.BlockSpec((tm, tn), lambda i, j: (i, ep(j))),
        out_shape=jax.ShapeDtypeStruct((m, d_ff), BF16),
        scratch_shapes=[pltpu.VMEM((tm + 2 * HALO, d), BF16), r_buf, r_buf, r_buf, r_buf],
        compiler_params=_params("parallel", "arbitrary"),
        name="ffn_up_conv_silu",
    )(h, h, h, w_blocks, w_blocks, conv_w, conv_w, conv_b, conv_b)


def _prep_w_in(w_in, d_model, heads):
    dq = heads * 2 * DH_DIFF
    dv = heads * V_DIFF
    qkv_cols = 2 * dq + dv
    lat_cols = w_in.shape[1] - qkv_cols - 2 * d_model
    lat_pad = -lat_cols % LANE
    col_scale = jnp.concatenate([jnp.full((dq,), DH_DIFF ** -0.5 * LOG2E, F32),
                                 jnp.ones((dq + dv,), F32)])
    w_qkv = (w_in[:, :qkv_cols] * col_scale).astype(BF16)
    w_lat = jnp.pad(w_in[:, qkv_cols:qkv_cols + lat_cols], ((0, 0), (0, lat_pad))).astype(BF16)
    w_gate = w_in[:, qkv_cols + lat_cols:].astype(BF16)
    return w_qkv, w_lat, w_gate


def _prep_w_q_up(w, heads):
    k = w.shape[0]
    w = w.reshape(k, heads, QK_NOPE + QK_ROPE)
    w = jnp.pad(w, ((0, 0), (0, 0), (0, MLA_QK_PAD - QK_NOPE - QK_ROPE)))
    return w.reshape(k, heads * MLA_QK_PAD).astype(BF16)


def _prep_w_kv_up(w, heads):
    kv = w.shape[0]
    w = w.reshape(kv, heads, QK_NOPE + V_MLA)
    top_k = jnp.pad(w[:, :, :QK_NOPE], ((0, 0), (0, 0), (0, MLA_QK_PAD - QK_NOPE)))
    eye = jnp.pad(jnp.eye(QK_ROPE, dtype=F32), ((0, LANE - QK_ROPE), (QK_NOPE, MLA_QK_PAD - QK_NOPE - QK_ROPE)))
    bot_k = jnp.broadcast_to(eye[:, None, :], (LANE, heads, MLA_QK_PAD))
    w_k = jnp.concatenate([top_k, bot_k], axis=0).reshape(kv + LANE, heads * MLA_QK_PAD)
    w_v = jnp.pad(w[:, :, QK_NOPE:].reshape(kv, heads * V_MLA), ((0, LANE), (0, 0)))
    return jnp.concatenate([w_k, w_v], axis=1).astype(BF16)


def _rope_tables(seq):
    pos = jnp.arange(seq, dtype=F32)
    inv_freq = ROPE_THETA ** (-jnp.arange(0, QK_ROPE, 2, dtype=F32) / QK_ROPE)
    ang = pos[:, None] * inv_freq[None, :]
    cos, sin = jnp.cos(ang), jnp.sin(ang)
    half = QK_ROPE // 2
    z = lambda n: jnp.zeros((seq, n), F32)
    c = jnp.concatenate([cos, cos, z(LANE - QK_ROPE)], axis=1)
    s1 = jnp.concatenate([-sin, z(LANE - half)], axis=1)
    s2 = jnp.concatenate([z(half), sin, z(LANE - QK_ROPE)], axis=1)
    return c, s1, s2


def _encoder_layer(x, batch, seq, layer_idx, rel_bias, wts):
    (rms_attn_g, w_qkv, w_lat, w_gate, lams, diff_subln_g, mla_q_norm_g, w_q_up, mla_kv_norm_g,
     w_kv_ext, w_branch_a, w_branch_b, w_out, rms_ffn_g, w_ffn_up, conv_w, conv_b, w_ffn_down) = wts
    m, d_model = x.shape
    heads = w_branch_a.shape[0] // V_DIFF
    q_lora = mla_q_norm_g.shape[0]
    kv_lora = mla_kv_norm_g.shape[0]
    assert q_lora % kv_lora == 0 and (q_lora + kv_lora) % LANE == 0
    lambda_init = 0.8 - 0.6 * math.exp(-0.3 * layer_idx)
    tm = _tile("mm_tm", seq)

    h = rmsnorm(x, rms_attn_g, BF16)
    qkv = matmul(h, w_qkv, BF16, tm, _tile("mm_tn", w_qkv.shape[1]), "in_proj_qkv")
    lat = matmul(h, w_lat, F32, _tile("lat_tm", seq), w_lat.shape[1], "in_proj_latent")
    gates = matmul(h, w_gate, F32, tm, _tile("mm_tn", d_model), "in_proj_gates")

    a_out = diff_attention(qkv, rel_bias, lams, diff_subln_g, lambda_init, batch, seq, heads)

    rope_tabs = _rope_tables(seq)
    q_mla = mla_q_up(lat, mla_q_norm_g, w_q_up, rope_tabs, seq, q_lora)
    kv_mla = mla_kv_up(lat, mla_kv_norm_g, w_kv_ext, rope_tabs, seq, q_lora, kv_lora)
    b_out = mla_attention(q_mla, kv_mla, batch, seq, heads)

    merged = gated_merge(a_out, b_out, w_branch_a, w_branch_b, gates, tm, _tile("mm_tn", d_model))
    x = matmul_residual(merged, w_out, x, tm, _tile("mm_tn", d_model), "out_proj")

    h2 = rmsnorm(x, rms_ffn_g, BF16)
    act = ffn_up(h2, w_ffn_up, conv_w, conv_b, seq)
    return matmul_residual(act, w_ffn_down, x, _tile("down_tm", seq), _tile("down_tn", d_model),
                           "ffn_down")


def kernel(x_prompt, x_sample, rel_bias, final_norm_g, rms_attn_g, w_in, lambda_q1, lambda_k1,
           lambda_q2, lambda_k2, diff_subln_g, mla_q_norm_g, w_mla_q_up, mla_kv_norm_g,
           w_mla_kv_up, w_branch_a, w_branch_b, w_out, rms_ffn_g, w_ffn_up, conv_w, conv_b,
           w_ffn_down):
    depth = w_in.shape[0]
    d_model = x_prompt.shape[-1]
    heads = w_branch_a.shape[1] // V_DIFF

    layers = []
    for l in range(depth):
        w_qkv, w_lat, w_gate = _prep_w_in(w_in[l], d_model, heads)
        layers.append((
            rms_attn_g[l], w_qkv, w_lat, w_gate,
            (lambda_q1[l], lambda_k1[l], lambda_q2[l], lambda_k2[l]), diff_subln_g[l],
            mla_q_norm_g[l], _prep_w_q_up(w_mla_q_up[l], heads),
            mla_kv_norm_g[l], _prep_w_kv_up(w_mla_kv_up[l], heads),
            w_branch_a[l].astype(BF16), w_branch_b[l].astype(BF16), w_out[l].astype(BF16),
            rms_ffn_g[l], w_ffn_up[l].astype(BF16), conv_w[l].astype(F32),
            conv_b[l].reshape(1, -1).astype(F32), w_ffn_down[l].astype(BF16)))

    def trunk(x):
        batch, seq, _ = x.shape
        y = x.reshape(batch * seq, d_model)
        for l in range(depth):
            y = _encoder_layer(y, batch, seq, l, rel_bias, layers[l])
        return rmsnorm(y, final_norm_g, x.dtype).reshape(x.shape)

    return (trunk(x_prompt), trunk(x_sample))
```

```python
import functools
import math

import jax
import jax.numpy as jnp
from jax import lax
from jax.experimental import pallas as pl
from jax.experimental.pallas import tpu as pltpu

DH_DIFF = 64
V_DIFF = 2 * DH_DIFF
QK_NOPE = 128
QK_ROPE = 64
V_MLA = 128
ROPE_THETA = 10000.0
N_BUCKETS = 32
MAX_DISTANCE = 128
EPS = 1e-6
LOG2E = math.log2(math.e)

LANE = 128
BF16_SUBLANE = 16
VMEM_LIMIT_BYTES = 56 * 1024 * 1024

MLA_QK_PAD = 2 * LANE
HALO = BF16_SUBLANE

TILES = dict(
    norm_tm=512,
    mm_tm=1024, mm_tn=512,
    lat_tm=512,
    qup_tm=1024, qup_tn=1024,
    kvup_tm=1024, kvup_tn=1024,
    ffn_tm=1024, ffn_tn=256,
    down_tm=512, down_tn=256,
    diff_tq=512, diff_tk=1024,
    mla_tq=1024, mla_tk=1024,
)

F32 = jnp.float32
BF16 = jnp.bfloat16


def _tile(name, dim):
    t = min(TILES[name], dim)
    assert dim % t == 0, (name, dim, t)
    return t


def _params(*sem):
    return pltpu.CompilerParams(dimension_semantics=sem, vmem_limit_bytes=VMEM_LIMIT_BYTES)


def _rms(x, g):
    return x * lax.rsqrt(jnp.mean(x * x, axis=-1, keepdims=True) + EPS) * g


def _rope128(p, c, s1, s2):
    return (p * c + pltpu.roll(p, LANE - QK_ROPE // 2, 1) * s1
            + pltpu.roll(p, QK_ROPE // 2, 1) * s2)


def _rmsnorm_kernel(x_ref, g_ref, o_ref):
    o_ref[...] = _rms(x_ref[...].astype(F32), g_ref[...]).astype(o_ref.dtype)


def rmsnorm(x, g, out_dtype):
    m, d = x.shape
    tm = _tile("norm_tm", m)
    return pl.pallas_call(
        _rmsnorm_kernel,
        grid=(m // tm,),
        in_specs=[pl.BlockSpec((tm, d), lambda i: (i, 0)),
                  pl.BlockSpec((1, d), lambda i: (0, 0))],
        out_specs=pl.BlockSpec((tm, d), lambda i: (i, 0)),
        out_shape=jax.ShapeDtypeStruct((m, d), out_dtype),
        compiler_params=_params("parallel"),
        name="rmsnorm",
    )(x, g.reshape(1, d).astype(F32))


def _mm_kernel(a_ref, b_ref, o_ref):
    o_ref[...] = jnp.dot(a_ref[...], b_ref[...], preferred_element_type=F32).astype(o_ref.dtype)


def matmul(a, b, out_dtype, tm, tn, name):
    m, k = a.shape
    n = b.shape[1]
    return pl.pallas_call(
        _mm_kernel,
        grid=(m // tm, n // tn),
        in_specs=[pl.BlockSpec((tm, k), lambda i, j: (i, 0)),
                  pl.BlockSpec((k, tn), lambda i, j: (0, j))],
        out_specs=pl.BlockSpec((tm, tn), lambda i, j: (i, j)),
        out_shape=jax.ShapeDtypeStruct((m, n), out_dtype),
        compiler_params=_params("parallel", "arbitrary"),
        name=name,
    )(a, b)


def _mm_res_kernel(a_ref, b_ref, r_ref, o_ref):
    o_ref[...] = r_ref[...] + jnp.dot(a_ref[...], b_ref[...], preferred_element_type=F32)


def matmul_residual(a, b, res, tm, tn, name):
    m, k = a.shape
    n = b.shape[1]
    return pl.pallas_call(
        _mm_res_kernel,
        grid=(m // tm, n // tn),
        in_specs=[pl.BlockSpec((tm, k), lambda i, j: (i, 0)),
                  pl.BlockSpec((k, tn), lambda i, j: (0, j)),
                  pl.BlockSpec((tm, tn), lambda i, j: (i, j))],
        out_specs=pl.BlockSpec((tm, tn), lambda i, j: (i, j)),
        out_shape=jax.ShapeDtypeStruct((m, n), F32),
        compiler_params=_params("parallel", "arbitrary"),
        name=name,
    )(a, b, res)


def _merge_kernel(a_ref, b_ref, wa_ref, wb_ref, ga_ref, gb_ref, o_ref):
    ya = jnp.dot(a_ref[...], wa_ref[...], preferred_element_type=F32)
    yb = jnp.dot(b_ref[...], wb_ref[...], preferred_element_type=F32)
    sa = 1.0 / (1.0 + jnp.exp(-ga_ref[...]))
    sb = 1.0 / (1.0 + jnp.exp(-gb_ref[...]))
    o_ref[...] = (sa * ya + sb * yb).astype(o_ref.dtype)


def gated_merge(a, b, wa, wb, gates, tm, tn):
    m, ka = a.shape
    kb = b.shape[1]
    n = wa.shape[1]
    nb = n // tn
    return pl.pallas_call(
        _merge_kernel,
        grid=(m // tm, nb),
        in_specs=[pl.BlockSpec((tm, ka), lambda i, j: (i, 0)),
                  pl.BlockSpec((tm, kb), lambda i, j: (i, 0)),
                  pl.BlockSpec((ka, tn), lambda i, j: (0, j)),
                  pl.BlockSpec((kb, tn), lambda i, j: (0, j)),
                  pl.BlockSpec((tm, tn), lambda i, j: (i, j)),
                  pl.BlockSpec((tm, tn), lambda i, j: (i, j + nb))],
        out_specs=pl.BlockSpec((tm, tn), lambda i, j: (i, j)),
        out_shape=jax.ShapeDtypeStruct((m, n), BF16),
        compiler_params=_params("parallel", "arbitrary"),
        name="gated_merge",
    )(a, b, wa, wb, gates, gates)


def _mla_q_kernel(lat_ref, g_ref, w_ref, c_ref, s1_ref, s2_ref, o_ref, a_sc, *, heads, scale):
    @pl.when(pl.program_id(1) == 0)
    def _():
        a_sc[...] = _rms(lat_ref[...], g_ref[...]).astype(BF16)

    acc = jnp.dot(a_sc[...], w_ref[...], preferred_element_type=F32)
    c, s1, s2 = c_ref[...], s1_ref[...], s2_ref[...]
    for h in range(heads):
        lo = h * MLA_QK_PAD
        o_ref[:, lo:lo + QK_NOPE] = (acc[:, lo:lo + QK_NOPE] * scale).astype(o_ref.dtype)
        rot = _rope128(acc[:, lo + QK_NOPE:lo + MLA_QK_PAD], c, s1, s2)
        o_ref[:, lo + QK_NOPE:lo + MLA_QK_PAD] = (rot * scale).astype(o_ref.dtype)


def mla_q_up(lat, g, w, rope_tabs, seq, q_lora):
    m = lat.shape[0]
    n = w.shape[1]
    tm = _tile("qup_tm", seq)
    tn = _tile("qup_tn", n)
    pos_blocks = seq // tm
    tab_spec = pl.BlockSpec((tm, LANE), lambda i, j: (i % pos_blocks, 0))
    kern = functools.partial(_mla_q_kernel, heads=tn // MLA_QK_PAD,
                             scale=(QK_NOPE + QK_ROPE) ** -0.5 * LOG2E)
    return pl.pallas_call(
        kern,
        grid=(m // tm, n // tn),
        in_specs=[pl.BlockSpec((tm, q_lora), lambda i, j: (i, 0)),
                  pl.BlockSpec((1, q_lora), lambda i, j: (0, 0)),
                  pl.BlockSpec((q_lora, tn), lambda i, j: (0, j)),
                  tab_spec, tab_spec, tab_spec],
        out_specs=pl.BlockSpec((tm, tn), lambda i, j: (i, j)),
        out_shape=jax.ShapeDtypeStruct((m, n), BF16),
        scratch_shapes=[pltpu.VMEM((tm, q_lora), BF16)],
        compiler_params=_params("parallel", "arbitrary"),
        name="mla_q_up",
    )(lat, g.reshape(1, q_lora).astype(F32), w, *rope_tabs)


def _mla_kv_kernel(kv_ref, pe_ref, g_ref, w_ref, c_ref, s1_ref, s2_ref, o_ref, a_sc, *, kv_lora):
    @pl.when(pl.program_id(1) == 0)
    def _():
        a_sc[:, :kv_lora] = _rms(kv_ref[...], g_ref[...]).astype(BF16)
        rot = _rope128(pe_ref[...], c_ref[...], s1_ref[...], s2_ref[...])
        a_sc[:, kv_lora:] = rot.astype(BF16)

    o_ref[...] = jnp.dot(a_sc[...], w_ref[...], preferred_element_type=F32).astype(o_ref.dtype)


def mla_kv_up(lat, g, w_ext, rope_tabs, seq, q_lora, kv_lora):
    m = lat.shape[0]
    n = w_ext.shape[1]
    tm = _tile("kvup_tm", seq)
    tn = _tile("kvup_tn", n)
    pos_blocks = seq // tm
    tab_spec = pl.BlockSpec((tm, LANE), lambda i, j: (i % pos_blocks, 0))
    kv_blk = q_lora // kv_lora
    pe_blk = (q_lora + kv_lora) // LANE
    return pl.pallas_call(
        functools.partial(_mla_kv_kernel, kv_lora=kv_lora),
        grid=(m // tm, n // tn),
        in_specs=[pl.BlockSpec((tm, kv_lora), lambda i, j: (i, kv_blk)),
                  pl.BlockSpec((tm, LANE), lambda i, j: (i, pe_blk)),
                  pl.BlockSpec((1, kv_lora), lambda i, j: (0, 0)),
                  pl.BlockSpec((kv_lora + LANE, tn), lambda i, j: (0, j)),
                  tab_spec, tab_spec, tab_spec],
        out_specs=pl.BlockSpec((tm, tn), lambda i, j: (i, j)),
        out_shape=jax.ShapeDtypeStruct((m, n), BF16),
        scratch_shapes=[pltpu.VMEM((tm, kv_lora + LANE), BF16)],
        compiler_params=_params("parallel", "arbitrary"),
        name="mla_kv_up",
    )(lat, lat, g.reshape(1, kv_lora).astype(F32), w_ext, *rope_tabs)


def _lane_tile(x, n):
    return jnp.concatenate([x] * n, axis=1)


def _softmax_init(m_sc, acc_sc):
    m_sc[...] = jnp.full(m_sc.shape, -jnp.inf, F32)
    acc_sc[...] = jnp.zeros(acc_sc.shape, F32)


def _softmax_update(s, vx, m_sc, acc_sc, c=None):
    tk = s.shape[1]
    m_prev = m_sc[...]
    m_cur = jnp.max(s, axis=1, keepdims=True)
    if c is not None:
        m_cur = m_cur + c
    m_new = jnp.maximum(m_prev, m_cur)
    alpha = jnp.exp2(m_prev - m_new)
    shift = m_new if c is None else m_new - c
    p = jnp.exp2(s - _lane_tile(shift, tk // LANE))
    acc_sc[...] = (_lane_tile(alpha, 2) * acc_sc[...]
                   + jnp.dot(p.astype(BF16), vx, preferred_element_type=F32))
    m_sc[...] = m_new


def _extend_v(v_ref, vx_sc, tk):
    width = v_ref.shape[1]

    def body(j, carry):
        r = pl.ds(pl.multiple_of(j * tk, tk), tk)
        vx_sc[r, :width] = v_ref[r, :]
        vx_sc[r, width:] = jnp.ones((tk, vx_sc.shape[1] - width), vx_sc.dtype)
        return carry

    lax.fori_loop(0, v_ref.shape[0] // tk, body, 0)


def _chunk_rows(j, tk):
    start = j * tk
    return pl.ds(start if isinstance(j, int) else pl.multiple_of(start, tk), tk)


def _qk(q, k):
    return lax.dot_general(q, k, (((1,), (1,)), ((), ())), preferred_element_type=F32)


def _chunk_loop(lo, hi, scores, s_bufs, consume):
    def by_parity(j, fn):
        for parity in (0, 1):
            @pl.when((j & 1) == parity)
            def _():
                fn(j, s_bufs[parity], s_bufs[1 - parity])

    def one(j, cur, nxt):
        nxt[...] = scores(j + 1)
        consume(j, cur)

    def two(j, cur, nxt):
        nxt[...] = scores(j + 1)
        consume(j, cur)
        cur[...] = scores(j + 2)
        consume(j + 1, nxt)

    pairs = lax.shift_right_logical(jnp.maximum(hi - lo, 0), 1)

    def body(t, carry):
        by_parity(lo + 2 * t, two)
        return carry

    lax.fori_loop(0, pairs, body, 0)
    rest = lo + 2 * pairs

    @pl.when(rest < hi)
    def _():
        by_parity(rest, one)


def _first_chunk(s_bufs, scores):
    first = functools.reduce(jnp.logical_and, [pl.program_id(a) == 0 for a in range(3)])

    @pl.when(first)
    def _():
        s_bufs[0][...] = scores(0)


def _last_chunk(n_chunks, next_scores, s_bufs, consume):
    s_bufs[0][...] = next_scores()
    consume(n_chunks - 1, s_bufs[1])


def _next_step(b, h, i, batch, heads, nq):
    n = jnp.minimum((b * heads + h) * nq + i + 1, batch * heads * nq - 1)
    bh = n // nq
    return bh // heads, bh % heads, n % nq


def _mla_attn_kernel(q_ref, k_ref, v_ref, qn_ref, kn_ref, o_ref, vx_sc, sa_sc, sb_sc, m_sc, acc_sc,
                     *, tk):
    @pl.when(pl.program_id(2) == 0)
    def _():
        _extend_v(v_ref, vx_sc, tk)

    _softmax_init(m_sc, acc_sc)
    rows = lambda j: _chunk_rows(j, tk)
    scores = lambda j: _qk(q_ref[...], k_ref[rows(j), :])

    def consume(j, s_ref):
        _softmax_update(s_ref[...], vx_sc[rows(j), :], m_sc, acc_sc)

    nk = k_ref.shape[0] // tk
    s_bufs = (sa_sc, sb_sc)
    _first_chunk(s_bufs, scores)
    _chunk_loop(0, nk - 1, scores, s_bufs, consume)
    _last_chunk(nk, lambda: _qk(qn_ref[...], kn_ref[...]), s_bufs, consume)
    acc = acc_sc[...]
    o_ref[...] = (acc[:, :V_MLA] / acc[:, V_MLA:]).astype(o_ref.dtype)


def mla_attention(q, kv, batch, seq, heads):
    tq = _tile("mla_tq", seq)
    tk = _tile("mla_tk", seq)
    nq, nk = seq // tq, seq // tk
    assert nk % 2 == 0
    v_blk0 = heads * MLA_QK_PAD // V_MLA
    nxt = lambda b, h, i: _next_step(b, h, i, batch, heads, nq)

    def q_next(b, h, i):
        b2, h2, i2 = nxt(b, h, i)
        return b2 * nq + i2, h2

    def k_next(b, h, i):
        b2, h2, _ = nxt(b, h, i)
        return b2 * nk, h2

    return pl.pallas_call(
        functools.partial(_mla_attn_kernel, tk=tk),
        grid=(batch, heads, nq),
        in_specs=[pl.BlockSpec((tq, MLA_QK_PAD), lambda b, h, i: (b * nq + i, h)),
                  pl.BlockSpec((seq, MLA_QK_PAD), lambda b, h, i: (b, h)),
                  pl.BlockSpec((seq, V_MLA), lambda b, h, i: (b, v_blk0 + h)),
                  pl.BlockSpec((tq, MLA_QK_PAD), q_next),
                  pl.BlockSpec((tk, MLA_QK_PAD), k_next)],
        out_specs=pl.BlockSpec((tq, V_MLA), lambda b, h, i: (b * nq + i, h)),
        out_shape=jax.ShapeDtypeStruct((batch * seq, heads * V_MLA), BF16),
        scratch_shapes=[pltpu.VMEM((seq, 2 * V_MLA), BF16),
                        pltpu.VMEM((tq, tk), F32), pltpu.VMEM((tq, tk), F32),
                        pltpu.VMEM((tq, LANE), F32), pltpu.VMEM((tq, 2 * V_MLA), F32)],
        compiler_params=_params("arbitrary", "arbitrary", "arbitrary"),
        name="mla_attention",
    )(q, kv, kv, q, kv)


def _diff_attn_kernel(tab_ref, far_ref, ids_ref, q_ref, k_ref, v_ref, qn_ref, kn_ref, lq1_ref, lk1_ref, lq2_ref,
                      lk2_ref, g_ref, o_ref, q2_sc, vx_sc, bias_sc, sa_sc, sb_sc, m_sc, acc_sc, *, tq, tk,
                      step, near_off0, lambda_init):
    h, i = pl.program_id(1), pl.program_id(2)
    seq = k_ref.shape[0]
    nk = seq // tk

    @pl.when(i == 0)
    def _():
        _extend_v(v_ref, vx_sc, tk)
        for t in range(bias_sc.shape[0]):
            ids = ids_ref[t]
            f = jnp.zeros(ids.shape, F32)
            for b in range(N_BUCKETS):
                f = jnp.where(ids == b, tab_ref[h, b], f)
            f = f * LOG2E
            bias_sc[t] = pltpu.roll(jnp.broadcast_to(f, (tq, tq + tk)), tk + 1, 1,
                                    stride=1, stride_axis=0)[:, :tk]

    def stacked(q):
        lane = lax.broadcasted_iota(jnp.int32, q.shape, 1)
        zero = jnp.zeros_like(q)
        return jnp.concatenate([jnp.where(lane < DH_DIFF, q, zero),
                                jnp.where(lane >= DH_DIFF, q, zero)], axis=0)

    q2_sc[...] = stacked(q_ref[...])
    _softmax_init(m_sc, acc_sc)

    rows = lambda j: _chunk_rows(j, tk)
    scores = lambda j: _qk(q2_sc[...], k_ref[rows(j), :])

    def far(c):
        def consume(j, s_ref):
            _softmax_update(s_ref[...], vx_sc[rows(j), :], m_sc, acc_sc, c)
        return consume

    def near(j, s_ref):
        bias = bias_sc[(j * tk - i * tq - near_off0) // step]
        s = s_ref[...] + jnp.concatenate([bias, bias], axis=0)
        _softmax_update(s, vx_sc[rows(j), :], m_sc, acc_sc)

    j_near = jnp.maximum(i * tq - (MAX_DISTANCE - 1), 0) // tk
    j_right = jnp.minimum((i * tq + tq + MAX_DISTANCE - 1 + tk - 1) // tk, nk)
    far_left = far(tab_ref[h, far_ref[0]] * LOG2E)
    far_right = far(tab_ref[h, far_ref[1]] * LOG2E)
    s_bufs = (sa_sc, sb_sc)
    last = nk - 1
    _first_chunk(s_bufs, scores)
    _chunk_loop(0, j_near, scores, s_bufs, far_left)
    _chunk_loop(j_near, jnp.minimum(j_right, last), scores, s_bufs, near)
    _chunk_loop(j_right, last, scores, s_bufs, far_right)
    next_scores = lambda: _qk(stacked(qn_ref[...]), kn_ref[...])

    @pl.when(j_right > last)
    def _():
        _last_chunk(nk, next_scores, s_bufs, near)

    @pl.when(j_right <= last)
    def _():
        _last_chunk(nk, next_scores, s_bufs, far_right)

    acc = acc_sc[...]
    o = acc[:, :V_DIFF] / acc[:, V_DIFF:]
    lam = (jnp.exp(jnp.sum(lq1_ref[...] * lk1_ref[...], axis=1, keepdims=True))
           - jnp.exp(jnp.sum(lq2_ref[...] * lk2_ref[...], axis=1, keepdims=True)) + lambda_init)
    a = o[:tq] - lam * o[tq:]
    o_ref[...] = (_rms(a, g_ref[...]) * (1.0 - lambda_init)).astype(o_ref.dtype)


def _t5_bucket(rel):
    nb = N_BUCKETS // 2
    max_exact = nb // 2
    ret = (rel > 0).astype(jnp.int32) * nb
    n = jnp.abs(rel)
    nf = jnp.maximum(n, max_exact).astype(F32)
    large = max_exact + (jnp.log(nf / max_exact) / math.log(MAX_DISTANCE / max_exact)
                         * (nb - max_exact)).astype(jnp.int32)
    large = jnp.minimum(large, nb - 1)
    return ret + jnp.where(n < max_exact, n, large)


def diff_attention(qkv, rel_bias, lams, subln_g, lambda_init, batch, seq, heads):
    tq = _tile("diff_tq", seq)
    tk = _tile("diff_tk", seq)
    nq = seq // tq
    step = math.gcd(tq, tk)
    lo = -(tk - 1 + MAX_DISTANCE)
    near_off0 = (lo // step + 1) * step
    near_offs = list(range(near_off0, tq - 1 + MAX_DISTANCE, step))
    win = (jnp.array(near_offs, jnp.int32)[:, None] - (tq - 1)
           + jnp.arange(tq + tk, dtype=jnp.int32)[None, :])
    ids = _t5_bucket(win).reshape(len(near_offs), 1, tq + tk)
    far_ids = _t5_bucket(jnp.array([-MAX_DISTANCE, MAX_DISTANCE], jnp.int32))
    vec = lambda: pl.BlockSpec((1, DH_DIFF), lambda b, h, i: (0, 0))
    kern = functools.partial(_diff_attn_kernel, tq=tq, tk=tk, step=step, near_off0=near_off0,
                             lambda_init=lambda_init)
    nk = seq // tk
    assert nk % 2 == 0
    nxt = lambda b, h, i: _next_step(b, h, i, batch, heads, nq)

    def q_next(b, h, i):
        b2, h2, i2 = nxt(b, h, i)
        return b2 * nq + i2, h2

    def k_next(b, h, i):
        b2, h2, _ = nxt(b, h, i)
        return b2 * nk, heads + h2

    return pl.pallas_call(
        kern,
        grid=(batch, heads, nq),
        in_specs=[pl.BlockSpec(memory_space=pltpu.SMEM),
                  pl.BlockSpec(memory_space=pltpu.SMEM),
                  pl.BlockSpec(ids.shape, lambda b, h, i: (0, 0, 0)),
                  pl.BlockSpec((tq, V_DIFF), lambda b, h, i: (b * nq + i, h)),
                  pl.BlockSpec((seq, V_DIFF), lambda b, h, i: (b, heads + h)),
                  pl.BlockSpec((seq, V_DIFF), lambda b, h, i: (b, 2 * heads + h)),
                  pl.BlockSpec((tq, V_DIFF), q_next),
                  pl.BlockSpec((tk, V_DIFF), k_next),
                  vec(), vec(), vec(), vec(),
                  pl.BlockSpec((1, V_DIFF), lambda b, h, i: (0, 0))],
        out_specs=pl.BlockSpec((tq, V_DIFF), lambda b, h, i: (b * nq + i, h)),
        out_shape=jax.ShapeDtypeStruct((batch * seq, heads * V_DIFF), BF16),
        scratch_shapes=[pltpu.VMEM((2 * tq, V_DIFF), BF16), pltpu.VMEM((seq, 2 * V_DIFF), BF16),
                        pltpu.VMEM((len(near_offs), tq, tk), F32),
                        pltpu.VMEM((2 * tq, tk), F32), pltpu.VMEM((2 * tq, tk), F32),
                        pltpu.VMEM((2 * tq, LANE), F32), pltpu.VMEM((2 * tq, 2 * V_DIFF), F32)],
        compiler_params=_params("arbitrary", "arbitrary", "arbitrary"),
        name="diff_attention",
    )(rel_bias.T.astype(F32), far_ids, ids, qkv, qkv, qkv, qkv, qkv,
      *[v.reshape(1, DH_DIFF).astype(F32) for v in lams], subln_g.reshape(1, V_DIFF).astype(F32))


def _ffn_up_kernel(a_ref, prev_ref, next_ref, wg_ref, wu_ref, cwg_ref, cwu_ref, cbg_ref, cbu_ref,
                   o_ref, a_sc, *, tm, tiles_per_seq):
    i = pl.program_id(0)

    @pl.when(pl.program_id(1) == 0)
    def _():
        pos = i % tiles_per_seq
        prev = prev_ref[...]
        nxt = next_ref[...]
        a_sc[:HALO] = jnp.where(pos == 0, jnp.zeros_like(prev), prev)
        a_sc[HALO:HALO + tm] = a_ref[...]
        a_sc[HALO + tm:] = jnp.where(pos == tiles_per_seq - 1, jnp.zeros_like(nxt), nxt)

    def conv(w_ref, cw_ref, cb_ref):
        r = jnp.dot(a_sc[...], w_ref[...], preferred_element_type=F32)
        rows = r.shape[0]
        r_prev = pltpu.roll(r, 1, 0)[HALO:HALO + tm]
        r_next = pltpu.roll(r, rows - 1, 0)[HALO:HALO + tm]
        cw = cw_ref[...]
        return (r_prev * cw[0:1] + r[HALO:HALO + tm] * cw[1:2] + r_next * cw[2:3]) + cb_ref[...]

    gate = conv(wg_ref, cwg_ref, cbg_ref)
    up = conv(wu_ref, cwu_ref, cbu_ref)
    o_ref[...] = (gate * (1.0 / (1.0 + jnp.exp(-gate))) * up).astype(o_ref.dtype)


def ffn_up(h, w_up, conv_w, conv_b, seq):
    m, d = h.shape
    d_ff = w_up.shape[1] // 2
    tm = _tile("ffn_tm", seq)
    tn = _tile("ffn_tn", d_ff)
    nb = d_ff // tn
    hb = tm // HALO
    last_hb = m // HALO - 1
    kern = functools.partial(_ffn_up_kernel, tm=tm, tiles_per_seq=seq // tm)
    return pl.pallas_call(
        kern,
        grid=(m // tm, nb),
        in_specs=[pl.BlockSpec((tm, d), lambda i, j: (i, 0)),
                  pl.BlockSpec((HALO, d), lambda i, j: (jnp.maximum(i * hb - 1, 0), 0)),
                  pl.BlockSpec((HALO, d), lambda i, j: (jnp.minimum((i + 1) * hb, last_hb), 0)),
                  pl.BlockSpec((d, tn), lambda i, j: (0, j)),
                  pl.BlockSpec((d, tn), lambda i, j: (0, j + nb)),
                  pl.BlockSpec((3, tn), lambda i, j: (0, j)),
                  pl.BlockSpec((3, tn), lambda i, j: (0, j + nb)),
                  pl.BlockSpec((1, tn), lambda i, j: (0, j)),
                  pl.BlockSpec((1, tn), lambda i, j: (0, j + nb))],
        out_specs=pl.BlockSpec((tm, tn), lambda i, j: (i, j)),
        out_shape=jax.ShapeDtypeStruct((m, d_ff), BF16),
        scratch_shapes=[pltpu.VMEM((tm + 2 * HALO, d), BF16)],
        compiler_params=_params("parallel", "arbitrary"),
        name="ffn_up_conv_silu",
    )(h, h, h, w_up, w_up, conv_w, conv_w, conv_b, conv_b)


def _prep_w_in(w_in, d_model, heads):
    dq = heads * 2 * DH_DIFF
    dv = heads * V_DIFF
    qkv_cols = 2 * dq + dv
    lat_cols = w_in.shape[1] - qkv_cols - 2 * d_model
    lat_pad = -lat_cols % LANE
    col_scale = jnp.concatenate([jnp.full((dq,), DH_DIFF ** -0.5 * LOG2E, F32),
                                 jnp.ones((dq + dv,), F32)])
    w_qkv = (w_in[:, :qkv_cols] * col_scale).astype(BF16)
    w_lat = jnp.pad(w_in[:, qkv_cols:qkv_cols + lat_cols], ((0, 0), (0, lat_pad))).astype(BF16)
    w_gate = w_in[:, qkv_cols + lat_cols:].astype(BF16)
    return w_qkv, w_lat, w_gate


def _prep_w_q_up(w, heads):
    k = w.shape[0]
    w = w.reshape(k, heads, QK_NOPE + QK_ROPE)
    w = jnp.pad(w, ((0, 0), (0, 0), (0, MLA_QK_PAD - QK_NOPE - QK_ROPE)))
    return w.reshape(k, heads * MLA_QK_PAD).astype(BF16)


def _prep_w_kv_up(w, heads):
    kv = w.shape[0]
    w = w.reshape(kv, heads, QK_NOPE + V_MLA)
    top_k = jnp.pad(w[:, :, :QK_NOPE], ((0, 0), (0, 0), (0, MLA_QK_PAD - QK_NOPE)))
    eye = jnp.pad(jnp.eye(QK_ROPE, dtype=F32), ((0, LANE - QK_ROPE), (QK_NOPE, MLA_QK_PAD - QK_NOPE - QK_ROPE)))
    bot_k = jnp.broadcast_to(eye[:, None, :], (LANE, heads, MLA_QK_PAD))
    w_k = jnp.concatenate([top_k, bot_k], axis=0).reshape(kv + LANE, heads * MLA_QK_PAD)
    w_v = jnp.pad(w[:, :, QK_NOPE:].reshape(kv, heads * V_MLA), ((0, LANE), (0, 0)))
    return jnp.concatenate([w_k, w_v], axis=1).astype(BF16)


def _rope_tables(seq):
    pos = jnp.arange(seq, dtype=F32)
    inv_freq = ROPE_THETA ** (-jnp.arange(0, QK_ROPE, 2, dtype=F32) / QK_ROPE)
    ang = pos[:, None] * inv_freq[None, :]
    cos, sin = jnp.cos(ang), jnp.sin(ang)
    half = QK_ROPE // 2
    z = lambda n: jnp.zeros((seq, n), F32)
    c = jnp.concatenate([cos, cos, z(LANE - QK_ROPE)], axis=1)
    s1 = jnp.concatenate([-sin, z(LANE - half)], axis=1)
    s2 = jnp.concatenate([z(half), sin, z(LANE - QK_ROPE)], axis=1)
    return c, s1, s2


def _encoder_layer(x, batch, seq, layer_idx, rel_bias, wts):
    (rms_attn_g, w_qkv, w_lat, w_gate, lams, diff_subln_g, mla_q_norm_g, w_q_up, mla_kv_norm_g,
     w_kv_ext, w_branch_a, w_branch_b, w_out, rms_ffn_g, w_ffn_up, conv_w, conv_b, w_ffn_down) = wts
    m, d_model = x.shape
    heads = w_branch_a.shape[0] // V_DIFF
    q_lora = mla_q_norm_g.shape[0]
    kv_lora = mla_kv_norm_g.shape[0]
    assert q_lora % kv_lora == 0 and (q_lora + kv_lora) % LANE == 0
    lambda_init = 0.8 - 0.6 * math.exp(-0.3 * layer_idx)
    tm = _tile("mm_tm", seq)

    h = rmsnorm(x, rms_attn_g, BF16)
    qkv = matmul(h, w_qkv, BF16, tm, _tile("mm_tn", w_qkv.shape[1]), "in_proj_qkv")
    lat = matmul(h, w_lat, F32, _tile("lat_tm", seq), w_lat.shape[1], "in_proj_latent")
    gates = matmul(h, w_gate, F32, tm, _tile("mm_tn", d_model), "in_proj_gates")

    a_out = diff_attention(qkv, rel_bias, lams, diff_subln_g, lambda_init, batch, seq, heads)

    rope_tabs = _rope_tables(seq)
    q_mla = mla_q_up(lat, mla_q_norm_g, w_q_up, rope_tabs, seq, q_lora)
    kv_mla = mla_kv_up(lat, mla_kv_norm_g, w_kv_ext, rope_tabs, seq, q_lora, kv_lora)
    b_out = mla_attention(q_mla, kv_mla, batch, seq, heads)

    merged = gated_merge(a_out, b_out, w_branch_a, w_branch_b, gates, tm, _tile("mm_tn", d_model))
    x = matmul_residual(merged, w_out, x, tm, _tile("mm_tn", d_model), "out_proj")

    h2 = rmsnorm(x, rms_ffn_g, BF16)
    act = ffn_up(h2, w_ffn_up, conv_w, conv_b, seq)
    return matmul_residual(act, w_ffn_down, x, _tile("down_tm", seq), _tile("down_tn", d_model),
                           "ffn_down")


def kernel(x_prompt, x_sample, rel_bias, final_norm_g, rms_attn_g, w_in, lambda_q1, lambda_k1,
           lambda_q2, lambda_k2, diff_subln_g, mla_q_norm_g, w_mla_q_up, mla_kv_norm_g,
           w_mla_kv_up, w_branch_a, w_branch_b, w_out, rms_ffn_g, w_ffn_up, conv_w, conv_b,
           w_ffn_down):
    depth = w_in.shape[0]
    d_model = x_prompt.shape[-1]
    heads = w_branch_a.shape[1] // V_DIFF

    layers = []
    for l in range(depth):
        w_qkv, w_lat, w_gate = _prep_w_in(w_in[l], d_model, heads)
        layers.append((
            rms_attn_g[l], w_qkv, w_lat, w_gate,
            (lambda_q1[l], lambda_k1[l], lambda_q2[l], lambda_k2[l]), diff_subln_g[l],
            mla_q_norm_g[l], _prep_w_q_up(w_mla_q_up[l], heads),
            mla_kv_norm_g[l], _prep_w_kv_up(w_mla_kv_up[l], heads),
            w_branch_a[l].astype(BF16), w_branch_b[l].astype(BF16), w_out[l].astype(BF16),
            rms_ffn_g[l], w_ffn_up[l].astype(BF16), conv_w[l].astype(F32),
            conv_b[l].reshape(1, -1).astype(F32), w_ffn_down[l].astype(BF16)))

    def trunk(x):
        batch, seq, _ = x.shape
        y = x.reshape(batch * seq, d_model)
        for l in range(depth):
            y = _encoder_layer(y, batch, seq, l, rel_bias, layers[l])
        return rmsnorm(y, final_norm_g, x.dtype).reshape(x.shape)

    return (trunk(x_prompt), trunk(x_sample))
```

```python
import functools
import math

import jax
import jax.numpy as jnp
from jax import lax
from jax.experimental import pallas as pl
from jax.experimental.pallas import tpu as pltpu

DH_DIFF = 64
V_DIFF = 2 * DH_DIFF
QK_NOPE = 128
QK_ROPE = 64
V_MLA = 128
ROPE_THETA = 10000.0
N_BUCKETS = 32
MAX_DISTANCE = 128
EPS = 1e-6
LOG2E = math.log2(math.e)

LANE = 128
BF16_SUBLANE = 16
VMEM_LIMIT_BYTES = 56 * 1024 * 1024

MLA_QK_PAD = 2 * LANE
HALO = BF16_SUBLANE

TILES = dict(
    norm_tm=512,
    mm_tm=1024, mm_tn=512,
    lat_tm=512,
    qup_tm=1024, qup_tn=1024,
    kvup_tm=1024, kvup_tn=1024,
    ffn_tm=1024, ffn_tn=256,
    down_tm=512, down_tn=256,
    diff_tq=512, diff_tk=1024,
    mla_tq=1024, mla_tk=2048,
)

F32 = jnp.float32
BF16 = jnp.bfloat16


def _tile(name, dim):
    t = min(TILES[name], dim)
    assert dim % t == 0, (name, dim, t)
    return t


def _params(*sem):
    return pltpu.CompilerParams(dimension_semantics=sem, vmem_limit_bytes=VMEM_LIMIT_BYTES)


def _rms(x, g):
    return x * lax.rsqrt(jnp.mean(x * x, axis=-1, keepdims=True) + EPS) * g


def _rope128(p, c, s1, s2):
    return (p * c + pltpu.roll(p, LANE - QK_ROPE // 2, 1) * s1
            + pltpu.roll(p, QK_ROPE // 2, 1) * s2)


def _rmsnorm_kernel(x_ref, g_ref, o_ref):
    o_ref[...] = _rms(x_ref[...].astype(F32), g_ref[...]).astype(o_ref.dtype)


def rmsnorm(x, g, out_dtype):
    m, d = x.shape
    tm = _tile("norm_tm", m)
    return pl.pallas_call(
        _rmsnorm_kernel,
        grid=(m // tm,),
        in_specs=[pl.BlockSpec((tm, d), lambda i: (i, 0)),
                  pl.BlockSpec((1, d), lambda i: (0, 0))],
        out_specs=pl.BlockSpec((tm, d), lambda i: (i, 0)),
        out_shape=jax.ShapeDtypeStruct((m, d), out_dtype),
        compiler_params=_params("parallel"),
        name="rmsnorm",
    )(x, g.reshape(1, d).astype(F32))


def _mm_kernel(a_ref, b_ref, o_ref):
    o_ref[...] = jnp.dot(a_ref[...], b_ref[...], preferred_element_type=F32).astype(o_ref.dtype)


def matmul(a, b, out_dtype, tm, tn, name):
    m, k = a.shape
    n = b.shape[1]
    return pl.pallas_call(
        _mm_kernel,
        grid=(m // tm, n // tn),
        in_specs=[pl.BlockSpec((tm, k), lambda i, j: (i, 0)),
                  pl.BlockSpec((k, tn), lambda i, j: (0, j))],
        out_specs=pl.BlockSpec((tm, tn), lambda i, j: (i, j)),
        out_shape=jax.ShapeDtypeStruct((m, n), out_dtype),
        compiler_params=_params("parallel", "arbitrary"),
        name=name,
    )(a, b)


def _mm_res_kernel(a_ref, b_ref, r_ref, o_ref):
    o_ref[...] = r_ref[...] + jnp.dot(a_ref[...], b_ref[...], preferred_element_type=F32)


def matmul_residual(a, b, res, tm, tn, name):
    m, k = a.shape
    n = b.shape[1]
    return pl.pallas_call(
        _mm_res_kernel,
        grid=(m // tm, n // tn),
        in_specs=[pl.BlockSpec((tm, k), lambda i, j: (i, 0)),
                  pl.BlockSpec((k, tn), lambda i, j: (0, j)),
                  pl.BlockSpec((tm, tn), lambda i, j: (i, j))],
        out_specs=pl.BlockSpec((tm, tn), lambda i, j: (i, j)),
        out_shape=jax.ShapeDtypeStruct((m, n), F32),
        compiler_params=_params("parallel", "arbitrary"),
        name=name,
    )(a, b, res)


def _merge_kernel(a_ref, b_ref, wa_ref, wb_ref, ga_ref, gb_ref, o_ref):
    ya = jnp.dot(a_ref[...], wa_ref[...], preferred_element_type=F32)
    yb = jnp.dot(b_ref[...], wb_ref[...], preferred_element_type=F32)
    sa = 1.0 / (1.0 + jnp.exp(-ga_ref[...]))
    sb = 1.0 / (1.0 + jnp.exp(-gb_ref[...]))
    o_ref[...] = (sa * ya + sb * yb).astype(o_ref.dtype)


def gated_merge(a, b, wa, wb, gates, tm, tn):
    m, ka = a.shape
    kb = b.shape[1]
    n = wa.shape[1]
    nb = n // tn
    return pl.pallas_call(
        _merge_kernel,
        grid=(m // tm, nb),
        in_specs=[pl.BlockSpec((tm, ka), lambda i, j: (i, 0)),
                  pl.BlockSpec((tm, kb), lambda i, j: (i, 0)),
                  pl.BlockSpec((ka, tn), lambda i, j: (0, j)),
                  pl.BlockSpec((kb, tn), lambda i, j: (0, j)),
                  pl.BlockSpec((tm, tn), lambda i, j: (i, j)),
                  pl.BlockSpec((tm, tn), lambda i, j: (i, j + nb))],
        out_specs=pl.BlockSpec((tm, tn), lambda i, j: (i, j)),
        out_shape=jax.ShapeDtypeStruct((m, n), BF16),
        compiler_params=_params("parallel", "arbitrary"),
        name="gated_merge",
    )(a, b, wa, wb, gates, gates)


def _mla_q_kernel(lat_ref, g_ref, w_ref, c_ref, s1_ref, s2_ref, o_ref, a_sc, *, heads, scale):
    @pl.when(pl.program_id(1) == 0)
    def _():
        a_sc[...] = _rms(lat_ref[...], g_ref[...]).astype(BF16)

    acc = jnp.dot(a_sc[...], w_ref[...], preferred_element_type=F32)
    c, s1, s2 = c_ref[...], s1_ref[...], s2_ref[...]
    for h in range(heads):
        lo = h * MLA_QK_PAD
        o_ref[:, lo:lo + QK_NOPE] = (acc[:, lo:lo + QK_NOPE] * scale).astype(o_ref.dtype)
        rot = _rope128(acc[:, lo + QK_NOPE:lo + MLA_QK_PAD], c, s1, s2)
        o_ref[:, lo + QK_NOPE:lo + MLA_QK_PAD] = (rot * scale).astype(o_ref.dtype)


def mla_q_up(lat, g, w, rope_tabs, seq, q_lora):
    m = lat.shape[0]
    n = w.shape[1]
    tm = _tile("qup_tm", seq)
    tn = _tile("qup_tn", n)
    pos_blocks = seq // tm
    tab_spec = pl.BlockSpec((tm, LANE), lambda i, j: (i % pos_blocks, 0))
    kern = functools.partial(_mla_q_kernel, heads=tn // MLA_QK_PAD,
                             scale=(QK_NOPE + QK_ROPE) ** -0.5 * LOG2E)
    return pl.pallas_call(
        kern,
        grid=(m // tm, n // tn),
        in_specs=[pl.BlockSpec((tm, q_lora), lambda i, j: (i, 0)),
                  pl.BlockSpec((1, q_lora), lambda i, j: (0, 0)),
                  pl.BlockSpec((q_lora, tn), lambda i, j: (0, j)),
                  tab_spec, tab_spec, tab_spec],
        out_specs=pl.BlockSpec((tm, tn), lambda i, j: (i, j)),
        out_shape=jax.ShapeDtypeStruct((m, n), BF16),
        scratch_shapes=[pltpu.VMEM((tm, q_lora), BF16)],
        compiler_params=_params("parallel", "arbitrary"),
        name="mla_q_up",
    )(lat, g.reshape(1, q_lora).astype(F32), w, *rope_tabs)


def _mla_kv_kernel(kv_ref, pe_ref, g_ref, w_ref, c_ref, s1_ref, s2_ref, o_ref, a_sc, *, kv_lora):
    @pl.when(pl.program_id(1) == 0)
    def _():
        a_sc[:, :kv_lora] = _rms(kv_ref[...], g_ref[...]).astype(BF16)
        rot = _rope128(pe_ref[...], c_ref[...], s1_ref[...], s2_ref[...])
        a_sc[:, kv_lora:] = rot.astype(BF16)

    o_ref[...] = jnp.dot(a_sc[...], w_ref[...], preferred_element_type=F32).astype(o_ref.dtype)


def mla_kv_up(lat, g, w_ext, rope_tabs, seq, q_lora, kv_lora):
    m = lat.shape[0]
    n = w_ext.shape[1]
    tm = _tile("kvup_tm", seq)
    tn = _tile("kvup_tn", n)
    pos_blocks = seq // tm
    tab_spec = pl.BlockSpec((tm, LANE), lambda i, j: (i % pos_blocks, 0))
    kv_blk = q_lora // kv_lora
    pe_blk = (q_lora + kv_lora) // LANE
    return pl.pallas_call(
        functools.partial(_mla_kv_kernel, kv_lora=kv_lora),
        grid=(m // tm, n // tn),
        in_specs=[pl.BlockSpec((tm, kv_lora), lambda i, j: (i, kv_blk)),
                  pl.BlockSpec((tm, LANE), lambda i, j: (i, pe_blk)),
                  pl.BlockSpec((1, kv_lora), lambda i, j: (0, 0)),
                  pl.BlockSpec((kv_lora + LANE, tn), lambda i, j: (0, j)),
                  tab_spec, tab_spec, tab_spec],
        out_specs=pl.BlockSpec((tm, tn), lambda i, j: (i, j)),
        out_shape=jax.ShapeDtypeStruct((m, n), BF16),
        scratch_shapes=[pltpu.VMEM((tm, kv_lora + LANE), BF16)],
        compiler_params=_params("parallel", "arbitrary"),
        name="mla_kv_up",
    )(lat, lat, g.reshape(1, kv_lora).astype(F32), w_ext, *rope_tabs)


def _lane_tile(x, n):
    return jnp.concatenate([x] * n, axis=1)


def _softmax_init(m_sc, acc_sc):
    m_sc[...] = jnp.full(m_sc.shape, -jnp.inf, F32)
    acc_sc[...] = jnp.zeros(acc_sc.shape, F32)


def _softmax_update(s, vx, m_sc, acc_sc, c=None):
    tk = s.shape[1]
    m_prev = m_sc[...]
    m_cur = jnp.max(s, axis=1, keepdims=True)
    if c is not None:
        m_cur = m_cur + c
    m_new = jnp.maximum(m_prev, m_cur)
    alpha = jnp.exp2(m_prev - m_new)
    shift = m_new if c is None else m_new - c
    p = jnp.exp2(s - _lane_tile(shift, tk // LANE))
    acc_sc[...] = (_lane_tile(alpha, 2) * acc_sc[...]
                   + jnp.dot(p.astype(BF16), vx, preferred_element_type=F32))
    m_sc[...] = m_new


def _extend_v(v_ref, vx_sc, tk):
    width = v_ref.shape[1]

    def body(j, carry):
        r = pl.ds(pl.multiple_of(j * tk, tk), tk)
        vx_sc[r, :width] = v_ref[r, :]
        vx_sc[r, width:] = jnp.ones((tk, vx_sc.shape[1] - width), vx_sc.dtype)
        return carry

    lax.fori_loop(0, v_ref.shape[0] // tk, body, 0)


def _chunk_rows(j, tk):
    start = j * tk
    return pl.ds(start if isinstance(j, int) else pl.multiple_of(start, tk), tk)


def _qk(q, k):
    return lax.dot_general(q, k, (((1,), (1,)), ((), ())), preferred_element_type=F32)


def _chunk_loop(lo, hi, scores, s_bufs, consume):
    def by_parity(j, fn):
        for parity in (0, 1):
            @pl.when((j & 1) == parity)
            def _():
                fn(j, s_bufs[parity], s_bufs[1 - parity])

    def one(j, cur, nxt):
        nxt[...] = scores(j + 1)
        consume(j, cur)

    def two(j, cur, nxt):
        nxt[...] = scores(j + 1)
        consume(j, cur)
        cur[...] = scores(j + 2)
        consume(j + 1, nxt)

    pairs = lax.shift_right_logical(jnp.maximum(hi - lo, 0), 1)

    def body(t, carry):
        by_parity(lo + 2 * t, two)
        return carry

    lax.fori_loop(0, pairs, body, 0)
    rest = lo + 2 * pairs

    @pl.when(rest < hi)
    def _():
        by_parity(rest, one)


def _first_chunk(s_bufs, scores):
    first = functools.reduce(jnp.logical_and, [pl.program_id(a) == 0 for a in range(3)])

    @pl.when(first)
    def _():
        s_bufs[0][...] = scores(0)


def _last_chunk(n_chunks, next_scores, s_bufs, consume):
    s_bufs[0][...] = next_scores()
    consume(n_chunks - 1, s_bufs[1])


def _next_step(b, h, i, batch, heads, nq):
    n = jnp.minimum((b * heads + h) * nq + i + 1, batch * heads * nq - 1)
    bh = n // nq
    return bh // heads, bh % heads, n % nq


def _mla_attn_kernel(q_ref, k_ref, v_ref, qn_ref, kn_ref, o_ref, vx_sc, sa_sc, sb_sc, m_sc, acc_sc,
                     *, tk):
    @pl.when(pl.program_id(2) == 0)
    def _():
        _extend_v(v_ref, vx_sc, tk)

    _softmax_init(m_sc, acc_sc)
    rows = lambda j: _chunk_rows(j, tk)
    scores = lambda j: _qk(q_ref[...], k_ref[rows(j), :])

    def consume(j, s_ref):
        _softmax_update(s_ref[...], vx_sc[rows(j), :], m_sc, acc_sc)

    nk = k_ref.shape[0] // tk
    s_bufs = (sa_sc, sb_sc)
    _first_chunk(s_bufs, scores)
    _chunk_loop(0, nk - 1, scores, s_bufs, consume)
    _last_chunk(nk, lambda: _qk(qn_ref[...], kn_ref[...]), s_bufs, consume)
    acc = acc_sc[...]
    o_ref[...] = (acc[:, :V_MLA] / acc[:, V_MLA:]).astype(o_ref.dtype)


def mla_attention(q, kv, batch, seq, heads):
    tq = _tile("mla_tq", seq)
    tk = _tile("mla_tk", seq)
    nq, nk = seq // tq, seq // tk
    assert nk % 2 == 0
    v_blk0 = heads * MLA_QK_PAD // V_MLA
    nxt = lambda b, h, i: _next_step(b, h, i, batch, heads, nq)

    def q_next(b, h, i):
        b2, h2, i2 = nxt(b, h, i)
        return b2 * nq + i2, h2

    def k_next(b, h, i):
        b2, h2, _ = nxt(b, h, i)
        return b2 * nk, h2

    return pl.pallas_call(
        functools.partial(_mla_attn_kernel, tk=tk),
        grid=(batch, heads, nq),
        in_specs=[pl.BlockSpec((tq, MLA_QK_PAD), lambda b, h, i: (b * nq + i, h)),
                  pl.BlockSpec((seq, MLA_QK_PAD), lambda b, h, i: (b, h)),
                  pl.BlockSpec((seq, V_MLA), lambda b, h, i: (b, v_blk0 + h)),
                  pl.BlockSpec((tq, MLA_QK_PAD), q_next),
                  pl.BlockSpec((tk, MLA_QK_PAD), k_next)],
        out_specs=pl.BlockSpec((tq, V_MLA), lambda b, h, i: (b * nq + i, h)),
        out_shape=jax.ShapeDtypeStruct((batch * seq, heads * V_MLA), BF16),
        scratch_shapes=[pltpu.VMEM((seq, 2 * V_MLA), BF16),
                        pltpu.VMEM((tq, tk), F32), pltpu.VMEM((tq, tk), F32),
                        pltpu.VMEM((tq, LANE), F32), pltpu.VMEM((tq, 2 * V_MLA), F32)],
        compiler_params=_params("arbitrary", "arbitrary", "arbitrary"),
        name="mla_attention",
    )(q, kv, kv, q, kv)


def _diff_attn_kernel(tab_ref, far_ref, ids_ref, q_ref, k_ref, v_ref, qn_ref, kn_ref, lq1_ref, lk1_ref, lq2_ref,
                      lk2_ref, g_ref, o_ref, q2_sc, vx_sc, bias_sc, sa_sc, sb_sc, m_sc, acc_sc, *, tq, tk,
                      step, near_off0, lambda_init):
    h, i = pl.program_id(1), pl.program_id(2)
    seq = k_ref.shape[0]
    nk = seq // tk

    @pl.when(i == 0)
    def _():
        _extend_v(v_ref, vx_sc, tk)
        for t in range(bias_sc.shape[0]):
            ids = ids_ref[t]
            f = jnp.zeros(ids.shape, F32)
            for b in range(N_BUCKETS):
                f = jnp.where(ids == b, tab_ref[h, b], f)
            f = f * LOG2E
            bias_sc[t] = pltpu.roll(jnp.broadcast_to(f, (tq, tq + tk)), tk + 1, 1,
                                    stride=1, stride_axis=0)[:, :tk]

    def stacked(q):
        lane = lax.broadcasted_iota(jnp.int32, q.shape, 1)
        zero = jnp.zeros_like(q)
        return jnp.concatenate([jnp.where(lane < DH_DIFF, q, zero),
                                jnp.where(lane >= DH_DIFF, q, zero)], axis=0)

    q2_sc[...] = stacked(q_ref[...])
    _softmax_init(m_sc, acc_sc)

    rows = lambda j: _chunk_rows(j, tk)
    scores = lambda j: _qk(q2_sc[...], k_ref[rows(j), :])

    def far(c):
        def consume(j, s_ref):
            _softmax_update(s_ref[...], vx_sc[rows(j), :], m_sc, acc_sc, c)
        return consume

    def near(j, s_ref):
        bias = bias_sc[(j * tk - i * tq - near_off0) // step]
        s = s_ref[...] + jnp.concatenate([bias, bias], axis=0)
        _softmax_update(s, vx_sc[rows(j), :], m_sc, acc_sc)

    j_near = jnp.maximum(i * tq - (MAX_DISTANCE - 1), 0) // tk
    j_right = jnp.minimum((i * tq + tq + MAX_DISTANCE - 1 + tk - 1) // tk, nk)
    far_left = far(tab_ref[h, far_ref[0]] * LOG2E)
    far_right = far(tab_ref[h, far_ref[1]] * LOG2E)
    s_bufs = (sa_sc, sb_sc)
    last = nk - 1
    _first_chunk(s_bufs, scores)
    _chunk_loop(0, j_near, scores, s_bufs, far_left)
    _chunk_loop(j_near, jnp.minimum(j_right, last), scores, s_bufs, near)
    _chunk_loop(j_right, last, scores, s_bufs, far_right)
    next_scores = lambda: _qk(stacked(qn_ref[...]), kn_ref[...])

    @pl.when(j_right > last)
    def _():
        _last_chunk(nk, next_scores, s_bufs, near)

    @pl.when(j_right <= last)
    def _():
        _last_chunk(nk, next_scores, s_bufs, far_right)

    acc = acc_sc[...]
    o = acc[:, :V_DIFF] / acc[:, V_DIFF:]
    lam = (jnp.exp(jnp.sum(lq1_ref[...] * lk1_ref[...], axis=1, keepdims=True))
           - jnp.exp(jnp.sum(lq2_ref[...] * lk2_ref[...], axis=1, keepdims=True)) + lambda_init)
    a = o[:tq] - lam * o[tq:]
    o_ref[...] = (_rms(a, g_ref[...]) * (1.0 - lambda_init)).astype(o_ref.dtype)


def _t5_bucket(rel):
    nb = N_BUCKETS // 2
    max_exact = nb // 2
    ret = (rel > 0).astype(jnp.int32) * nb
    n = jnp.abs(rel)
    nf = jnp.maximum(n, max_exact).astype(F32)
    large = max_exact + (jnp.log(nf / max_exact) / math.log(MAX_DISTANCE / max_exact)
                         * (nb - max_exact)).astype(jnp.int32)
    large = jnp.minimum(large, nb - 1)
    return ret + jnp.where(n < max_exact, n, large)


def diff_attention(qkv, rel_bias, lams, subln_g, lambda_init, batch, seq, heads):
    tq = _tile("diff_tq", seq)
    tk = _tile("diff_tk", seq)
    nq = seq // tq
    step = math.gcd(tq, tk)
    lo = -(tk - 1 + MAX_DISTANCE)
    near_off0 = (lo // step + 1) * step
    near_offs = list(range(near_off0, tq - 1 + MAX_DISTANCE, step))
    win = (jnp.array(near_offs, jnp.int32)[:, None] - (tq - 1)
           + jnp.arange(tq + tk, dtype=jnp.int32)[None, :])
    ids = _t5_bucket(win).reshape(len(near_offs), 1, tq + tk)
    far_ids = _t5_bucket(jnp.array([-MAX_DISTANCE, MAX_DISTANCE], jnp.int32))
    vec = lambda: pl.BlockSpec((1, DH_DIFF), lambda b, h, i: (0, 0))
    kern = functools.partial(_diff_attn_kernel, tq=tq, tk=tk, step=step, near_off0=near_off0,
                             lambda_init=lambda_init)
    nk = seq // tk
    assert nk % 2 == 0
    nxt = lambda b, h, i: _next_step(b, h, i, batch, heads, nq)

    def q_next(b, h, i):
        b2, h2, i2 = nxt(b, h, i)
        return b2 * nq + i2, h2

    def k_next(b, h, i):
        b2, h2, _ = nxt(b, h, i)
        return b2 * nk, heads + h2

    return pl.pallas_call(
        kern,
        grid=(batch, heads, nq),
        in_specs=[pl.BlockSpec(memory_space=pltpu.SMEM),
                  pl.BlockSpec(memory_space=pltpu.SMEM),
                  pl.BlockSpec(ids.shape, lambda b, h, i: (0, 0, 0)),
                  pl.BlockSpec((tq, V_DIFF), lambda b, h, i: (b * nq + i, h)),
                  pl.BlockSpec((seq, V_DIFF), lambda b, h, i: (b, heads + h)),
                  pl.BlockSpec((seq, V_DIFF), lambda b, h, i: (b, 2 * heads + h)),
                  pl.BlockSpec((tq, V_DIFF), q_next),
                  pl.BlockSpec((tk, V_DIFF), k_next),
                  vec(), vec(), vec(), vec(),
                  pl.BlockSpec((1, V_DIFF), lambda b, h, i: (0, 0))],
        out_specs=pl.BlockSpec((tq, V_DIFF), lambda b, h, i: (b * nq + i, h)),
        out_shape=jax.ShapeDtypeStruct((batch * seq, heads * V_DIFF), BF16),
        scratch_shapes=[pltpu.VMEM((2 * tq, V_DIFF), BF16), pltpu.VMEM((seq, 2 * V_DIFF), BF16),
                        pltpu.VMEM((len(near_offs), tq, tk), F32),
                        pltpu.VMEM((2 * tq, tk), F32), pltpu.VMEM((2 * tq, tk), F32),
                        pltpu.VMEM((2 * tq, LANE), F32), pltpu.VMEM((2 * tq, 2 * V_DIFF), F32)],
        compiler_params=_params("arbitrary", "arbitrary", "arbitrary"),
        name="diff_attention",
    )(rel_bias.T.astype(F32), far_ids, ids, qkv, qkv, qkv, qkv, qkv,
      *[v.reshape(1, DH_DIFF).astype(F32) for v in lams], subln_g.reshape(1, V_DIFF).astype(F32))


def _ffn_up_kernel(a_ref, prev_ref, next_ref, wg_ref, wu_ref, cwg_ref, cwu_ref, cbg_ref, cbu_ref,
                   o_ref, a_sc, *, tm, tiles_per_seq):
    i = pl.program_id(0)

    @pl.when(pl.program_id(1) == 0)
    def _():
        pos = i % tiles_per_seq
        prev = prev_ref[...]
        nxt = next_ref[...]
        a_sc[:HALO] = jnp.where(pos == 0, jnp.zeros_like(prev), prev)
        a_sc[HALO:HALO + tm] = a_ref[...]
        a_sc[HALO + tm:] = jnp.where(pos == tiles_per_seq - 1, jnp.zeros_like(nxt), nxt)

    def conv(w_ref, cw_ref, cb_ref):
        r = jnp.dot(a_sc[...], w_ref[...], preferred_element_type=F32)
        rows = r.shape[0]
        r_prev = pltpu.roll(r, 1, 0)[HALO:HALO + tm]
        r_next = pltpu.roll(r, rows - 1, 0)[HALO:HALO + tm]
        cw = cw_ref[...]
        return (r_prev * cw[0:1] + r[HALO:HALO + tm] * cw[1:2] + r_next * cw[2:3]) + cb_ref[...]

    gate = conv(wg_ref, cwg_ref, cbg_ref)
    up = conv(wu_ref, cwu_ref, cbu_ref)
    o_ref[...] = (gate * (1.0 / (1.0 + jnp.exp(-gate))) * up).astype(o_ref.dtype)


def ffn_up(h, w_up, conv_w, conv_b, seq):
    m, d = h.shape
    d_ff = w_up.shape[1] // 2
    tm = _tile("ffn_tm", seq)
    tn = _tile("ffn_tn", d_ff)
    nb = d_ff // tn
    hb = tm // HALO
    last_hb = m // HALO - 1
    kern = functools.partial(_ffn_up_kernel, tm=tm, tiles_per_seq=seq // tm)
    return pl.pallas_call(
        kern,
        grid=(m // tm, nb),
        in_specs=[pl.BlockSpec((tm, d), lambda i, j: (i, 0)),
                  pl.BlockSpec((HALO, d), lambda i, j: (jnp.maximum(i * hb - 1, 0), 0)),
                  pl.BlockSpec((HALO, d), lambda i, j: (jnp.minimum((i + 1) * hb, last_hb), 0)),
                  pl.BlockSpec((d, tn), lambda i, j: (0, j)),
                  pl.BlockSpec((d, tn), lambda i, j: (0, j + nb)),
                  pl.BlockSpec((3, tn), lambda i, j: (0, j)),
                  pl.BlockSpec((3, tn), lambda i, j: (0, j + nb)),
                  pl.BlockSpec((1, tn), lambda i, j: (0, j)),
                  pl.BlockSpec((1, tn), lambda i, j: (0, j + nb))],
        out_specs=pl.BlockSpec((tm, tn), lambda i, j: (i, j)),
        out_shape=jax.ShapeDtypeStruct((m, d_ff), BF16),
        scratch_shapes=[pltpu.VMEM((tm + 2 * HALO, d), BF16)],
        compiler_params=_params("parallel", "arbitrary"),
        name="ffn_up_conv_silu",
    )(h, h, h, w_up, w_up, conv_w, conv_w, conv_b, conv_b)


def _prep_w_in(w_in, d_model, heads):
    dq = heads * 2 * DH_DIFF
    dv = heads * V_DIFF
    qkv_cols = 2 * dq + dv
    lat_cols = w_in.shape[1] - qkv_cols - 2 * d_model
    lat_pad = -lat_cols % LANE
    col_scale = jnp.concatenate([jnp.full((dq,), DH_DIFF ** -0.5 * LOG2E, F32),
                                 jnp.ones((dq + dv,), F32)])
    w_qkv = (w_in[:, :qkv_cols] * col_scale).astype(BF16)
    w_lat = jnp.pad(w_in[:, qkv_cols:qkv_cols + lat_cols], ((0, 0), (0, lat_pad))).astype(BF16)
    w_gate = w_in[:, qkv_cols + lat_cols:].astype(BF16)
    return w_qkv, w_lat, w_gate


def _prep_w_q_up(w, heads):
    k = w.shape[0]
    w = w.reshape(k, heads, QK_NOPE + QK_ROPE)
    w = jnp.pad(w, ((0, 0), (0, 0), (0, MLA_QK_PAD - QK_NOPE - QK_ROPE)))
    return w.reshape(k, heads * MLA_QK_PAD).astype(BF16)


def _prep_w_kv_up(w, heads):
    kv = w.shape[0]
    w = w.reshape(kv, heads, QK_NOPE + V_MLA)
    top_k = jnp.pad(w[:, :, :QK_NOPE], ((0, 0), (0, 0), (0, MLA_QK_PAD - QK_NOPE)))
    eye = jnp.pad(jnp.eye(QK_ROPE, dtype=F32), ((0, LANE - QK_ROPE), (QK_NOPE, MLA_QK_PAD - QK_NOPE - QK_ROPE)))
    bot_k = jnp.broadcast_to(eye[:, None, :], (LANE, heads, MLA_QK_PAD))
    w_k = jnp.concatenate([top_k, bot_k], axis=0).reshape(kv + LANE, heads * MLA_QK_PAD)
    w_v = jnp.pad(w[:, :, QK_NOPE:].reshape(kv, heads * V_MLA), ((0, LANE), (0, 0)))
    return jnp.concatenate([w_k, w_v], axis=1).astype(BF16)


def _rope_tables(seq):
    pos = jnp.arange(seq, dtype=F32)
    inv_freq = ROPE_THETA ** (-jnp.arange(0, QK_ROPE, 2, dtype=F32) / QK_ROPE)
    ang = pos[:, None] * inv_freq[None, :]
    cos, sin = jnp.cos(ang), jnp.sin(ang)
    half = QK_ROPE // 2
    z = lambda n: jnp.zeros((seq, n), F32)
    c = jnp.concatenate([cos, cos, z(LANE - QK_ROPE)], axis=1)
    s1 = jnp.concatenate([-sin, z(LANE - half)], axis=1)
    s2 = jnp.concatenate([z(half), sin, z(LANE - QK_ROPE)], axis=1)
    return c, s1, s2


def _encoder_layer(x, batch, seq, layer_idx, rel_bias, wts):
    (rms_attn_g, w_qkv, w_lat, w_gate, lams, diff_subln_g, mla_q_norm_g, w_q_up, mla_kv_norm_g,
     w_kv_ext, w_branch_a, w_branch_b, w_out, rms_ffn_g, w_ffn_up, conv_w, conv_b, w_ffn_down) = wts
    m, d_model = x.shape
    heads = w_branch_a.shape[0] // V_DIFF
    q_lora = mla_q_norm_g.shape[0]
    kv_lora = mla_kv_norm_g.shape[0]
    assert q_lora % kv_lora == 0 and (q_lora + kv_lora) % LANE == 0
    lambda_init = 0.8 - 0.6 * math.exp(-0.3 * layer_idx)
    tm = _tile("mm_tm", seq)

    h = rmsnorm(x, rms_attn_g, BF16)
    qkv = matmul(h, w_qkv, BF16, tm, _tile("mm_tn", w_qkv.shape[1]), "in_proj_qkv")
    lat = matmul(h, w_lat, F32, _tile("lat_tm", seq), w_lat.shape[1], "in_proj_latent")
    gates = matmul(h, w_gate, F32, tm, _tile("mm_tn", d_model), "in_proj_gates")

    a_out = diff_attention(qkv, rel_bias, lams, diff_subln_g, lambda_init, batch, seq, heads)

    rope_tabs = _rope_tables(seq)
    q_mla = mla_q_up(lat, mla_q_norm_g, w_q_up, rope_tabs, seq, q_lora)
    kv_mla = mla_kv_up(lat, mla_kv_norm_g, w_kv_ext, rope_tabs, seq, q_lora, kv_lora)
    b_out = mla_attention(q_mla, kv_mla, batch, seq, heads)

    merged = gated_merge(a_out, b_out, w_branch_a, w_branch_b, gates, tm, _tile("mm_tn", d_model))
    x = matmul_residual(merged, w_out, x, tm, _tile("mm_tn", d_model), "out_proj")

    h2 = rmsnorm(x, rms_ffn_g, BF16)
    act = ffn_up(h2, w_ffn_up, conv_w, conv_b, seq)
    return matmul_residual(act, w_ffn_down, x, _tile("down_tm", seq), _tile("down_tn", d_model),
                           "ffn_down")


def kernel(x_prompt, x_sample, rel_bias, final_norm_g, rms_attn_g, w_in, lambda_q1, lambda_k1,
           lambda_q2, lambda_k2, diff_subln_g, mla_q_norm_g, w_mla_q_up, mla_kv_norm_g,
           w_mla_kv_up, w_branch_a, w_branch_b, w_out, rms_ffn_g, w_ffn_up, conv_w, conv_b,
           w_ffn_down):
    depth = w_in.shape[0]
    d_model = x_prompt.shape[-1]
    heads = w_branch_a.shape[1] // V_DIFF

    layers = []
    for l in range(depth):
        w_qkv, w_lat, w_gate = _prep_w_in(w_in[l], d_model, heads)
        layers.append((
            rms_attn_g[l], w_qkv, w_lat, w_gate,
            (lambda_q1[l], lambda_k1[l], lambda_q2[l], lambda_k2[l]), diff_subln_g[l],
            mla_q_norm_g[l], _prep_w_q_up(w_mla_q_up[l], heads),
            mla_kv_norm_g[l], _prep_w_kv_up(w_mla_kv_up[l], heads),
            w_branch_a[l].astype(BF16), w_branch_b[l].astype(BF16), w_out[l].astype(BF16),
            rms_ffn_g[l], w_ffn_up[l].astype(BF16), conv_w[l].astype(F32),
            conv_b[l].reshape(1, -1).astype(F32), w_ffn_down[l].astype(BF16)))

    def trunk(x):
        batch, seq, _ = x.shape
        y = x.reshape(batch * seq, d_model)
        for l in range(depth):
            y = _encoder_layer(y, batch, seq, l, rel_bias, layers[l])
        return rmsnorm(y, final_norm_g, x.dtype).reshape(x.shape)

    return (trunk(x_prompt), trunk(x_sample))
```

```python
import functools
import math

import jax
import jax.numpy as jnp
from jax import lax
from jax.experimental import pallas as pl
from jax.experimental.pallas import tpu as pltpu

DH_DIFF = 64
V_DIFF = 2 * DH_DIFF
QK_NOPE = 128
QK_ROPE = 64
V_MLA = 128
ROPE_THETA = 10000.0
N_BUCKETS = 32
MAX_DISTANCE = 128
EPS = 1e-6
LOG2E = math.log2(math.e)

LANE = 128
BF16_SUBLANE = 16
VMEM_LIMIT_BYTES = 56 * 1024 * 1024

MLA_QK_PAD = 2 * LANE
HALO = BF16_SUBLANE

TILES = dict(
    norm_tm=512,
    mm_tm=1024, mm_tn=512,
    lat_tm=512,
    qup_tm=1024, qup_tn=1024,
    kvup_tm=1024, kvup_tn=1024,
    ffn_tm=1024, ffn_tn=256,
    down_tm=512, down_tn=256,
    diff_tq=512, diff_tk=1024,
    mla_tq=1024, mla_tk=2048,
)

F32 = jnp.float32
BF16 = jnp.bfloat16


def _tile(name, dim):
    t = min(TILES[name], dim)
    assert dim % t == 0, (name, dim, t)
    return t


def _params(*sem):
    return pltpu.CompilerParams(dimension_semantics=sem, vmem_limit_bytes=VMEM_LIMIT_BYTES)


def _rms(x, g):
    return x * lax.rsqrt(jnp.mean(x * x, axis=-1, keepdims=True) + EPS) * g


def _rope128(p, c, s1, s2):
    return (p * c + pltpu.roll(p, LANE - QK_ROPE // 2, 1) * s1
            + pltpu.roll(p, QK_ROPE // 2, 1) * s2)


def _rmsnorm_kernel(x_ref, g_ref, o_ref):
    o_ref[...] = _rms(x_ref[...].astype(F32), g_ref[...]).astype(o_ref.dtype)


def rmsnorm(x, g, out_dtype):
    m, d = x.shape
    tm = _tile("norm_tm", m)
    return pl.pallas_call(
        _rmsnorm_kernel,
        grid=(m // tm,),
        in_specs=[pl.BlockSpec((tm, d), lambda i: (i, 0)),
                  pl.BlockSpec((1, d), lambda i: (0, 0))],
        out_specs=pl.BlockSpec((tm, d), lambda i: (i, 0)),
        out_shape=jax.ShapeDtypeStruct((m, d), out_dtype),
        compiler_params=_params("parallel"),
        name="rmsnorm",
    )(x, g.reshape(1, d).astype(F32))


def _mm_kernel(a_ref, b_ref, o_ref):
    o_ref[...] = jnp.dot(a_ref[...], b_ref[...], preferred_element_type=F32).astype(o_ref.dtype)


def matmul(a, b, out_dtype, tm, tn, name):
    m, k = a.shape
    n = b.shape[1]
    return pl.pallas_call(
        _mm_kernel,
        grid=(m // tm, n // tn),
        in_specs=[pl.BlockSpec((tm, k), lambda i, j: (i, 0)),
                  pl.BlockSpec((k, tn), lambda i, j: (0, j))],
        out_specs=pl.BlockSpec((tm, tn), lambda i, j: (i, j)),
        out_shape=jax.ShapeDtypeStruct((m, n), out_dtype),
        compiler_params=_params("parallel", "arbitrary"),
        name=name,
    )(a, b)


def _mm_res_kernel(a_ref, b_ref, r_ref, o_ref):
    o_ref[...] = r_ref[...] + jnp.dot(a_ref[...], b_ref[...], preferred_element_type=F32)


def matmul_residual(a, b, res, tm, tn, name):
    m, k = a.shape
    n = b.shape[1]
    return pl.pallas_call(
        _mm_res_kernel,
        grid=(m // tm, n // tn),
        in_specs=[pl.BlockSpec((tm, k), lambda i, j: (i, 0)),
                  pl.BlockSpec((k, tn), lambda i, j: (0, j)),
                  pl.BlockSpec((tm, tn), lambda i, j: (i, j))],
        out_specs=pl.BlockSpec((tm, tn), lambda i, j: (i, j)),
        out_shape=jax.ShapeDtypeStruct((m, n), F32),
        compiler_params=_params("parallel", "arbitrary"),
        name=name,
    )(a, b, res)


def _merge_kernel(a_ref, b_ref, wa_ref, wb_ref, ga_ref, gb_ref, o_ref):
    ya = jnp.dot(a_ref[...], wa_ref[...], preferred_element_type=F32)
    yb = jnp.dot(b_ref[...], wb_ref[...], preferred_element_type=F32)
    sa = 1.0 / (1.0 + jnp.exp(-ga_ref[...]))
    sb = 1.0 / (1.0 + jnp.exp(-gb_ref[...]))
    o_ref[...] = (sa * ya + sb * yb).astype(o_ref.dtype)


def gated_merge(a, b, wa, wb, gates, tm, tn):
    m, ka = a.shape
    kb = b.shape[1]
    n = wa.shape[1]
    nb = n // tn
    return pl.pallas_call(
        _merge_kernel,
        grid=(m // tm, nb),
        in_specs=[pl.BlockSpec((tm, ka), lambda i, j: (i, 0)),
                  pl.BlockSpec((tm, kb), lambda i, j: (i, 0)),
                  pl.BlockSpec((ka, tn), lambda i, j: (0, j)),
                  pl.BlockSpec((kb, tn), lambda i, j: (0, j)),
                  pl.BlockSpec((tm, tn), lambda i, j: (i, j)),
                  pl.BlockSpec((tm, tn), lambda i, j: (i, j + nb))],
        out_specs=pl.BlockSpec((tm, tn), lambda i, j: (i, j)),
        out_shape=jax.ShapeDtypeStruct((m, n), BF16),
        compiler_params=_params("parallel", "arbitrary"),
        name="gated_merge",
    )(a, b, wa, wb, gates, gates)


def _mla_q_kernel(lat_ref, g_ref, w_ref, c_ref, s1_ref, s2_ref, o_ref, a_sc, *, heads, scale):
    @pl.when(pl.program_id(1) == 0)
    def _():
        a_sc[...] = _rms(lat_ref[...], g_ref[...]).astype(BF16)

    acc = jnp.dot(a_sc[...], w_ref[...], preferred_element_type=F32)
    c, s1, s2 = c_ref[...], s1_ref[...], s2_ref[...]
    for h in range(heads):
        lo = h * MLA_QK_PAD
        o_ref[:, lo:lo + QK_NOPE] = (acc[:, lo:lo + QK_NOPE] * scale).astype(o_ref.dtype)
        rot = _rope128(acc[:, lo + QK_NOPE:lo + MLA_QK_PAD], c, s1, s2)
        o_ref[:, lo + QK_NOPE:lo + MLA_QK_PAD] = (rot * scale).astype(o_ref.dtype)


def mla_q_up(lat, g, w, rope_tabs, seq, q_lora):
    m = lat.shape[0]
    n = w.shape[1]
    tm = _tile("qup_tm", seq)
    tn = _tile("qup_tn", n)
    pos_blocks = seq // tm
    tab_spec = pl.BlockSpec((tm, LANE), lambda i, j: (i % pos_blocks, 0))
    kern = functools.partial(_mla_q_kernel, heads=tn // MLA_QK_PAD,
                             scale=(QK_NOPE + QK_ROPE) ** -0.5 * LOG2E)
    return pl.pallas_call(
        kern,
        grid=(m // tm, n // tn),
        in_specs=[pl.BlockSpec((tm, q_lora), lambda i, j: (i, 0)),
                  pl.BlockSpec((1, q_lora), lambda i, j: (0, 0)),
                  pl.BlockSpec((q_lora, tn), lambda i, j: (0, j)),
                  tab_spec, tab_spec, tab_spec],
        out_specs=pl.BlockSpec((tm, tn), lambda i, j: (i, j)),
        out_shape=jax.ShapeDtypeStruct((m, n), BF16),
        scratch_shapes=[pltpu.VMEM((tm, q_lora), BF16)],
        compiler_params=_params("parallel", "arbitrary"),
        name="mla_q_up",
    )(lat, g.reshape(1, q_lora).astype(F32), w, *rope_tabs)


def _mla_kv_kernel(kv_ref, pe_ref, g_ref, w_ref, c_ref, s1_ref, s2_ref, o_ref, a_sc, *, kv_lora):
    @pl.when(pl.program_id(1) == 0)
    def _():
        a_sc[:, :kv_lora] = _rms(kv_ref[...], g_ref[...]).astype(BF16)
        rot = _rope128(pe_ref[...], c_ref[...], s1_ref[...], s2_ref[...])
        a_sc[:, kv_lora:] = rot.astype(BF16)

    o_ref[...] = jnp.dot(a_sc[...], w_ref[...], preferred_element_type=F32).astype(o_ref.dtype)


def mla_kv_up(lat, g, w_ext, rope_tabs, seq, q_lora, kv_lora):
    m = lat.shape[0]
    n = w_ext.shape[1]
    tm = _tile("kvup_tm", seq)
    tn = _tile("kvup_tn", n)
    pos_blocks = seq // tm
    tab_spec = pl.BlockSpec((tm, LANE), lambda i, j: (i % pos_blocks, 0))
    kv_blk = q_lora // kv_lora
    pe_blk = (q_lora + kv_lora) // LANE
    return pl.pallas_call(
        functools.partial(_mla_kv_kernel, kv_lora=kv_lora),
        grid=(m // tm, n // tn),
        in_specs=[pl.BlockSpec((tm, kv_lora), lambda i, j: (i, kv_blk)),
                  pl.BlockSpec((tm, LANE), lambda i, j: (i, pe_blk)),
                  pl.BlockSpec((1, kv_lora), lambda i, j: (0, 0)),
                  pl.BlockSpec((kv_lora + LANE, tn), lambda i, j: (0, j)),
                  tab_spec, tab_spec, tab_spec],
        out_specs=pl.BlockSpec((tm, tn), lambda i, j: (i, j)),
        out_shape=jax.ShapeDtypeStruct((m, n), BF16),
        scratch_shapes=[pltpu.VMEM((tm, kv_lora + LANE), BF16)],
        compiler_params=_params("parallel", "arbitrary"),
        name="mla_kv_up",
    )(lat, lat, g.reshape(1, kv_lora).astype(F32), w_ext, *rope_tabs)


def _lane_tile(x, n):
    return jnp.concatenate([x] * n, axis=1)


def _softmax_init(m_sc, acc_sc):
    m_sc[...] = jnp.full(m_sc.shape, -jnp.inf, F32)
    acc_sc[...] = jnp.zeros(acc_sc.shape, F32)


def _softmax_update(s, vx, m_sc, acc_sc, c=None):
    tk = s.shape[1]
    m_prev = m_sc[...]
    m_cur = jnp.max(s, axis=1, keepdims=True)
    if c is not None:
        m_cur = m_cur + c
    m_new = jnp.maximum(m_prev, m_cur)
    alpha = jnp.exp2(m_prev - m_new)
    shift = m_new if c is None else m_new - c
    p = jnp.exp2(s - _lane_tile(shift, tk // LANE))
    acc_sc[...] = (_lane_tile(alpha, 2) * acc_sc[...]
                   + jnp.dot(p.astype(BF16), vx, preferred_element_type=F32))
    m_sc[...] = m_new


def _extend_v(v_ref, vx_sc, tk):
    width = v_ref.shape[1]

    def body(j, carry):
        r = pl.ds(pl.multiple_of(j * tk, tk), tk)
        vx_sc[r, :width] = v_ref[r, :]
        vx_sc[r, width:] = jnp.ones((tk, vx_sc.shape[1] - width), vx_sc.dtype)
        return carry

    lax.fori_loop(0, v_ref.shape[0] // tk, body, 0)


def _chunk_rows(j, tk):
    start = j * tk
    return pl.ds(start if isinstance(j, int) else pl.multiple_of(start, tk), tk)


def _qk(q, k):
    return lax.dot_general(q, k, (((1,), (1,)), ((), ())), preferred_element_type=F32)


def _chunk_loop(lo, hi, scores, s_bufs, consume):
    def by_parity(j, fn):
        for parity in (0, 1):
            @pl.when((j & 1) == parity)
            def _():
                fn(j, s_bufs[parity], s_bufs[1 - parity])

    def one(j, cur, nxt):
        nxt[...] = scores(j + 1)
        consume(j, cur)

    def two(j, cur, nxt):
        nxt[...] = scores(j + 1)
        consume(j, cur)
        cur[...] = scores(j + 2)
        consume(j + 1, nxt)

    pairs = lax.shift_right_logical(jnp.maximum(hi - lo, 0), 1)

    def body(t, carry):
        by_parity(lo + 2 * t, two)
        return carry

    lax.fori_loop(0, pairs, body, 0)
    rest = lo + 2 * pairs

    @pl.when(rest < hi)
    def _():
        by_parity(rest, one)


def _first_chunk(s_bufs, scores):
    first = functools.reduce(jnp.logical_and, [pl.program_id(a) == 0 for a in range(3)])

    @pl.when(first)
    def _():
        s_bufs[0][...] = scores(0)


def _last_chunk(n_chunks, next_scores, s_bufs, consume):
    s_bufs[0][...] = next_scores()
    consume(n_chunks - 1, s_bufs[1])


def _next_step(b, h, i, batch, heads, nq):
    n = jnp.minimum((b * heads + h) * nq + i + 1, batch * heads * nq - 1)
    bh = n // nq
    return bh // heads, bh % heads, n % nq


def _mla_attn_kernel(q_ref, k_ref, v_ref, qn_ref, kn_ref, o_ref, vx_sc, sa_sc, sb_sc, m_sc, acc_sc,
                     *, tk):
    @pl.when(pl.program_id(2) == 0)
    def _():
        _extend_v(v_ref, vx_sc, tk)

    _softmax_init(m_sc, acc_sc)
    rows = lambda j: _chunk_rows(j, tk)
    scores = lambda j: _qk(q_ref[...], k_ref[rows(j), :])

    def consume(j, s_ref):
        _softmax_update(s_ref[...], vx_sc[rows(j), :], m_sc, acc_sc)

    nk = k_ref.shape[0] // tk
    s_bufs = (sa_sc, sb_sc)
    _first_chunk(s_bufs, scores)
    _chunk_loop(0, nk - 1, scores, s_bufs, consume)
    _last_chunk(nk, lambda: _qk(qn_ref[...], kn_ref[...]), s_bufs, consume)
    acc = acc_sc[...]
    o_ref[...] = (acc[:, :V_MLA] / acc[:, V_MLA:]).astype(o_ref.dtype)


def mla_attention(q, kv, batch, seq, heads):
    tq = _tile("mla_tq", seq)
    tk = _tile("mla_tk", seq)
    nq, nk = seq // tq, seq // tk
    assert nk % 2 == 0
    v_blk0 = heads * MLA_QK_PAD // V_MLA
    nxt = lambda b, h, i: _next_step(b, h, i, batch, heads, nq)

    def q_next(b, h, i):
        b2, h2, i2 = nxt(b, h, i)
        return b2 * nq + i2, h2

    def k_next(b, h, i):
        b2, h2, _ = nxt(b, h, i)
        return b2 * nk, h2

    return pl.pallas_call(
        functools.partial(_mla_attn_kernel, tk=tk),
        grid=(batch, heads, nq),
        in_specs=[pl.BlockSpec((tq, MLA_QK_PAD), lambda b, h, i: (b * nq + i, h)),
                  pl.BlockSpec((seq, MLA_QK_PAD), lambda b, h, i: (b, h)),
                  pl.BlockSpec((seq, V_MLA), lambda b, h, i: (b, v_blk0 + h)),
                  pl.BlockSpec((tq, MLA_QK_PAD), q_next),
                  pl.BlockSpec((tk, MLA_QK_PAD), k_next)],
        out_specs=pl.BlockSpec((tq, V_MLA), lambda b, h, i: (b * nq + i, h)),
        out_shape=jax.ShapeDtypeStruct((batch * seq, heads * V_MLA), BF16),
        scratch_shapes=[pltpu.VMEM((seq, 2 * V_MLA), BF16),
                        pltpu.VMEM((tq, tk), F32), pltpu.VMEM((tq, tk), F32),
                        pltpu.VMEM((tq, LANE), F32), pltpu.VMEM((tq, 2 * V_MLA), F32)],
        compiler_params=_params("arbitrary", "arbitrary", "arbitrary"),
        name="mla_attention",
    )(q, kv, kv, q, kv)


def _first_biased_chunk(i, tq, tk, nk, n_bias):
    return jnp.minimum(jnp.maximum(i * tq - (MAX_DISTANCE - 1), 0) // tk, nk - n_bias)


def _diff_attn_kernel(tab_ref, far_ref, ids_ref, q_ref, k_ref, v_ref, qn_ref, kn_ref, lq1_ref, lk1_ref, lq2_ref,
                      lk2_ref, g_ref, o_ref, q2_sc, vx_sc, bias_sc, sa_sc, sb_sc, m_sc, acc_sc, *, tq, tk,
                      step, off0, n_bias, lambda_init):
    h, i = pl.program_id(1), pl.program_id(2)
    seq = k_ref.shape[0]
    nk = seq // tk

    @pl.when(i == 0)
    def _():
        _extend_v(v_ref, vx_sc, tk)
        for t in range(bias_sc.shape[0]):
            ids = ids_ref[t]
            f = jnp.zeros(ids.shape, F32)
            for b in range(N_BUCKETS):
                f = jnp.where(ids == b, tab_ref[h, b], f)
            f = f * LOG2E
            bias_sc[t] = pltpu.roll(jnp.broadcast_to(f, (tq, tq + tk)), tk + 1, 1,
                                    stride=1, stride_axis=0)[:, :tk]

    def stacked(q):
        lane = lax.broadcasted_iota(jnp.int32, q.shape, 1)
        zero = jnp.zeros_like(q)
        return jnp.concatenate([jnp.where(lane < DH_DIFF, q, zero),
                                jnp.where(lane >= DH_DIFF, q, zero)], axis=0)

    q2_sc[...] = stacked(q_ref[...])
    _softmax_init(m_sc, acc_sc)

    rows = lambda j: _chunk_rows(j, tk)
    scores = lambda j: _qk(q2_sc[...], k_ref[rows(j), :])
    c_left = tab_ref[h, far_ref[0]] * LOG2E
    c_right = tab_ref[h, far_ref[1]] * LOG2E

    j0 = _first_biased_chunk(i, tq, tk, nk, n_bias)

    def chunk_at(d):
        j = j0 + d
        return j if d < n_bias else jnp.where(j >= nk, j - nk, j)

    def consume(d, s_ref):
        j = chunk_at(d)
        vx = vx_sc[rows(j), :]
        if d < n_bias:
            bias = bias_sc[(j * tk - i * tq - off0) // step]
            s = s_ref[...] + jnp.concatenate([bias, bias], axis=0)
            _softmax_update(s, vx, m_sc, acc_sc)
        else:
            _softmax_update(s_ref[...], vx, m_sc, acc_sc, jnp.where(j < j0, c_left, c_right))

    s_bufs = (sa_sc, sb_sc)
    _first_chunk(s_bufs, lambda _: scores(j0))
    for d in range(nk - 1):
        s_bufs[(d + 1) % 2][...] = scores(chunk_at(d + 1))
        consume(d, s_bufs[d % 2])
    _last_chunk(nk, lambda: _qk(stacked(qn_ref[...]), kn_ref[...]), s_bufs, consume)

    acc = acc_sc[...]
    o = acc[:, :V_DIFF] / acc[:, V_DIFF:]
    lam = (jnp.exp(jnp.sum(lq1_ref[...] * lk1_ref[...], axis=1, keepdims=True))
           - jnp.exp(jnp.sum(lq2_ref[...] * lk2_ref[...], axis=1, keepdims=True)) + lambda_init)
    a = o[:tq] - lam * o[tq:]
    o_ref[...] = (_rms(a, g_ref[...]) * (1.0 - lambda_init)).astype(o_ref.dtype)


def _t5_bucket(rel):
    nb = N_BUCKETS // 2
    max_exact = nb // 2
    ret = (rel > 0).astype(jnp.int32) * nb
    n = jnp.abs(rel)
    nf = jnp.maximum(n, max_exact).astype(F32)
    large = max_exact + (jnp.log(nf / max_exact) / math.log(MAX_DISTANCE / max_exact)
                         * (nb - max_exact)).astype(jnp.int32)
    large = jnp.minimum(large, nb - 1)
    return ret + jnp.where(n < max_exact, n, large)


def diff_attention(qkv, rel_bias, lams, subln_g, lambda_init, batch, seq, heads):
    tq = _tile("diff_tq", seq)
    tk = _tile("diff_tk", seq)
    nq = seq // tq
    step = math.gcd(tq, tk)
    nk = seq // tk
    near_lo, near_hi = -(tk - 1 + MAX_DISTANCE), tq - 1 + MAX_DISTANCE
    n_bias = -(-(near_hi - near_lo - tk) // tk) + 1
    assert nk % 2 == 0 and nk >= n_bias
    off0 = min((near_lo // step + 1) * step, tq - n_bias * tk)
    offs = list(range(off0, max(near_hi - 1, (n_bias - 1) * tk) + 1, step))
    win = (jnp.array(offs, jnp.int32)[:, None] - (tq - 1)
           + jnp.arange(tq + tk, dtype=jnp.int32)[None, :])
    ids = _t5_bucket(win).reshape(len(offs), 1, tq + tk)
    far_ids = _t5_bucket(jnp.array([-MAX_DISTANCE, MAX_DISTANCE], jnp.int32))
    vec = lambda: pl.BlockSpec((1, DH_DIFF), lambda b, h, i: (0, 0))
    kern = functools.partial(_diff_attn_kernel, tq=tq, tk=tk, step=step, off0=off0, n_bias=n_bias,
                             lambda_init=lambda_init)
    nxt = lambda b, h, i: _next_step(b, h, i, batch, heads, nq)

    def q_next(b, h, i):
        b2, h2, i2 = nxt(b, h, i)
        return b2 * nq + i2, h2

    def k_next(b, h, i):
        b2, h2, i2 = nxt(b, h, i)
        return b2 * nk + _first_biased_chunk(i2, tq, tk, nk, n_bias), heads + h2

    return pl.pallas_call(
        kern,
        grid=(batch, heads, nq),
        in_specs=[pl.BlockSpec(memory_space=pltpu.SMEM),
                  pl.BlockSpec(memory_space=pltpu.SMEM),
                  pl.BlockSpec(ids.shape, lambda b, h, i: (0, 0, 0)),
                  pl.BlockSpec((tq, V_DIFF), lambda b, h, i: (b * nq + i, h)),
                  pl.BlockSpec((seq, V_DIFF), lambda b, h, i: (b, heads + h)),
                  pl.BlockSpec((seq, V_DIFF), lambda b, h, i: (b, 2 * heads + h)),
                  pl.BlockSpec((tq, V_DIFF), q_next),
                  pl.BlockSpec((tk, V_DIFF), k_next),
                  vec(), vec(), vec(), vec(),
                  pl.BlockSpec((1, V_DIFF), lambda b, h, i: (0, 0))],
        out_specs=pl.BlockSpec((tq, V_DIFF), lambda b, h, i: (b * nq + i, h)),
        out_shape=jax.ShapeDtypeStruct((batch * seq, heads * V_DIFF), BF16),
        scratch_shapes=[pltpu.VMEM((2 * tq, V_DIFF), BF16), pltpu.VMEM((seq, 2 * V_DIFF), BF16),
                        pltpu.VMEM((len(offs), tq, tk), F32),
                        pltpu.VMEM((2 * tq, tk), F32), pltpu.VMEM((2 * tq, tk), F32),
                        pltpu.VMEM((2 * tq, LANE), F32), pltpu.VMEM((2 * tq, 2 * V_DIFF), F32)],
        compiler_params=_params("arbitrary", "arbitrary", "arbitrary"),
        name="diff_attention",
    )(rel_bias.T.astype(F32), far_ids, ids, qkv, qkv, qkv, qkv, qkv,
      *[v.reshape(1, DH_DIFF).astype(F32) for v in lams], subln_g.reshape(1, V_DIFF).astype(F32))


def _ffn_up_kernel(a_ref, prev_ref, next_ref, wg_ref, wu_ref, cwg_ref, cwu_ref, cbg_ref, cbu_ref,
                   o_ref, a_sc, *, tm, tiles_per_seq):
    i = pl.program_id(0)

    @pl.when(pl.program_id(1) == 0)
    def _():
        pos = i % tiles_per_seq
        prev = prev_ref[...]
        nxt = next_ref[...]
        a_sc[:HALO] = jnp.where(pos == 0, jnp.zeros_like(prev), prev)
        a_sc[HALO:HALO + tm] = a_ref[...]
        a_sc[HALO + tm:] = jnp.where(pos == tiles_per_seq - 1, jnp.zeros_like(nxt), nxt)

    def conv(w_ref, cw_ref, cb_ref):
        r = jnp.dot(a_sc[...], w_ref[...], preferred_element_type=F32)
        rows = r.shape[0]
        r_prev = pltpu.roll(r, 1, 0)[HALO:HALO + tm]
        r_next = pltpu.roll(r, rows - 1, 0)[HALO:HALO + tm]
        cw = cw_ref[...]
        return (r_prev * cw[0:1] + r[HALO:HALO + tm] * cw[1:2] + r_next * cw[2:3]) + cb_ref[...]

    gate = conv(wg_ref, cwg_ref, cbg_ref)
    up = conv(wu_ref, cwu_ref, cbu_ref)
    o_ref[...] = (gate * (1.0 / (1.0 + jnp.exp(-gate))) * up).astype(o_ref.dtype)


def ffn_up(h, w_up, conv_w, conv_b, seq):
    m, d = h.shape
    d_ff = w_up.shape[1] // 2
    tm = _tile("ffn_tm", seq)
    tn = _tile("ffn_tn", d_ff)
    nb = d_ff // tn
    hb = tm // HALO
    last_hb = m // HALO - 1
    kern = functools.partial(_ffn_up_kernel, tm=tm, tiles_per_seq=seq // tm)
    return pl.pallas_call(
        kern,
        grid=(m // tm, nb),
        in_specs=[pl.BlockSpec((tm, d), lambda i, j: (i, 0)),
                  pl.BlockSpec((HALO, d), lambda i, j: (jnp.maximum(i * hb - 1, 0), 0)),
                  pl.BlockSpec((HALO, d), lambda i, j: (jnp.minimum((i + 1) * hb, last_hb), 0)),
                  pl.BlockSpec((d, tn), lambda i, j: (0, j)),
                  pl.BlockSpec((d, tn), lambda i, j: (0, j + nb)),
                  pl.BlockSpec((3, tn), lambda i, j: (0, j)),
                  pl.BlockSpec((3, tn), lambda i, j: (0, j + nb)),
                  pl.BlockSpec((1, tn), lambda i, j: (0, j)),
                  pl.BlockSpec((1, tn), lambda i, j: (0, j + nb))],
        out_specs=pl.BlockSpec((tm, tn), lambda i, j: (i, j)),
        out_shape=jax.ShapeDtypeStruct((m, d_ff), BF16),
        scratch_shapes=[pltpu.VMEM((tm + 2 * HALO, d), BF16)],
        compiler_params=_params("parallel", "arbitrary"),
        name="ffn_up_conv_silu",
    )(h, h, h, w_up, w_up, conv_w, conv_w, conv_b, conv_b)


def _prep_w_in(w_in, d_model, heads):
    dq = heads * 2 * DH_DIFF
    dv = heads * V_DIFF
    qkv_cols = 2 * dq + dv
    lat_cols = w_in.shape[1] - qkv_cols - 2 * d_model
    lat_pad = -lat_cols % LANE
    col_scale = jnp.concatenate([jnp.full((dq,), DH_DIFF ** -0.5 * LOG2E, F32),
                                 jnp.ones((dq + dv,), F32)])
    w_qkv = (w_in[:, :qkv_cols] * col_scale).astype(BF16)
    w_lat = jnp.pad(w_in[:, qkv_cols:qkv_cols + lat_cols], ((0, 0), (0, lat_pad))).astype(BF16)
    w_gate = w_in[:, qkv_cols + lat_cols:].astype(BF16)
    return w_qkv, w_lat, w_gate


def _prep_w_q_up(w, heads):
    k = w.shape[0]
    w = w.reshape(k, heads, QK_NOPE + QK_ROPE)
    w = jnp.pad(w, ((0, 0), (0, 0), (0, MLA_QK_PAD - QK_NOPE - QK_ROPE)))
    return w.reshape(k, heads * MLA_QK_PAD).astype(BF16)


def _prep_w_kv_up(w, heads):
    kv = w.shape[0]
    w = w.reshape(kv, heads, QK_NOPE + V_MLA)
    top_k = jnp.pad(w[:, :, :QK_NOPE], ((0, 0), (0, 0), (0, MLA_QK_PAD - QK_NOPE)))
    eye = jnp.pad(jnp.eye(QK_ROPE, dtype=F32), ((0, LANE - QK_ROPE), (QK_NOPE, MLA_QK_PAD - QK_NOPE - QK_ROPE)))
    bot_k = jnp.broadcast_to(eye[:, None, :], (LANE, heads, MLA_QK_PAD))
    w_k = jnp.concatenate([top_k, bot_k], axis=0).reshape(kv + LANE, heads * MLA_QK_PAD)
    w_v = jnp.pad(w[:, :, QK_NOPE:].reshape(kv, heads * V_MLA), ((0, LANE), (0, 0)))
    return jnp.concatenate([w_k, w_v], axis=1).astype(BF16)


def _rope_tables(seq):
    pos = jnp.arange(seq, dtype=F32)
    inv_freq = ROPE_THETA ** (-jnp.arange(0, QK_ROPE, 2, dtype=F32) / QK_ROPE)
    ang = pos[:, None] * inv_freq[None, :]
    cos, sin = jnp.cos(ang), jnp.sin(ang)
    half = QK_ROPE // 2
    z = lambda n: jnp.zeros((seq, n), F32)
    c = jnp.concatenate([cos, cos, z(LANE - QK_ROPE)], axis=1)
    s1 = jnp.concatenate([-sin, z(LANE - half)], axis=1)
    s2 = jnp.concatenate([z(half), sin, z(LANE - QK_ROPE)], axis=1)
    return c, s1, s2


def _encoder_layer(x, batch, seq, layer_idx, rel_bias, wts):
    (rms_attn_g, w_qkv, w_lat, w_gate, lams, diff_subln_g, mla_q_norm_g, w_q_up, mla_kv_norm_g,
     w_kv_ext, w_branch_a, w_branch_b, w_out, rms_ffn_g, w_ffn_up, conv_w, conv_b, w_ffn_down) = wts
    m, d_model = x.shape
    heads = w_branch_a.shape[0] // V_DIFF
    q_lora = mla_q_norm_g.shape[0]
    kv_lora = mla_kv_norm_g.shape[0]
    assert q_lora % kv_lora == 0 and (q_lora + kv_lora) % LANE == 0
    lambda_init = 0.8 - 0.6 * math.exp(-0.3 * layer_idx)
    tm = _tile("mm_tm", seq)

    h = rmsnorm(x, rms_attn_g, BF16)
    qkv = matmul(h, w_qkv, BF16, tm, _tile("mm_tn", w_qkv.shape[1]), "in_proj_qkv")
    lat = matmul(h, w_lat, F32, _tile("lat_tm", seq), w_lat.shape[1], "in_proj_latent")
    gates = matmul(h, w_gate, F32, tm, _tile("mm_tn", d_model), "in_proj_gates")

    a_out = diff_attention(qkv, rel_bias, lams, diff_subln_g, lambda_init, batch, seq, heads)

    rope_tabs = _rope_tables(seq)
    q_mla = mla_q_up(lat, mla_q_norm_g, w_q_up, rope_tabs, seq, q_lora)
    kv_mla = mla_kv_up(lat, mla_kv_norm_g, w_kv_ext, rope_tabs, seq, q_lora, kv_lora)
    b_out = mla_attention(q_mla, kv_mla, batch, seq, heads)

    merged = gated_merge(a_out, b_out, w_branch_a, w_branch_b, gates, tm, _tile("mm_tn", d_model))
    x = matmul_residual(merged, w_out, x, tm, _tile("mm_tn", d_model), "out_proj")

    h2 = rmsnorm(x, rms_ffn_g, BF16)
    act = ffn_up(h2, w_ffn_up, conv_w, conv_b, seq)
    return matmul_residual(act, w_ffn_down, x, _tile("down_tm", seq), _tile("down_tn", d_model),
                           "ffn_down")


def kernel(x_prompt, x_sample, rel_bias, final_norm_g, rms_attn_g, w_in, lambda_q1, lambda_k1,
           lambda_q2, lambda_k2, diff_subln_g, mla_q_norm_g, w_mla_q_up, mla_kv_norm_g,
           w_mla_kv_up, w_branch_a, w_branch_b, w_out, rms_ffn_g, w_ffn_up, conv_w, conv_b,
           w_ffn_down):
    depth = w_in.shape[0]
    d_model = x_prompt.shape[-1]
    heads = w_branch_a.shape[1] // V_DIFF

    layers = []
    for l in range(depth):
        w_qkv, w_lat, w_gate = _prep_w_in(w_in[l], d_model, heads)
        layers.append((
            rms_attn_g[l], w_qkv, w_lat, w_gate,
            (lambda_q1[l], lambda_k1[l], lambda_q2[l], lambda_k2[l]), diff_subln_g[l],
            mla_q_norm_g[l], _prep_w_q_up(w_mla_q_up[l], heads),
            mla_kv_norm_g[l], _prep_w_kv_up(w_mla_kv_up[l], heads),
            w_branch_a[l].astype(BF16), w_branch_b[l].astype(BF16), w_out[l].astype(BF16),
            rms_ffn_g[l], w_ffn_up[l].astype(BF16), conv_w[l].astype(F32),
            conv_b[l].reshape(1, -1).astype(F32), w_ffn_down[l].astype(BF16)))

    def trunk(x):
        batch, seq, _ = x.shape
        y = x.reshape(batch * seq, d_model)
        for l in range(depth):
            y = _encoder_layer(y, batch, seq, l, rel_bias, layers[l])
        return rmsnorm(y, final_norm_g, x.dtype).reshape(x.shape)

    return (trunk(x_prompt), trunk(x_sample))
```

```python
import functools
import math

import jax
import jax.numpy as jnp
from jax import lax
from jax.experimental import pallas as pl
from jax.experimental.pallas import tpu as pltpu

DH_DIFF = 64
V_DIFF = 2 * DH_DIFF
QK_NOPE = 128
QK_ROPE = 64
V_MLA = 128
ROPE_THETA = 10000.0
N_BUCKETS = 32
MAX_DISTANCE = 128
EPS = 1e-6
LOG2E = math.log2(math.e)

LANE = 128
BF16_SUBLANE = 16
VMEM_LIMIT_BYTES = 56 * 1024 * 1024

MLA_QK_PAD = 2 * LANE
HALO = BF16_SUBLANE

TILES = dict(
    norm_tm=512,
    mm_tm=1024, mm_tn=512, in_tn=1024,
    lat_tm=512,
    qup_tm=1024, qup_tn=1024,
    kvup_tm=1024, kvup_tn=1024,
    ffn_tm=1024, ffn_tn=256,
    down_tm=512, down_tn=512,
    diff_tq=512, diff_tk=1024,
    mla_tq=1024, mla_tk=2048,
)

F32 = jnp.float32
BF16 = jnp.bfloat16


def _tile(name, dim):
    t = min(TILES[name], dim)
    assert dim % t == 0, (name, dim, t)
    return t


def _params(*sem):
    return pltpu.CompilerParams(dimension_semantics=sem, vmem_limit_bytes=VMEM_LIMIT_BYTES)


def _rms(x, g):
    return x * lax.rsqrt(jnp.mean(x * x, axis=-1, keepdims=True) + EPS) * g


def _rope128(p, c, s1, s2):
    return (p * c + pltpu.roll(p, LANE - QK_ROPE // 2, 1) * s1
            + pltpu.roll(p, QK_ROPE // 2, 1) * s2)


def _rmsnorm_kernel(x_ref, g_ref, o_ref):
    o_ref[...] = _rms(x_ref[...].astype(F32), g_ref[...]).astype(o_ref.dtype)


def rmsnorm(x, g, out_dtype):
    m, d = x.shape
    tm = _tile("norm_tm", m)
    return pl.pallas_call(
        _rmsnorm_kernel,
        grid=(m // tm,),
        in_specs=[pl.BlockSpec((tm, d), lambda i: (i, 0)),
                  pl.BlockSpec((1, d), lambda i: (0, 0))],
        out_specs=pl.BlockSpec((tm, d), lambda i: (i, 0)),
        out_shape=jax.ShapeDtypeStruct((m, d), out_dtype),
        compiler_params=_params("parallel"),
        name="rmsnorm",
    )(x, g.reshape(1, d).astype(F32))


def _mm_nt_kernel(a_ref, bt_ref, o_ref):
    o_ref[...] = _qk(a_ref[...], bt_ref[...]).astype(o_ref.dtype)


def matmul_nt(a, bt, out_dtype, tm, tn, name):
    m, k = a.shape
    n = bt.shape[0]
    return pl.pallas_call(
        _mm_nt_kernel,
        grid=(m // tm, n // tn),
        in_specs=[pl.BlockSpec((tm, k), lambda i, j: (i, 0)),
                  pl.BlockSpec((tn, k), lambda i, j: (j, 0))],
        out_specs=pl.BlockSpec((tm, tn), lambda i, j: (i, j)),
        out_shape=jax.ShapeDtypeStruct((m, n), out_dtype),
        compiler_params=_params("parallel", "arbitrary"),
        name=name,
    )(a, bt)


def _mm_res_kernel(a_ref, b_ref, r_ref, o_ref):
    o_ref[...] = r_ref[...] + jnp.dot(a_ref[...], b_ref[...], preferred_element_type=F32)


def matmul_residual(a, b, res, tm, tn, name):
    m, k = a.shape
    n = b.shape[1]
    return pl.pallas_call(
        _mm_res_kernel,
        grid=(m // tm, n // tn),
        in_specs=[pl.BlockSpec((tm, k), lambda i, j: (i, 0)),
                  pl.BlockSpec((k, tn), lambda i, j: (0, j)),
                  pl.BlockSpec((tm, tn), lambda i, j: (i, j))],
        out_specs=pl.BlockSpec((tm, tn), lambda i, j: (i, j)),
        out_shape=jax.ShapeDtypeStruct((m, n), F32),
        compiler_params=_params("parallel", "arbitrary"),
        name=name,
    )(a, b, res)


def _merge_kernel(a_ref, b_ref, wa_ref, wb_ref, ga_ref, gb_ref, o_ref):
    ya = jnp.dot(a_ref[...], wa_ref[...], preferred_element_type=F32)
    yb = jnp.dot(b_ref[...], wb_ref[...], preferred_element_type=F32)
    sa = 1.0 / (1.0 + jnp.exp(-ga_ref[...]))
    sb = 1.0 / (1.0 + jnp.exp(-gb_ref[...]))
    o_ref[...] = (sa * ya + sb * yb).astype(o_ref.dtype)


def gated_merge(a, b, wa, wb, gates, tm, tn):
    m, ka = a.shape
    kb = b.shape[1]
    n = wa.shape[1]
    nb = n // tn
    return pl.pallas_call(
        _merge_kernel,
        grid=(m // tm, nb),
        in_specs=[pl.BlockSpec((tm, ka), lambda i, j: (i, 0)),
                  pl.BlockSpec((tm, kb), lambda i, j: (i, 0)),
                  pl.BlockSpec((ka, tn), lambda i, j: (0, j)),
                  pl.BlockSpec((kb, tn), lambda i, j: (0, j)),
                  pl.BlockSpec((tm, tn), lambda i, j: (i, j)),
                  pl.BlockSpec((tm, tn), lambda i, j: (i, j + nb))],
        out_specs=pl.BlockSpec((tm, tn), lambda i, j: (i, j)),
        out_shape=jax.ShapeDtypeStruct((m, n), BF16),
        compiler_params=_params("parallel", "arbitrary"),
        name="gated_merge",
    )(a, b, wa, wb, gates, gates)


def _mla_q_kernel(lat_ref, g_ref, w_ref, c_ref, s1_ref, s2_ref, o_ref, a_sc, *, heads, scale):
    @pl.when(pl.program_id(1) == 0)
    def _():
        a_sc[...] = _rms(lat_ref[...], g_ref[...]).astype(BF16)

    acc = jnp.dot(a_sc[...], w_ref[...], preferred_element_type=F32)
    c, s1, s2 = c_ref[...], s1_ref[...], s2_ref[...]
    for h in range(heads):
        lo = h * MLA_QK_PAD
        o_ref[:, lo:lo + QK_NOPE] = (acc[:, lo:lo + QK_NOPE] * scale).astype(o_ref.dtype)
        rot = _rope128(acc[:, lo + QK_NOPE:lo + MLA_QK_PAD], c, s1, s2)
        o_ref[:, lo + QK_NOPE:lo + MLA_QK_PAD] = (rot * scale).astype(o_ref.dtype)


def mla_q_up(lat, g, w, rope_tabs, seq, q_lora):
    m = lat.shape[0]
    n = w.shape[1]
    tm = _tile("qup_tm", seq)
    tn = _tile("qup_tn", n)
    pos_blocks = seq // tm
    tab_spec = pl.BlockSpec((tm, LANE), lambda i, j: (i % pos_blocks, 0))
    kern = functools.partial(_mla_q_kernel, heads=tn // MLA_QK_PAD,
                             scale=(QK_NOPE + QK_ROPE) ** -0.5 * LOG2E)
    return pl.pallas_call(
        kern,
        grid=(m // tm, n // tn),
        in_specs=[pl.BlockSpec((tm, q_lora), lambda i, j: (i, 0)),
                  pl.BlockSpec((1, q_lora), lambda i, j: (0, 0)),
                  pl.BlockSpec((q_lora, tn), lambda i, j: (0, j)),
                  tab_spec, tab_spec, tab_spec],
        out_specs=pl.BlockSpec((tm, tn), lambda i, j: (i, j)),
        out_shape=jax.ShapeDtypeStruct((m, n), BF16),
        scratch_shapes=[pltpu.VMEM((tm, q_lora), BF16)],
        compiler_params=_params("parallel", "arbitrary"),
        name="mla_q_up",
    )(lat, g.reshape(1, q_lora).astype(F32), w, *rope_tabs)


def _mla_kv_kernel(kv_ref, pe_ref, g_ref, w_ref, c_ref, s1_ref, s2_ref, o_ref, a_sc, *, kv_lora):
    @pl.when(pl.program_id(1) == 0)
    def _():
        a_sc[:, :kv_lora] = _rms(kv_ref[...], g_ref[...]).astype(BF16)
        rot = _rope128(pe_ref[...], c_ref[...], s1_ref[...], s2_ref[...])
        a_sc[:, kv_lora:] = rot.astype(BF16)

    o_ref[...] = jnp.dot(a_sc[...], w_ref[...], preferred_element_type=F32).astype(o_ref.dtype)


def mla_kv_up(lat, g, w_ext, rope_tabs, seq, q_lora, kv_lora):
    m = lat.shape[0]
    n = w_ext.shape[1]
    tm = _tile("kvup_tm", seq)
    tn = _tile("kvup_tn", n)
    pos_blocks = seq // tm
    tab_spec = pl.BlockSpec((tm, LANE), lambda i, j: (i % pos_blocks, 0))
    kv_blk = q_lora // kv_lora
    pe_blk = (q_lora + kv_lora) // LANE
    return pl.pallas_call(
        functools.partial(_mla_kv_kernel, kv_lora=kv_lora),
        grid=(m // tm, n // tn),
        in_specs=[pl.BlockSpec((tm, kv_lora), lambda i, j: (i, kv_blk)),
                  pl.BlockSpec((tm, LANE), lambda i, j: (i, pe_blk)),
                  pl.BlockSpec((1, kv_lora), lambda i, j: (0, 0)),
                  pl.BlockSpec((kv_lora + LANE, tn), lambda i, j: (0, j)),
                  tab_spec, tab_spec, tab_spec],
        out_specs=pl.BlockSpec((tm, tn), lambda i, j: (i, j)),
        out_shape=jax.ShapeDtypeStruct((m, n), BF16),
        scratch_shapes=[pltpu.VMEM((tm, kv_lora + LANE), BF16)],
        compiler_params=_params("parallel", "arbitrary"),
        name="mla_kv_up",
    )(lat, lat, g.reshape(1, kv_lora).astype(F32), w_ext, *rope_tabs)


def _lane_tile(x, n):
    return jnp.concatenate([x] * n, axis=1)


def _softmax_init(m_sc, acc_sc):
    m_sc[...] = jnp.full(m_sc.shape, -jnp.inf, F32)
    acc_sc[...] = jnp.zeros(acc_sc.shape, F32)


def _softmax_update(s, vx, m_sc, acc_sc, c=None):
    tk = s.shape[1]
    m_prev = m_sc[...]
    m_cur = jnp.max(s, axis=1, keepdims=True)
    if c is not None:
        m_cur = m_cur + c
    m_new = jnp.maximum(m_prev, m_cur)
    alpha = jnp.exp2(m_prev - m_new)
    shift = m_new if c is None else m_new - c
    p = jnp.exp2(s - _lane_tile(shift, tk // LANE))
    acc_sc[...] = (_lane_tile(alpha, 2) * acc_sc[...]
                   + jnp.dot(p.astype(BF16), vx, preferred_element_type=F32))
    m_sc[...] = m_new


def _extend_v(v_ref, vx_sc, tk):
    width = v_ref.shape[1]

    def body(j, carry):
        r = pl.ds(pl.multiple_of(j * tk, tk), tk)
        vx_sc[r, :width] = v_ref[r, :]
        vx_sc[r, width:] = jnp.ones((tk, vx_sc.shape[1] - width), vx_sc.dtype)
        return carry

    lax.fori_loop(0, v_ref.shape[0] // tk, body, 0)


def _chunk_rows(j, tk):
    start = j * tk
    return pl.ds(start if isinstance(j, int) else pl.multiple_of(start, tk), tk)


def _qk(q, k):
    return lax.dot_general(q, k, (((1,), (1,)), ((), ())), preferred_element_type=F32)


def _chunk_loop(lo, hi, scores, s_bufs, consume):
    def by_parity(j, fn):
        for parity in (0, 1):
            @pl.when((j & 1) == parity)
            def _():
                fn(j, s_bufs[parity], s_bufs[1 - parity])

    def one(j, cur, nxt):
        nxt[...] = scores(j + 1)
        consume(j, cur)

    def two(j, cur, nxt):
        nxt[...] = scores(j + 1)
        consume(j, cur)
        cur[...] = scores(j + 2)
        consume(j + 1, nxt)

    pairs = lax.shift_right_logical(jnp.maximum(hi - lo, 0), 1)

    def body(t, carry):
        by_parity(lo + 2 * t, two)
        return carry

    lax.fori_loop(0, pairs, body, 0)
    rest = lo + 2 * pairs

    @pl.when(rest < hi)
    def _():
        by_parity(rest, one)


def _first_chunk(s_bufs, scores):
    first = functools.reduce(jnp.logical_and, [pl.program_id(a) == 0 for a in range(3)])

    @pl.when(first)
    def _():
        s_bufs[0][...] = scores(0)


def _last_chunk(n_chunks, next_scores, s_bufs, consume):
    s_bufs[0][...] = next_scores()
    consume(n_chunks - 1, s_bufs[1])


def _next_step(b, h, i, batch, heads, nq):
    n = jnp.minimum((b * heads + h) * nq + i + 1, batch * heads * nq - 1)
    bh = n // nq
    return bh // heads, bh % heads, n % nq


def _mla_attn_kernel(q_ref, k_ref, v_ref, qn_ref, kn_ref, o_ref, vx_sc, sa_sc, sb_sc, m_sc, acc_sc,
                     *, tk):
    @pl.when(pl.program_id(2) == 0)
    def _():
        _extend_v(v_ref, vx_sc, tk)

    _softmax_init(m_sc, acc_sc)
    rows = lambda j: _chunk_rows(j, tk)
    scores = lambda j: _qk(q_ref[...], k_ref[rows(j), :])

    def consume(j, s_ref):
        _softmax_update(s_ref[...], vx_sc[rows(j), :], m_sc, acc_sc)

    nk = k_ref.shape[0] // tk
    s_bufs = (sa_sc, sb_sc)
    _first_chunk(s_bufs, scores)
    _chunk_loop(0, nk - 1, scores, s_bufs, consume)
    _last_chunk(nk, lambda: _qk(qn_ref[...], kn_ref[...]), s_bufs, consume)
    acc = acc_sc[...]
    o_ref[...] = (acc[:, :V_MLA] / acc[:, V_MLA:]).astype(o_ref.dtype)


def mla_attention(q, kv, batch, seq, heads):
    tq = _tile("mla_tq", seq)
    tk = _tile("mla_tk", seq)
    nq, nk = seq // tq, seq // tk
    assert nk % 2 == 0
    v_blk0 = heads * MLA_QK_PAD // V_MLA
    nxt = lambda b, h, i: _next_step(b, h, i, batch, heads, nq)

    def q_next(b, h, i):
        b2, h2, i2 = nxt(b, h, i)
        return b2 * nq + i2, h2

    def k_next(b, h, i):
        b2, h2, _ = nxt(b, h, i)
        return b2 * nk, h2

    return pl.pallas_call(
        functools.partial(_mla_attn_kernel, tk=tk),
        grid=(batch, heads, nq),
        in_specs=[pl.BlockSpec((tq, MLA_QK_PAD), lambda b, h, i: (b * nq + i, h)),
                  pl.BlockSpec((seq, MLA_QK_PAD), lambda b, h, i: (b, h)),
                  pl.BlockSpec((seq, V_MLA), lambda b, h, i: (b, v_blk0 + h)),
                  pl.BlockSpec((tq, MLA_QK_PAD), q_next),
                  pl.BlockSpec((tk, MLA_QK_PAD), k_next)],
        out_specs=pl.BlockSpec((tq, V_MLA), lambda b, h, i: (b * nq + i, h)),
        out_shape=jax.ShapeDtypeStruct((batch * seq, heads * V_MLA), BF16),
        scratch_shapes=[pltpu.VMEM((seq, 2 * V_MLA), BF16),
                        pltpu.VMEM((tq, tk), F32), pltpu.VMEM((tq, tk), F32),
                        pltpu.VMEM((tq, LANE), F32), pltpu.VMEM((tq, 2 * V_MLA), F32)],
        compiler_params=_params("arbitrary", "arbitrary", "arbitrary"),
        name="mla_attention",
    )(q, kv, kv, q, kv)


def _first_biased_chunk(i, tq, tk, nk, n_bias):
    return jnp.minimum(jnp.maximum(i * tq - (MAX_DISTANCE - 1), 0) // tk, nk - n_bias)


def _diff_attn_kernel(tab_ref, far_ref, ids_ref, q_ref, k_ref, v_ref, qn_ref, kn_ref, lq1_ref, lk1_ref, lq2_ref,
                      lk2_ref, g_ref, o_ref, q2_sc, vx_sc, bias_sc, sa_sc, sb_sc, m_sc, acc_sc, *, tq, tk,
                      step, off0, n_bias, lambda_init):
    h, i = pl.program_id(1), pl.program_id(2)
    seq = k_ref.shape[0]
    nk = seq // tk

    @pl.when(i == 0)
    def _():
        _extend_v(v_ref, vx_sc, tk)
        for t in range(bias_sc.shape[0]):
            ids = ids_ref[t]
            f = jnp.zeros(ids.shape, F32)
            for b in range(N_BUCKETS):
                f = jnp.where(ids == b, tab_ref[h, b], f)
            f = f * LOG2E
            bias_sc[t] = pltpu.roll(jnp.broadcast_to(f, (tq, tq + tk)), tk + 1, 1,
                                    stride=1, stride_axis=0)[:, :tk]

    def stacked(q):
        lane = lax.broadcasted_iota(jnp.int32, q.shape, 1)
        zero = jnp.zeros_like(q)
        return jnp.concatenate([jnp.where(lane < DH_DIFF, q, zero),
                                jnp.where(lane >= DH_DIFF, q, zero)], axis=0)

    q2_sc[...] = stacked(q_ref[...])
    _softmax_init(m_sc, acc_sc)

    rows = lambda j: _chunk_rows(j, tk)
    scores = lambda j: _qk(q2_sc[...], k_ref[rows(j), :])
    c_left = tab_ref[h, far_ref[0]] * LOG2E
    c_right = tab_ref[h, far_ref[1]] * LOG2E

    j0 = _first_biased_chunk(i, tq, tk, nk, n_bias)

    def chunk_at(d):
        j = j0 + d
        return j if d < n_bias else jnp.where(j >= nk, j - nk, j)

    def consume(d, s_ref):
        j = chunk_at(d)
        vx = vx_sc[rows(j), :]
        if d < n_bias:
            bias = bias_sc[(j * tk - i * tq - off0) // step]
            s = s_ref[...] + jnp.concatenate([bias, bias], axis=0)
            _softmax_update(s, vx, m_sc, acc_sc)
        else:
            _softmax_update(s_ref[...], vx, m_sc, acc_sc, jnp.where(j < j0, c_left, c_right))

    s_bufs = (sa_sc, sb_sc)
    _first_chunk(s_bufs, lambda _: scores(j0))
    for d in range(nk - 1):
        s_bufs[(d + 1) % 2][...] = scores(chunk_at(d + 1))
        consume(d, s_bufs[d % 2])
    _last_chunk(nk, lambda: _qk(stacked(qn_ref[...]), kn_ref[...]), s_bufs, consume)

    acc = acc_sc[...]
    o = acc[:, :V_DIFF] / acc[:, V_DIFF:]
    lam = (jnp.exp(jnp.sum(lq1_ref[...] * lk1_ref[...], axis=1, keepdims=True))
           - jnp.exp(jnp.sum(lq2_ref[...] * lk2_ref[...], axis=1, keepdims=True)) + lambda_init)
    a = o[:tq] - lam * o[tq:]
    o_ref[...] = (_rms(a, g_ref[...]) * (1.0 - lambda_init)).astype(o_ref.dtype)


def _t5_bucket(rel):
    nb = N_BUCKETS // 2
    max_exact = nb // 2
    ret = (rel > 0).astype(jnp.int32) * nb
    n = jnp.abs(rel)
    nf = jnp.maximum(n, max_exact).astype(F32)
    large = max_exact + (jnp.log(nf / max_exact) / math.log(MAX_DISTANCE / max_exact)
                         * (nb - max_exact)).astype(jnp.int32)
    large = jnp.minimum(large, nb - 1)
    return ret + jnp.where(n < max_exact, n, large)


def diff_attention(qkv, rel_bias, lams, subln_g, lambda_init, batch, seq, heads):
    tq = _tile("diff_tq", seq)
    tk = _tile("diff_tk", seq)
    nq = seq // tq
    step = math.gcd(tq, tk)
    nk = seq // tk
    near_lo, near_hi = -(tk - 1 + MAX_DISTANCE), tq - 1 + MAX_DISTANCE
    n_bias = -(-(near_hi - near_lo - tk) // tk) + 1
    assert nk % 2 == 0 and nk >= n_bias
    off0 = min((near_lo // step + 1) * step, tq - n_bias * tk)
    offs = list(range(off0, max(near_hi - 1, (n_bias - 1) * tk) + 1, step))
    win = (jnp.array(offs, jnp.int32)[:, None] - (tq - 1)
           + jnp.arange(tq + tk, dtype=jnp.int32)[None, :])
    ids = _t5_bucket(win).reshape(len(offs), 1, tq + tk)
    far_ids = _t5_bucket(jnp.array([-MAX_DISTANCE, MAX_DISTANCE], jnp.int32))
    vec = lambda: pl.BlockSpec((1, DH_DIFF), lambda b, h, i: (0, 0))
    kern = functools.partial(_diff_attn_kernel, tq=tq, tk=tk, step=step, off0=off0, n_bias=n_bias,
                             lambda_init=lambda_init)
    nxt = lambda b, h, i: _next_step(b, h, i, batch, heads, nq)

    def q_next(b, h, i):
        b2, h2, i2 = nxt(b, h, i)
        return b2 * nq + i2, h2

    def k_next(b, h, i):
        b2, h2, i2 = nxt(b, h, i)
        return b2 * nk + _first_biased_chunk(i2, tq, tk, nk, n_bias), heads + h2

    return pl.pallas_call(
        kern,
        grid=(batch, heads, nq),
        in_specs=[pl.BlockSpec(memory_space=pltpu.SMEM),
                  pl.BlockSpec(memory_space=pltpu.SMEM),
                  pl.BlockSpec(ids.shape, lambda b, h, i: (0, 0, 0)),
                  pl.BlockSpec((tq, V_DIFF), lambda b, h, i: (b * nq + i, h)),
                  pl.BlockSpec((seq, V_DIFF), lambda b, h, i: (b, heads + h)),
                  pl.BlockSpec((seq, V_DIFF), lambda b, h, i: (b, 2 * heads + h)),
                  pl.BlockSpec((tq, V_DIFF), q_next),
                  pl.BlockSpec((tk, V_DIFF), k_next),
                  vec(), vec(), vec(), vec(),
                  pl.BlockSpec((1, V_DIFF), lambda b, h, i: (0, 0))],
        out_specs=pl.BlockSpec((tq, V_DIFF), lambda b, h, i: (b * nq + i, h)),
        out_shape=jax.ShapeDtypeStruct((batch * seq, heads * V_DIFF), BF16),
        scratch_shapes=[pltpu.VMEM((2 * tq, V_DIFF), BF16), pltpu.VMEM((seq, 2 * V_DIFF), BF16),
                        pltpu.VMEM((len(offs), tq, tk), F32),
                        pltpu.VMEM((2 * tq, tk), F32), pltpu.VMEM((2 * tq, tk), F32),
                        pltpu.VMEM((2 * tq, LANE), F32), pltpu.VMEM((2 * tq, 2 * V_DIFF), F32)],
        compiler_params=_params("arbitrary", "arbitrary", "arbitrary"),
        name="diff_attention",
    )(rel_bias.T.astype(F32), far_ids, ids, qkv, qkv, qkv, qkv, qkv,
      *[v.reshape(1, DH_DIFF).astype(F32) for v in lams], subln_g.reshape(1, V_DIFF).astype(F32))


def _ffn_up_kernel(a_ref, prev_ref, next_ref, wg_ref, wu_ref, cwg_ref, cwu_ref, cbg_ref, cbu_ref,
                   o_ref, a_sc, *, tm, tiles_per_seq):
    i = pl.program_id(0)

    @pl.when(pl.program_id(1) == 0)
    def _():
        pos = i % tiles_per_seq
        prev = prev_ref[...]
        nxt = next_ref[...]
        a_sc[:HALO] = jnp.where(pos == 0, jnp.zeros_like(prev), prev)
        a_sc[HALO:HALO + tm] = a_ref[...]
        a_sc[HALO + tm:] = jnp.where(pos == tiles_per_seq - 1, jnp.zeros_like(nxt), nxt)

    def conv(w_ref, cw_ref, cb_ref):
        r = jnp.dot(a_sc[...], w_ref[...], preferred_element_type=F32)
        rows = r.shape[0]
        r_prev = pltpu.roll(r, 1, 0)[HALO:HALO + tm]
        r_next = pltpu.roll(r, rows - 1, 0)[HALO:HALO + tm]
        cw = cw_ref[...]
        return (r_prev * cw[0:1] + r[HALO:HALO + tm] * cw[1:2] + r_next * cw[2:3]) + cb_ref[...]

    gate = conv(wg_ref, cwg_ref, cbg_ref)
    up = conv(wu_ref, cwu_ref, cbu_ref)
    o_ref[...] = (gate * (1.0 / (1.0 + jnp.exp(-gate))) * up).astype(o_ref.dtype)


def ffn_up(h, w_up, conv_w, conv_b, seq):
    m, d = h.shape
    d_ff = w_up.shape[1] // 2
    tm = _tile("ffn_tm", seq)
    tn = _tile("ffn_tn", d_ff)
    nb = d_ff // tn
    hb = tm // HALO
    last_hb = m // HALO - 1
    kern = functools.partial(_ffn_up_kernel, tm=tm, tiles_per_seq=seq // tm)
    return pl.pallas_call(
        kern,
        grid=(m // tm, nb),
        in_specs=[pl.BlockSpec((tm, d), lambda i, j: (i, 0)),
                  pl.BlockSpec((HALO, d), lambda i, j: (jnp.maximum(i * hb - 1, 0), 0)),
                  pl.BlockSpec((HALO, d), lambda i, j: (jnp.minimum((i + 1) * hb, last_hb), 0)),
                  pl.BlockSpec((d, tn), lambda i, j: (0, j)),
                  pl.BlockSpec((d, tn), lambda i, j: (0, j + nb)),
                  pl.BlockSpec((3, tn), lambda i, j: (0, j)),
                  pl.BlockSpec((3, tn), lambda i, j: (0, j + nb)),
                  pl.BlockSpec((1, tn), lambda i, j: (0, j)),
                  pl.BlockSpec((1, tn), lambda i, j: (0, j + nb))],
        out_specs=pl.BlockSpec((tm, tn), lambda i, j: (i, j)),
        out_shape=jax.ShapeDtypeStruct((m, d_ff), BF16),
        scratch_shapes=[pltpu.VMEM((tm + 2 * HALO, d), BF16)],
        compiler_params=_params("parallel", "arbitrary"),
        name="ffn_up_conv_silu",
    )(h, h, h, w_up, w_up, conv_w, conv_w, conv_b, conv_b)


def _prep_w_in(w_in, d_model, heads):
    dq = heads * 2 * DH_DIFF
    dv = heads * V_DIFF
    qkv_cols = 2 * dq + dv
    lat_cols = w_in.shape[1] - qkv_cols - 2 * d_model
    lat_pad = -lat_cols % LANE
    row_scale = jnp.concatenate([jnp.full((dq,), DH_DIFF ** -0.5 * LOG2E, F32),
                                 jnp.ones((w_in.shape[1] - dq,), F32)])
    wt = (jnp.swapaxes(w_in, 0, 1) * row_scale[:, None]).astype(BF16)
    w_qkv = wt[:qkv_cols]
    w_lat = jnp.pad(wt[qkv_cols:qkv_cols + lat_cols], ((0, lat_pad), (0, 0)))
    w_gate = wt[qkv_cols + lat_cols:]
    return w_qkv, w_lat, w_gate


def _prep_w_q_up(w, heads):
    k = w.shape[0]
    w = w.reshape(k, heads, QK_NOPE + QK_ROPE)
    w = jnp.pad(w, ((0, 0), (0, 0), (0, MLA_QK_PAD - QK_NOPE - QK_ROPE)))
    return w.reshape(k, heads * MLA_QK_PAD).astype(BF16)


def _prep_w_kv_up(w, heads):
    kv = w.shape[0]
    w = w.reshape(kv, heads, QK_NOPE + V_MLA)
    top_k = jnp.pad(w[:, :, :QK_NOPE], ((0, 0), (0, 0), (0, MLA_QK_PAD - QK_NOPE)))
    eye = jnp.pad(jnp.eye(QK_ROPE, dtype=F32), ((0, LANE - QK_ROPE), (QK_NOPE, MLA_QK_PAD - QK_NOPE - QK_ROPE)))
    bot_k = jnp.broadcast_to(eye[:, None, :], (LANE, heads, MLA_QK_PAD))
    w_k = jnp.concatenate([top_k, bot_k], axis=0).reshape(kv + LANE, heads * MLA_QK_PAD)
    w_v = jnp.pad(w[:, :, QK_NOPE:].reshape(kv, heads * V_MLA), ((0, LANE), (0, 0)))
    return jnp.concatenate([w_k, w_v], axis=1).astype(BF16)


def _rope_tables(seq):
    pos = jnp.arange(seq, dtype=F32)
    inv_freq = ROPE_THETA ** (-jnp.arange(0, QK_ROPE, 2, dtype=F32) / QK_ROPE)
    ang = pos[:, None] * inv_freq[None, :]
    cos, sin = jnp.cos(ang), jnp.sin(ang)
    half = QK_ROPE // 2
    z = lambda n: jnp.zeros((seq, n), F32)
    c = jnp.concatenate([cos, cos, z(LANE - QK_ROPE)], axis=1)
    s1 = jnp.concatenate([-sin, z(LANE - half)], axis=1)
    s2 = jnp.concatenate([z(half), sin, z(LANE - QK_ROPE)], axis=1)
    return c, s1, s2


def _encoder_layer(x, batch, seq, layer_idx, rel_bias, wts):
    (rms_attn_g, w_qkv, w_lat, w_gate, lams, diff_subln_g, mla_q_norm_g, w_q_up, mla_kv_norm_g,
     w_kv_ext, w_branch_a, w_branch_b, w_out, rms_ffn_g, w_ffn_up, conv_w, conv_b, w_ffn_down) = wts
    m, d_model = x.shape
    heads = w_branch_a.shape[0] // V_DIFF
    q_lora = mla_q_norm_g.shape[0]
    kv_lora = mla_kv_norm_g.shape[0]
    assert q_lora % kv_lora == 0 and (q_lora + kv_lora) % LANE == 0
    lambda_init = 0.8 - 0.6 * math.exp(-0.3 * layer_idx)
    tm = _tile("mm_tm", seq)

    h = rmsnorm(x, rms_attn_g, BF16)
    qkv = matmul_nt(h, w_qkv, BF16, tm, _tile("in_tn", w_qkv.shape[0]), "in_proj_qkv")
    lat = matmul_nt(h, w_lat, F32, _tile("lat_tm", seq), w_lat.shape[0], "in_proj_latent")
    gates = matmul_nt(h, w_gate, F32, tm, _tile("in_tn", d_model), "in_proj_gates")

    a_out = diff_attention(qkv, rel_bias, lams, diff_subln_g, lambda_init, batch, seq, heads)

    rope_tabs = _rope_tables(seq)
    q_mla = mla_q_up(lat, mla_q_norm_g, w_q_up, rope_tabs, seq, q_lora)
    kv_mla = mla_kv_up(lat, mla_kv_norm_g, w_kv_ext, rope_tabs, seq, q_lora, kv_lora)
    b_out = mla_attention(q_mla, kv_mla, batch, seq, heads)

    merged = gated_merge(a_out, b_out, w_branch_a, w_branch_b, gates, tm, _tile("mm_tn", d_model))
    x = matmul_residual(merged, w_out, x, tm, _tile("mm_tn", d_model), "out_proj")

    h2 = rmsnorm(x, rms_ffn_g, BF16)
    act = ffn_up(h2, w_ffn_up, conv_w, conv_b, seq)
    return matmul_residual(act, w_ffn_down, x, _tile("down_tm", seq), _tile("down_tn", d_model),
                           "ffn_down")


def kernel(x_prompt, x_sample, rel_bias, final_norm_g, rms_attn_g, w_in, lambda_q1, lambda_k1,
           lambda_q2, lambda_k2, diff_subln_g, mla_q_norm_g, w_mla_q_up, mla_kv_norm_g,
           w_mla_kv_up, w_branch_a, w_branch_b, w_out, rms_ffn_g, w_ffn_up, conv_w, conv_b,
           w_ffn_down):
    depth = w_in.shape[0]
    d_model = x_prompt.shape[-1]
    heads = w_branch_a.shape[1] // V_DIFF

    layers = []
    for l in range(depth):
        w_qkv, w_lat, w_gate = _prep_w_in(w_in[l], d_model, heads)
        layers.append((
            rms_attn_g[l], w_qkv, w_lat, w_gate,
            (lambda_q1[l], lambda_k1[l], lambda_q2[l], lambda_k2[l]), diff_subln_g[l],
            mla_q_norm_g[l], _prep_w_q_up(w_mla_q_up[l], heads),
            mla_kv_norm_g[l], _prep_w_kv_up(w_mla_kv_up[l], heads),
            w_branch_a[l].astype(BF16), w_branch_b[l].astype(BF16), w_out[l].astype(BF16),
            rms_ffn_g[l], w_ffn_up[l].astype(BF16), conv_w[l].astype(F32),
            conv_b[l].reshape(1, -1).astype(F32), w_ffn_down[l].astype(BF16)))

    def trunk(x):
        batch, seq, _ = x.shape
        y = x.reshape(batch * seq, d_model)
        for l in range(depth):
            y = _encoder_layer(y, batch, seq, l, rel_bias, layers[l])
        return rmsnorm(y, final_norm_g, x.dtype).reshape(x.shape)

    return (trunk(x_prompt), trunk(x_sample))
```

```python
import functools
import math

import jax
import jax.numpy as jnp
from jax import lax
from jax.experimental import pallas as pl
from jax.experimental.pallas import tpu as pltpu

DH_DIFF = 64
V_DIFF = 2 * DH_DIFF
QK_NOPE = 128
QK_ROPE = 64
V_MLA = 128
ROPE_THETA = 10000.0
N_BUCKETS = 32
MAX_DISTANCE = 128
EPS = 1e-6
LOG2E = math.log2(math.e)

LANE = 128
BF16_SUBLANE = 16
VMEM_LIMIT_BYTES = 56 * 1024 * 1024

MLA_QK_PAD = 2 * LANE
HALO = BF16_SUBLANE

TILES = dict(
    norm_tm=512,
    mm_tm=1024, mm_tn=512, in_tn=1024,
    lat_tm=512,
    qup_tm=1024, qup_tn=1024,
    kvup_tm=1024, kvup_tn=1024,
    ffn_tm=1024, ffn_tn=256,
    down_tm=512, down_tn=512,
    diff_tq=512, diff_tk=1024,
    mla_tq=1024, mla_tk=2048,
)

F32 = jnp.float32
BF16 = jnp.bfloat16


def _tile(name, dim):
    t = min(TILES[name], dim)
    assert dim % t == 0, (name, dim, t)
    return t


def _params(*sem):
    return pltpu.CompilerParams(dimension_semantics=sem, vmem_limit_bytes=VMEM_LIMIT_BYTES)


def _rms(x, g):
    return x * lax.rsqrt(jnp.mean(x * x, axis=-1, keepdims=True) + EPS) * g


def _rope128(p, c, s1, s2):
    return (p * c + pltpu.roll(p, LANE - QK_ROPE // 2, 1) * s1
            + pltpu.roll(p, QK_ROPE // 2, 1) * s2)


def _rmsnorm_kernel(x_ref, g_ref, o_ref):
    o_ref[...] = _rms(x_ref[...].astype(F32), g_ref[...]).astype(o_ref.dtype)


def rmsnorm(x, g, out_dtype):
    m, d = x.shape
    tm = _tile("norm_tm", m)
    return pl.pallas_call(
        _rmsnorm_kernel,
        grid=(m // tm,),
        in_specs=[pl.BlockSpec((tm, d), lambda i: (i, 0)),
                  pl.BlockSpec((1, d), lambda i: (0, 0))],
        out_specs=pl.BlockSpec((tm, d), lambda i: (i, 0)),
        out_shape=jax.ShapeDtypeStruct((m, d), out_dtype),
        compiler_params=_params("parallel"),
        name="rmsnorm",
    )(x, g.reshape(1, d).astype(F32))


def _mm_nt_kernel(a_ref, bt_ref, o_ref):
    o_ref[...] = _qk(a_ref[...], bt_ref[...]).astype(o_ref.dtype)


def matmul_nt(a, bt, out_dtype, tm, tn, name):
    m, k = a.shape
    n = bt.shape[0]
    return pl.pallas_call(
        _mm_nt_kernel,
        grid=(m // tm, n // tn),
        in_specs=[pl.BlockSpec((tm, k), lambda i, j: (i, 0)),
                  pl.BlockSpec((tn, k), lambda i, j: (j, 0))],
        out_specs=pl.BlockSpec((tm, tn), lambda i, j: (i, j)),
        out_shape=jax.ShapeDtypeStruct((m, n), out_dtype),
        compiler_params=_params("parallel", "arbitrary"),
        name=name,
    )(a, bt)


def _mm_res_kernel(a_ref, b_ref, r_ref, o_ref):
    o_ref[...] = r_ref[...] + jnp.dot(a_ref[...], b_ref[...], preferred_element_type=F32)


def matmul_residual(a, b, res, tm, tn, name):
    m, k = a.shape
    n = b.shape[1]
    return pl.pallas_call(
        _mm_res_kernel,
        grid=(m // tm, n // tn),
        in_specs=[pl.BlockSpec((tm, k), lambda i, j: (i, 0)),
                  pl.BlockSpec((k, tn), lambda i, j: (0, j)),
                  pl.BlockSpec((tm, tn), lambda i, j: (i, j))],
        out_specs=pl.BlockSpec((tm, tn), lambda i, j: (i, j)),
        out_shape=jax.ShapeDtypeStruct((m, n), F32),
        compiler_params=_params("parallel", "arbitrary"),
        name=name,
    )(a, b, res)


def _merge_kernel(a_ref, b_ref, wa_ref, wb_ref, ga_ref, gb_ref, o_ref):
    ya = jnp.dot(a_ref[...], wa_ref[...], preferred_element_type=F32)
    yb = jnp.dot(b_ref[...], wb_ref[...], preferred_element_type=F32)
    sa = 1.0 / (1.0 + jnp.exp(-ga_ref[...]))
    sb = 1.0 / (1.0 + jnp.exp(-gb_ref[...]))
    o_ref[...] = (sa * ya + sb * yb).astype(o_ref.dtype)


def gated_merge(a, b, wa, wb, gates, tm, tn):
    m, ka = a.shape
    kb = b.shape[1]
    n = wa.shape[1]
    nb = n // tn
    return pl.pallas_call(
        _merge_kernel,
        grid=(m // tm, nb),
        in_specs=[pl.BlockSpec((tm, ka), lambda i, j: (i, 0)),
                  pl.BlockSpec((tm, kb), lambda i, j: (i, 0)),
                  pl.BlockSpec((ka, tn), lambda i, j: (0, j)),
                  pl.BlockSpec((kb, tn), lambda i, j: (0, j)),
                  pl.BlockSpec((tm, tn), lambda i, j: (i, j)),
                  pl.BlockSpec((tm, tn), lambda i, j: (i, j + nb))],
        out_specs=pl.BlockSpec((tm, tn), lambda i, j: (i, j)),
        out_shape=jax.ShapeDtypeStruct((m, n), BF16),
        compiler_params=_params("parallel", "arbitrary"),
        name="gated_merge",
    )(a, b, wa, wb, gates, gates)


def _mla_q_kernel(lat_ref, g_ref, w_ref, c_ref, s1_ref, s2_ref, o_ref, a_sc, *, heads, scale):
    @pl.when(pl.program_id(1) == 0)
    def _():
        a_sc[...] = _rms(lat_ref[...], g_ref[...]).astype(BF16)

    acc = jnp.dot(a_sc[...], w_ref[...], preferred_element_type=F32)
    c, s1, s2 = c_ref[...], s1_ref[...], s2_ref[...]
    for h in range(heads):
        lo = h * MLA_QK_PAD
        o_ref[:, lo:lo + QK_NOPE] = (acc[:, lo:lo + QK_NOPE] * scale).astype(o_ref.dtype)
        rot = _rope128(acc[:, lo + QK_NOPE:lo + MLA_QK_PAD], c, s1, s2)
        o_ref[:, lo + QK_NOPE:lo + MLA_QK_PAD] = (rot * scale).astype(o_ref.dtype)


def mla_q_up(lat, g, w, rope_tabs, seq, q_lora):
    m = lat.shape[0]
    n = w.shape[1]
    tm = _tile("qup_tm", seq)
    tn = _tile("qup_tn", n)
    pos_blocks = seq // tm
    tab_spec = pl.BlockSpec((tm, LANE), lambda i, j: (i % pos_blocks, 0))
    kern = functools.partial(_mla_q_kernel, heads=tn // MLA_QK_PAD,
                             scale=(QK_NOPE + QK_ROPE) ** -0.5 * LOG2E)
    return pl.pallas_call(
        kern,
        grid=(m // tm, n // tn),
        in_specs=[pl.BlockSpec((tm, q_lora), lambda i, j: (i, 0)),
                  pl.BlockSpec((1, q_lora), lambda i, j: (0, 0)),
                  pl.BlockSpec((q_lora, tn), lambda i, j: (0, j)),
                  tab_spec, tab_spec, tab_spec],
        out_specs=pl.BlockSpec((tm, tn), lambda i, j: (i, j)),
        out_shape=jax.ShapeDtypeStruct((m, n), BF16),
        scratch_shapes=[pltpu.VMEM((tm, q_lora), BF16)],
        compiler_params=_params("parallel", "arbitrary"),
        name="mla_q_up",
    )(lat, g.reshape(1, q_lora).astype(F32), w, *rope_tabs)


def _mla_kv_kernel(kv_ref, pe_ref, g_ref, w_ref, c_ref, s1_ref, s2_ref, o_ref, a_sc, *, kv_lora):
    @pl.when(pl.program_id(1) == 0)
    def _():
        a_sc[:, :kv_lora] = _rms(kv_ref[...], g_ref[...]).astype(BF16)
        rot = _rope128(pe_ref[...], c_ref[...], s1_ref[...], s2_ref[...])
        a_sc[:, kv_lora:] = rot.astype(BF16)

    o_ref[...] = jnp.dot(a_sc[...], w_ref[...], preferred_element_type=F32).astype(o_ref.dtype)


def mla_kv_up(lat, g, w_ext, rope_tabs, seq, q_lora, kv_lora):
    m = lat.shape[0]
    n = w_ext.shape[1]
    tm = _tile("kvup_tm", seq)
    tn = _tile("kvup_tn", n)
    pos_blocks = seq // tm
    tab_spec = pl.BlockSpec((tm, LANE), lambda i, j: (i % pos_blocks, 0))
    kv_blk = q_lora // kv_lora
    pe_blk = (q_lora + kv_lora) // LANE
    return pl.pallas_call(
        functools.partial(_mla_kv_kernel, kv_lora=kv_lora),
        grid=(m // tm, n // tn),
        in_specs=[pl.BlockSpec((tm, kv_lora), lambda i, j: (i, kv_blk)),
                  pl.BlockSpec((tm, LANE), lambda i, j: (i, pe_blk)),
                  pl.BlockSpec((1, kv_lora), lambda i, j: (0, 0)),
                  pl.BlockSpec((kv_lora + LANE, tn), lambda i, j: (0, j)),
                  tab_spec, tab_spec, tab_spec],
        out_specs=pl.BlockSpec((tm, tn), lambda i, j: (i, j)),
        out_shape=jax.ShapeDtypeStruct((m, n), BF16),
        scratch_shapes=[pltpu.VMEM((tm, kv_lora + LANE), BF16)],
        compiler_params=_params("parallel", "arbitrary"),
        name="mla_kv_up",
    )(lat, lat, g.reshape(1, kv_lora).astype(F32), w_ext, *rope_tabs)


def _lane_tile(x, n):
    return jnp.concatenate([x] * n, axis=1)


def _softmax_init(m_sc, acc_sc):
    m_sc[...] = jnp.full(m_sc.shape, -jnp.inf, F32)
    acc_sc[...] = jnp.zeros(acc_sc.shape, F32)


def _softmax_update(s, vx, m_sc, acc_sc, c=None):
    tk = s.shape[1]
    m_prev = m_sc[...]
    m_cur = jnp.max(s, axis=1, keepdims=True)
    if c is not None:
        m_cur = m_cur + c
    m_new = jnp.maximum(m_prev, m_cur)
    alpha = jnp.exp2(m_prev - m_new)
    shift = m_new if c is None else m_new - c
    p = jnp.exp2(s - _lane_tile(shift, tk // LANE))
    acc_sc[...] = (_lane_tile(alpha, 2) * acc_sc[...]
                   + jnp.dot(p.astype(BF16), vx, preferred_element_type=F32))
    m_sc[...] = m_new


def _extend_v(v_ref, vx_sc, tk):
    width = v_ref.shape[1]

    def body(j, carry):
        r = pl.ds(pl.multiple_of(j * tk, tk), tk)
        vx_sc[r, :width] = v_ref[r, :]
        vx_sc[r, width:] = jnp.ones((tk, vx_sc.shape[1] - width), vx_sc.dtype)
        return carry

    lax.fori_loop(0, v_ref.shape[0] // tk, body, 0)


def _chunk_rows(j, tk):
    start = j * tk
    return pl.ds(start if isinstance(j, int) else pl.multiple_of(start, tk), tk)


def _qk(q, k):
    return lax.dot_general(q, k, (((1,), (1,)), ((), ())), preferred_element_type=F32)


def _chunk_loop(lo, hi, scores, s_bufs, consume):
    def by_parity(j, fn):
        for parity in (0, 1):
            @pl.when((j & 1) == parity)
            def _():
                fn(j, s_bufs[parity], s_bufs[1 - parity])

    def one(j, cur, nxt):
        nxt[...] = scores(j + 1)
        consume(j, cur)

    def two(j, cur, nxt):
        nxt[...] = scores(j + 1)
        consume(j, cur)
        cur[...] = scores(j + 2)
        consume(j + 1, nxt)

    pairs = lax.shift_right_logical(jnp.maximum(hi - lo, 0), 1)

    def body(t, carry):
        by_parity(lo + 2 * t, two)
        return carry

    lax.fori_loop(0, pairs, body, 0)
    rest = lo + 2 * pairs

    @pl.when(rest < hi)
    def _():
        by_parity(rest, one)


def _first_chunk(s_bufs, scores):
    first = functools.reduce(jnp.logical_and, [pl.program_id(a) == 0 for a in range(3)])

    @pl.when(first)
    def _():
        s_bufs[0][...] = scores(0)


def _last_chunk(n_chunks, next_scores, s_bufs, consume):
    s_bufs[0][...] = next_scores()
    consume(n_chunks - 1, s_bufs[1])


def _next_step(b, h, i, batch, heads, nq):
    n = jnp.minimum((b * heads + h) * nq + i + 1, batch * heads * nq - 1)
    bh = n // nq
    return bh // heads, bh % heads, n % nq


def _mla_attn_kernel(q_ref, k_ref, v_ref, qn_ref, kn_ref, o_ref, vx_sc, sa_sc, sb_sc, m_sc, acc_sc,
                     *, tq, tk):
    @pl.when(pl.program_id(2) == 0)
    def _():
        _extend_v(v_ref, vx_sc, tk)

    rows = lambda j: _chunk_rows(j, tk)
    nk = k_ref.shape[0] // tk
    s_bufs = (sa_sc, sb_sc)

    def consume(j, s_ref):
        _softmax_update(s_ref[...], vx_sc[rows(j), :], m_sc, acc_sc)

    n_tiles = q_ref.shape[0] // tq
    for t in range(n_tiles):
        q_rows = pl.ds(t * tq, tq)
        scores = lambda j, q_rows=q_rows: _qk(q_ref[q_rows, :], k_ref[rows(j), :])
        if t == 0:
            _first_chunk(s_bufs, scores)
        _softmax_init(m_sc, acc_sc)
        _chunk_loop(0, nk - 1, scores, s_bufs, consume)
        if t + 1 < n_tiles:
            next_scores = lambda t=t: _qk(q_ref[pl.ds((t + 1) * tq, tq), :], k_ref[rows(0), :])
        else:
            next_scores = lambda: _qk(qn_ref[...], kn_ref[...])
        _last_chunk(nk, next_scores, s_bufs, consume)
        acc = acc_sc[...]
        o_ref[q_rows, :] = (acc[:, :V_MLA] / acc[:, V_MLA:]).astype(o_ref.dtype)


def _tiles_per_step(nq):
    return 2 if nq % 2 == 0 else 1


def mla_attention(q, kv, batch, seq, heads):
    tq = _tile("mla_tq", seq)
    tk = _tile("mla_tk", seq)
    nq, nk = seq // tq, seq // tk
    assert nk % 2 == 0
    per_step = _tiles_per_step(nq)
    steps = nq // per_step
    v_blk0 = heads * MLA_QK_PAD // V_MLA
    nxt = lambda b, h, i: _next_step(b, h, i, batch, heads, steps)

    def q_next(b, h, i):
        b2, h2, i2 = nxt(b, h, i)
        return b2 * nq + i2 * per_step, h2

    def k_next(b, h, i):
        b2, h2, _ = nxt(b, h, i)
        return b2 * nk, h2

    return pl.pallas_call(
        functools.partial(_mla_attn_kernel, tq=tq, tk=tk),
        grid=(batch, heads, steps),
        in_specs=[pl.BlockSpec((per_step * tq, MLA_QK_PAD), lambda b, h, i: (b * steps + i, h)),
                  pl.BlockSpec((seq, MLA_QK_PAD), lambda b, h, i: (b, h)),
                  pl.BlockSpec((seq, V_MLA), lambda b, h, i: (b, v_blk0 + h)),
                  pl.BlockSpec((tq, MLA_QK_PAD), q_next),
                  pl.BlockSpec((tk, MLA_QK_PAD), k_next)],
        out_specs=pl.BlockSpec((per_step * tq, V_MLA), lambda b, h, i: (b * steps + i, h)),
        out_shape=jax.ShapeDtypeStruct((batch * seq, heads * V_MLA), BF16),
        scratch_shapes=[pltpu.VMEM((seq, 2 * V_MLA), BF16),
                        pltpu.VMEM((tq, tk), F32), pltpu.VMEM((tq, tk), F32),
                        pltpu.VMEM((tq, LANE), F32), pltpu.VMEM((tq, 2 * V_MLA), F32)],
        compiler_params=_params("arbitrary", "arbitrary", "arbitrary"),
        name="mla_attention",
    )(q, kv, kv, q, kv)


def _first_biased_chunk(i, tq, tk, nk, n_bias):
    return jnp.minimum(jnp.maximum(i * tq - (MAX_DISTANCE - 1), 0) // tk, nk - n_bias)


def _diff_attn_kernel(tab_ref, far_ref, ids_ref, q_ref, k_ref, v_ref, qn_ref, kn_ref, lq1_ref, lk1_ref, lq2_ref,
                      lk2_ref, g_ref, o_ref, q2_sc, vx_sc, bias_sc, sa_sc, sb_sc, m_sc, acc_sc, *, tq, tk,
                      step, off0, n_bias, lambda_init):
    h, i = pl.program_id(1), pl.program_id(2)
    seq = k_ref.shape[0]
    nk = seq // tk

    @pl.when(i == 0)
    def _():
        _extend_v(v_ref, vx_sc, tk)
        for t in range(bias_sc.shape[0]):
            ids = ids_ref[t]
            f = jnp.zeros(ids.shape, F32)
            for b in range(N_BUCKETS):
                f = jnp.where(ids == b, tab_ref[h, b], f)
            f = f * LOG2E
            bias_sc[t] = pltpu.roll(jnp.broadcast_to(f, (tq, tq + tk)), tk + 1, 1,
                                    stride=1, stride_axis=0)[:, :tk]

    def stacked(q):
        lane = lax.broadcasted_iota(jnp.int32, q.shape, 1)
        zero = jnp.zeros_like(q)
        return jnp.concatenate([jnp.where(lane < DH_DIFF, q, zero),
                                jnp.where(lane >= DH_DIFF, q, zero)], axis=0)

    rows = lambda j: _chunk_rows(j, tk)
    c_left = tab_ref[h, far_ref[0]] * LOG2E
    c_right = tab_ref[h, far_ref[1]] * LOG2E
    lam = (jnp.exp(jnp.sum(lq1_ref[...] * lk1_ref[...], axis=1, keepdims=True))
           - jnp.exp(jnp.sum(lq2_ref[...] * lk2_ref[...], axis=1, keepdims=True)) + lambda_init)
    s_bufs = (sa_sc, sb_sc)

    def tile(t, n_tiles):
        it = i * n_tiles + t
        q_rows = pl.ds(t * tq, tq)
        j0 = _first_biased_chunk(it, tq, tk, nk, n_bias)
        q2_sc[...] = stacked(q_ref[q_rows, :])
        _softmax_init(m_sc, acc_sc)
        scores = lambda j: _qk(q2_sc[...], k_ref[rows(j), :])

        def chunk_at(d):
            j = j0 + d
            return j if d < n_bias else jnp.where(j >= nk, j - nk, j)

        def consume(d, s_ref):
            j = chunk_at(d)
            vx = vx_sc[rows(j), :]
            if d < n_bias:
                bias = bias_sc[(j * tk - it * tq - off0) // step]
                s = s_ref[...] + jnp.concatenate([bias, bias], axis=0)
                _softmax_update(s, vx, m_sc, acc_sc)
            else:
                _softmax_update(s_ref[...], vx, m_sc, acc_sc, jnp.where(j < j0, c_left, c_right))

        if t == 0:
            _first_chunk(s_bufs, lambda _: scores(j0))
        for d in range(nk - 1):
            s_bufs[(d + 1) % 2][...] = scores(chunk_at(d + 1))
            consume(d, s_bufs[d % 2])
        if t + 1 < n_tiles:
            j0_next = _first_biased_chunk(it + 1, tq, tk, nk, n_bias)
            next_scores = lambda: _qk(stacked(q_ref[pl.ds((t + 1) * tq, tq), :]),
                                      k_ref[rows(j0_next), :])
        else:
            next_scores = lambda: _qk(stacked(qn_ref[...]), kn_ref[...])
        _last_chunk(nk, next_scores, s_bufs, consume)

        acc = acc_sc[...]
        o = acc[:, :V_DIFF] / acc[:, V_DIFF:]
        a = o[:tq] - lam * o[tq:]
        o_ref[q_rows, :] = (_rms(a, g_ref[...]) * (1.0 - lambda_init)).astype(o_ref.dtype)

    n_tiles = q_ref.shape[0] // tq
    for t in range(n_tiles):
        tile(t, n_tiles)


def _t5_bucket(rel):
    nb = N_BUCKETS // 2
    max_exact = nb // 2
    ret = (rel > 0).astype(jnp.int32) * nb
    n = jnp.abs(rel)
    nf = jnp.maximum(n, max_exact).astype(F32)
    large = max_exact + (jnp.log(nf / max_exact) / math.log(MAX_DISTANCE / max_exact)
                         * (nb - max_exact)).astype(jnp.int32)
    large = jnp.minimum(large, nb - 1)
    return ret + jnp.where(n < max_exact, n, large)


def diff_attention(qkv, rel_bias, lams, subln_g, lambda_init, batch, seq, heads):
    tq = _tile("diff_tq", seq)
    tk = _tile("diff_tk", seq)
    nq = seq // tq
    step = math.gcd(tq, tk)
    nk = seq // tk
    near_lo, near_hi = -(tk - 1 + MAX_DISTANCE), tq - 1 + MAX_DISTANCE
    n_bias = -(-(near_hi - near_lo - tk) // tk) + 1
    assert nk % 2 == 0 and nk >= n_bias
    off0 = min((near_lo // step + 1) * step, tq - n_bias * tk)
    offs = list(range(off0, max(near_hi - 1, (n_bias - 1) * tk) + 1, step))
    win = (jnp.array(offs, jnp.int32)[:, None] - (tq - 1)
           + jnp.arange(tq + tk, dtype=jnp.int32)[None, :])
    ids = _t5_bucket(win).reshape(len(offs), 1, tq + tk)
    far_ids = _t5_bucket(jnp.array([-MAX_DISTANCE, MAX_DISTANCE], jnp.int32))
    vec = lambda: pl.BlockSpec((1, DH_DIFF), lambda b, h, i: (0, 0))
    kern = functools.partial(_diff_attn_kernel, tq=tq, tk=tk, step=step, off0=off0, n_bias=n_bias,
                             lambda_init=lambda_init)
    per_step = _tiles_per_step(nq)
    steps = nq // per_step
    nxt = lambda b, h, i: _next_step(b, h, i, batch, heads, steps)

    def q_next(b, h, i):
        b2, h2, i2 = nxt(b, h, i)
        return b2 * nq + i2 * per_step, h2

    def k_next(b, h, i):
        b2, h2, i2 = nxt(b, h, i)
        return b2 * nk + _first_biased_chunk(i2 * per_step, tq, tk, nk, n_bias), heads + h2

    return pl.pallas_call(
        kern,
        grid=(batch, heads, steps),
        in_specs=[pl.BlockSpec(memory_space=pltpu.SMEM),
                  pl.BlockSpec(memory_space=pltpu.SMEM),
                  pl.BlockSpec(ids.shape, lambda b, h, i: (0, 0, 0)),
                  pl.BlockSpec((per_step * tq, V_DIFF), lambda b, h, i: (b * steps + i, h)),
                  pl.BlockSpec((seq, V_DIFF), lambda b, h, i: (b, heads + h)),
                  pl.BlockSpec((seq, V_DIFF), lambda b, h, i: (b, 2 * heads + h)),
                  pl.BlockSpec((tq, V_DIFF), q_next),
                  pl.BlockSpec((tk, V_DIFF), k_next),
                  vec(), vec(), vec(), vec(),
                  pl.BlockSpec((1, V_DIFF), lambda b, h, i: (0, 0))],
        out_specs=pl.BlockSpec((per_step * tq, V_DIFF), lambda b, h, i: (b * steps + i, h)),
        out_shape=jax.ShapeDtypeStruct((batch * seq, heads * V_DIFF), BF16),
        scratch_shapes=[pltpu.VMEM((2 * tq, V_DIFF), BF16), pltpu.VMEM((seq, 2 * V_DIFF), BF16),
                        pltpu.VMEM((len(offs), tq, tk), F32),
                        pltpu.VMEM((2 * tq, tk), F32), pltpu.VMEM((2 * tq, tk), F32),
                        pltpu.VMEM((2 * tq, LANE), F32), pltpu.VMEM((2 * tq, 2 * V_DIFF), F32)],
        compiler_params=_params("arbitrary", "arbitrary", "arbitrary"),
        name="diff_attention",
    )(rel_bias.T.astype(F32), far_ids, ids, qkv, qkv, qkv, qkv, qkv,
      *[v.reshape(1, DH_DIFF).astype(F32) for v in lams], subln_g.reshape(1, V_DIFF).astype(F32))


def _ffn_up_kernel(a_ref, prev_ref, next_ref, wg_ref, wu_ref, cwg_ref, cwu_ref, cbg_ref, cbu_ref,
                   o_ref, a_sc, *, tm, tiles_per_seq):
    i = pl.program_id(0)

    @pl.when(pl.program_id(1) == 0)
    def _():
        pos = i % tiles_per_seq
        prev = prev_ref[...]
        nxt = next_ref[...]
        a_sc[:HALO] = jnp.where(pos == 0, jnp.zeros_like(prev), prev)
        a_sc[HALO:HALO + tm] = a_ref[...]
        a_sc[HALO + tm:] = jnp.where(pos == tiles_per_seq - 1, jnp.zeros_like(nxt), nxt)

    def conv(w_ref, cw_ref, cb_ref):
        r = jnp.dot(a_sc[...], w_ref[...], preferred_element_type=F32)
        rows = r.shape[0]
        r_prev = pltpu.roll(r, 1, 0)[HALO:HALO + tm]
        r_next = pltpu.roll(r, rows - 1, 0)[HALO:HALO + tm]
        cw = cw_ref[...]
        return (r_prev * cw[0:1] + r[HALO:HALO + tm] * cw[1:2] + r_next * cw[2:3]) + cb_ref[...]

    gate = conv(wg_ref, cwg_ref, cbg_ref)
    up = conv(wu_ref, cwu_ref, cbu_ref)
    o_ref[...] = (gate * (1.0 / (1.0 + jnp.exp(-gate))) * up).astype(o_ref.dtype)


def ffn_up(h, w_up, conv_w, conv_b, seq):
    m, d = h.shape
    d_ff = w_up.shape[1] // 2
    tm = _tile("ffn_tm", seq)
    tn = _tile("ffn_tn", d_ff)
    nb = d_ff // tn
    hb = tm // HALO
    last_hb = m // HALO - 1
    kern = functools.partial(_ffn_up_kernel, tm=tm, tiles_per_seq=seq // tm)
    return pl.pallas_call(
        kern,
        grid=(m // tm, nb),
        in_specs=[pl.BlockSpec((tm, d), lambda i, j: (i, 0)),
                  pl.BlockSpec((HALO, d), lambda i, j: (jnp.maximum(i * hb - 1, 0), 0)),
                  pl.BlockSpec((HALO, d), lambda i, j: (jnp.minimum((i + 1) * hb, last_hb), 0)),
                  pl.BlockSpec((d, tn), lambda i, j: (0, j)),
                  pl.BlockSpec((d, tn), lambda i, j: (0, j + nb)),
                  pl.BlockSpec((3, tn), lambda i, j: (0, j)),
                  pl.BlockSpec((3, tn), lambda i, j: (0, j + nb)),
                  pl.BlockSpec((1, tn), lambda i, j: (0, j)),
                  pl.BlockSpec((1, tn), lambda i, j: (0, j + nb))],
        out_specs=pl.BlockSpec((tm, tn), lambda i, j: (i, j)),
        out_shape=jax.ShapeDtypeStruct((m, d_ff), BF16),
        scratch_shapes=[pltpu.VMEM((tm + 2 * HALO, d), BF16)],
        compiler_params=_params("parallel", "arbitrary"),
        name="ffn_up_conv_silu",
    )(h, h, h, w_up, w_up, conv_w, conv_w, conv_b, conv_b)


def _prep_w_in(w_in, d_model, heads):
    dq = heads * 2 * DH_DIFF
    dv = heads * V_DIFF
    qkv_cols = 2 * dq + dv
    lat_cols = w_in.shape[1] - qkv_cols - 2 * d_model
    lat_pad = -lat_cols % LANE
    row_scale = jnp.concatenate([jnp.full((dq,), DH_DIFF ** -0.5 * LOG2E, F32),
                                 jnp.ones((w_in.shape[1] - dq,), F32)])
    wt = (jnp.swapaxes(w_in, 0, 1) * row_scale[:, None]).astype(BF16)
    w_qkv = wt[:qkv_cols]
    w_lat = jnp.pad(wt[qkv_cols:qkv_cols + lat_cols], ((0, lat_pad), (0, 0)))
    w_gate = wt[qkv_cols + lat_cols:]
    return w_qkv, w_lat, w_gate


def _prep_w_q_up(w, heads):
    k = w.shape[0]
    w = w.reshape(k, heads, QK_NOPE + QK_ROPE)
    w = jnp.pad(w, ((0, 0), (0, 0), (0, MLA_QK_PAD - QK_NOPE - QK_ROPE)))
    return w.reshape(k, heads * MLA_QK_PAD).astype(BF16)


def _prep_w_kv_up(w, heads):
    kv = w.shape[0]
    w = w.reshape(kv, heads, QK_NOPE + V_MLA)
    top_k = jnp.pad(w[:, :, :QK_NOPE], ((0, 0), (0, 0), (0, MLA_QK_PAD - QK_NOPE)))
    eye = jnp.pad(jnp.eye(QK_ROPE, dtype=F32), ((0, LANE - QK_ROPE), (QK_NOPE, MLA_QK_PAD - QK_NOPE - QK_ROPE)))
    bot_k = jnp.broadcast_to(eye[:, None, :], (LANE, heads, MLA_QK_PAD))
    w_k = jnp.concatenate([top_k, bot_k], axis=0).reshape(kv + LANE, heads * MLA_QK_PAD)
    w_v = jnp.pad(w[:, :, QK_NOPE:].reshape(kv, heads * V_MLA), ((0, LANE), (0, 0)))
    return jnp.concatenate([w_k, w_v], axis=1).astype(BF16)


def _rope_tables(seq):
    pos = jnp.arange(seq, dtype=F32)
    inv_freq = ROPE_THETA ** (-jnp.arange(0, QK_ROPE, 2, dtype=F32) / QK_ROPE)
    ang = pos[:, None] * inv_freq[None, :]
    cos, sin = jnp.cos(ang), jnp.sin(ang)
    half = QK_ROPE // 2
    z = lambda n: jnp.zeros((seq, n), F32)
    c = jnp.concatenate([cos, cos, z(LANE - QK_ROPE)], axis=1)
    s1 = jnp.concatenate([-sin, z(LANE - half)], axis=1)
    s2 = jnp.concatenate([z(half), sin, z(LANE - QK_ROPE)], axis=1)
    return c, s1, s2


def _encoder_layer(x, batch, seq, layer_idx, rel_bias, wts):
    (rms_attn_g, w_qkv, w_lat, w_gate, lams, diff_subln_g, mla_q_norm_g, w_q_up, mla_kv_norm_g,
     w_kv_ext, w_branch_a, w_branch_b, w_out, rms_ffn_g, w_ffn_up, conv_w, conv_b, w_ffn_down) = wts
    m, d_model = x.shape
    heads = w_branch_a.shape[0] // V_DIFF
    q_lora = mla_q_norm_g.shape[0]
    kv_lora = mla_kv_norm_g.shape[0]
    assert q_lora % kv_lora == 0 and (q_lora + kv_lora) % LANE == 0
    lambda_init = 0.8 - 0.6 * math.exp(-0.3 * layer_idx)
    tm = _tile("mm_tm", seq)

    h = rmsnorm(x, rms_attn_g, BF16)
    qkv = matmul_nt(h, w_qkv, BF16, tm, _tile("in_tn", w_qkv.shape[0]), "in_proj_qkv")
    lat = matmul_nt(h, w_lat, F32, _tile("lat_tm", seq), w_lat.shape[0], "in_proj_latent")
    gates = matmul_nt(h, w_gate, F32, tm, _tile("in_tn", d_model), "in_proj_gates")

    a_out = diff_attention(qkv, rel_bias, lams, diff_subln_g, lambda_init, batch, seq, heads)

    rope_tabs = _rope_tables(seq)
    q_mla = mla_q_up(lat, mla_q_norm_g, w_q_up, rope_tabs, seq, q_lora)
    kv_mla = mla_kv_up(lat, mla_kv_norm_g, w_kv_ext, rope_tabs, seq, q_lora, kv_lora)
    b_out = mla_attention(q_mla, kv_mla, batch, seq, heads)

    merged = gated_merge(a_out, b_out, w_branch_a, w_branch_b, gates, tm, _tile("mm_tn", d_model))
    x = matmul_residual(merged, w_out, x, tm, _tile("mm_tn", d_model), "out_proj")

    h2 = rmsnorm(x, rms_ffn_g, BF16)
    act = ffn_up(h2, w_ffn_up, conv_w, conv_b, seq)
    return matmul_residual(act, w_ffn_down, x, _tile("down_tm", seq), _tile("down_tn", d_model),
                           "ffn_down")


def kernel(x_prompt, x_sample, rel_bias, final_norm_g, rms_attn_g, w_in, lambda_q1, lambda_k1,
           lambda_q2, lambda_k2, diff_subln_g, mla_q_norm_g, w_mla_q_up, mla_kv_norm_g,
           w_mla_kv_up, w_branch_a, w_branch_b, w_out, rms_ffn_g, w_ffn_up, conv_w, conv_b,
           w_ffn_down):
    depth = w_in.shape[0]
    d_model = x_prompt.shape[-1]
    heads = w_branch_a.shape[1] // V_DIFF

    layers = []
    for l in range(depth):
        w_qkv, w_lat, w_gate = _prep_w_in(w_in[l], d_model, heads)
        layers.append((
            rms_attn_g[l], w_qkv, w_lat, w_gate,
            (lambda_q1[l], lambda_k1[l], lambda_q2[l], lambda_k2[l]), diff_subln_g[l],
            mla_q_norm_g[l], _prep_w_q_up(w_mla_q_up[l], heads),
            mla_kv_norm_g[l], _prep_w_kv_up(w_mla_kv_up[l], heads),
            w_branch_a[l].astype(BF16), w_branch_b[l].astype(BF16), w_out[l].astype(BF16),
            rms_ffn_g[l], w_ffn_up[l].astype(BF16), conv_w[l].astype(F32),
            conv_b[l].reshape(1, -1).astype(F32), w_ffn_down[l].astype(BF16)))

    def trunk(x):
        batch, seq, _ = x.shape
        y = x.reshape(batch * seq, d_model)
        for l in range(depth):
            y = _encoder_layer(y, batch, seq, l, rel_bias, layers[l])
        return rmsnorm(y, final_norm_g, x.dtype).reshape(x.shape)

    return (trunk(x_prompt), trunk(x_sample))
```

```python
import functools
import math

import jax
import jax.numpy as jnp
from jax import lax
from jax.experimental import pallas as pl
from jax.experimental.pallas import tpu as pltpu

DH_DIFF = 64
V_DIFF = 2 * DH_DIFF
QK_NOPE = 128
QK_ROPE = 64
V_MLA = 128
ROPE_THETA = 10000.0
N_BUCKETS = 32
MAX_DISTANCE = 128
EPS = 1e-6
LOG2E = math.log2(math.e)

LANE = 128
BF16_SUBLANE = 16
VMEM_LIMIT_BYTES = 56 * 1024 * 1024

MLA_QK_PAD = 2 * LANE
HALO = BF16_SUBLANE

TILES = dict(
    norm_tm=512,
    mm_tm=1024, mm_tn=512, in_tn=1024,
    lat_tm=512,
    qup_tm=1024, qup_tn=1024,
    kvup_tm=1024, kvup_tn=1024,
    ffn_tm=1024, ffn_tn=256,
    down_tm=512, down_tn=512,
    diff_tq=512, diff_tk=1024, diff_chunks_per_step=16,
    mla_tq=1024, mla_tk=2048, mla_chunks_per_step=8,
)

F32 = jnp.float32
BF16 = jnp.bfloat16


def _tile(name, dim):
    t = min(TILES[name], dim)
    assert dim % t == 0, (name, dim, t)
    return t


def _params(*sem):
    return pltpu.CompilerParams(dimension_semantics=sem, vmem_limit_bytes=VMEM_LIMIT_BYTES)


def _rms(x, g):
    return x * lax.rsqrt(jnp.mean(x * x, axis=-1, keepdims=True) + EPS) * g


def _rope128(p, c, s1, s2):
    return (p * c + pltpu.roll(p, LANE - QK_ROPE // 2, 1) * s1
            + pltpu.roll(p, QK_ROPE // 2, 1) * s2)


def _rmsnorm_kernel(x_ref, g_ref, o_ref):
    o_ref[...] = _rms(x_ref[...].astype(F32), g_ref[...]).astype(o_ref.dtype)


def rmsnorm(x, g, out_dtype):
    m, d = x.shape
    tm = _tile("norm_tm", m)
    return pl.pallas_call(
        _rmsnorm_kernel,
        grid=(m // tm,),
        in_specs=[pl.BlockSpec((tm, d), lambda i: (i, 0)),
                  pl.BlockSpec((1, d), lambda i: (0, 0))],
        out_specs=pl.BlockSpec((tm, d), lambda i: (i, 0)),
        out_shape=jax.ShapeDtypeStruct((m, d), out_dtype),
        compiler_params=_params("parallel"),
        name="rmsnorm",
    )(x, g.reshape(1, d).astype(F32))


def _mm_nt_kernel(a_ref, bt_ref, o_ref):
    o_ref[...] = _qk(a_ref[...], bt_ref[...]).astype(o_ref.dtype)


def matmul_nt(a, bt, out_dtype, tm, tn, name):
    m, k = a.shape
    n = bt.shape[0]
    return pl.pallas_call(
        _mm_nt_kernel,
        grid=(m // tm, n // tn),
        in_specs=[pl.BlockSpec((tm, k), lambda i, j: (i, 0)),
                  pl.BlockSpec((tn, k), lambda i, j: (j, 0))],
        out_specs=pl.BlockSpec((tm, tn), lambda i, j: (i, j)),
        out_shape=jax.ShapeDtypeStruct((m, n), out_dtype),
        compiler_params=_params("parallel", "arbitrary"),
        name=name,
    )(a, bt)


def _mm_res_kernel(a_ref, b_ref, r_ref, o_ref):
    o_ref[...] = r_ref[...] + jnp.dot(a_ref[...], b_ref[...], preferred_element_type=F32)


def matmul_residual(a, b, res, tm, tn, name):
    m, k = a.shape
    n = b.shape[1]
    return pl.pallas_call(
        _mm_res_kernel,
        grid=(m // tm, n // tn),
        in_specs=[pl.BlockSpec((tm, k), lambda i, j: (i, 0)),
                  pl.BlockSpec((k, tn), lambda i, j: (0, j)),
                  pl.BlockSpec((tm, tn), lambda i, j: (i, j))],
        out_specs=pl.BlockSpec((tm, tn), lambda i, j: (i, j)),
        out_shape=jax.ShapeDtypeStruct((m, n), F32),
        compiler_params=_params("parallel", "arbitrary"),
        name=name,
    )(a, b, res)


def _merge_kernel(a_ref, b_ref, wa_ref, wb_ref, ga_ref, gb_ref, o_ref):
    ya = jnp.dot(a_ref[...], wa_ref[...], preferred_element_type=F32)
    yb = jnp.dot(b_ref[...], wb_ref[...], preferred_element_type=F32)
    sa = 1.0 / (1.0 + jnp.exp(-ga_ref[...]))
    sb = 1.0 / (1.0 + jnp.exp(-gb_ref[...]))
    o_ref[...] = (sa * ya + sb * yb).astype(o_ref.dtype)


def gated_merge(a, b, wa, wb, gates, tm, tn):
    m, ka = a.shape
    kb = b.shape[1]
    n = wa.shape[1]
    nb = n // tn
    return pl.pallas_call(
        _merge_kernel,
        grid=(m // tm, nb),
        in_specs=[pl.BlockSpec((tm, ka), lambda i, j: (i, 0)),
                  pl.BlockSpec((tm, kb), lambda i, j: (i, 0)),
                  pl.BlockSpec((ka, tn), lambda i, j: (0, j)),
                  pl.BlockSpec((kb, tn), lambda i, j: (0, j)),
                  pl.BlockSpec((tm, tn), lambda i, j: (i, j)),
                  pl.BlockSpec((tm, tn), lambda i, j: (i, j + nb))],
        out_specs=pl.BlockSpec((tm, tn), lambda i, j: (i, j)),
        out_shape=jax.ShapeDtypeStruct((m, n), BF16),
        compiler_params=_params("parallel", "arbitrary"),
        name="gated_merge",
    )(a, b, wa, wb, gates, gates)


def _mla_q_kernel(lat_ref, g_ref, w_ref, c_ref, s1_ref, s2_ref, o_ref, a_sc, *, heads, scale):
    @pl.when(pl.program_id(1) == 0)
    def _():
        a_sc[...] = _rms(lat_ref[...], g_ref[...]).astype(BF16)

    acc = jnp.dot(a_sc[...], w_ref[...], preferred_element_type=F32)
    c, s1, s2 = c_ref[...], s1_ref[...], s2_ref[...]
    for h in range(heads):
        lo = h * MLA_QK_PAD
        o_ref[:, lo:lo + QK_NOPE] = (acc[:, lo:lo + QK_NOPE] * scale).astype(o_ref.dtype)
        rot = _rope128(acc[:, lo + QK_NOPE:lo + MLA_QK_PAD], c, s1, s2)
        o_ref[:, lo + QK_NOPE:lo + MLA_QK_PAD] = (rot * scale).astype(o_ref.dtype)


def mla_q_up(lat, g, w, rope_tabs, seq, q_lora):
    m = lat.shape[0]
    n = w.shape[1]
    tm = _tile("qup_tm", seq)
    tn = _tile("qup_tn", n)
    pos_blocks = seq // tm
    tab_spec = pl.BlockSpec((tm, LANE), lambda i, j: (i % pos_blocks, 0))
    kern = functools.partial(_mla_q_kernel, heads=tn // MLA_QK_PAD,
                             scale=(QK_NOPE + QK_ROPE) ** -0.5 * LOG2E)
    return pl.pallas_call(
        kern,
        grid=(m // tm, n // tn),
        in_specs=[pl.BlockSpec((tm, q_lora), lambda i, j: (i, 0)),
                  pl.BlockSpec((1, q_lora), lambda i, j: (0, 0)),
                  pl.BlockSpec((q_lora, tn), lambda i, j: (0, j)),
                  tab_spec, tab_spec, tab_spec],
        out_specs=pl.BlockSpec((tm, tn), lambda i, j: (i, j)),
        out_shape=jax.ShapeDtypeStruct((m, n), BF16),
        scratch_shapes=[pltpu.VMEM((tm, q_lora), BF16)],
        compiler_params=_params("parallel", "arbitrary"),
        name="mla_q_up",
    )(lat, g.reshape(1, q_lora).astype(F32), w, *rope_tabs)


def _mla_kv_kernel(kv_ref, pe_ref, g_ref, w_ref, c_ref, s1_ref, s2_ref, o_ref, a_sc, *, kv_lora):
    @pl.when(pl.program_id(1) == 0)
    def _():
        a_sc[:, :kv_lora] = _rms(kv_ref[...], g_ref[...]).astype(BF16)
        rot = _rope128(pe_ref[...], c_ref[...], s1_ref[...], s2_ref[...])
        a_sc[:, kv_lora:] = rot.astype(BF16)

    o_ref[...] = jnp.dot(a_sc[...], w_ref[...], preferred_element_type=F32).astype(o_ref.dtype)


def mla_kv_up(lat, g, w_ext, rope_tabs, seq, q_lora, kv_lora):
    m = lat.shape[0]
    n = w_ext.shape[1]
    tm = _tile("kvup_tm", seq)
    tn = _tile("kvup_tn", n)
    pos_blocks = seq // tm
    tab_spec = pl.BlockSpec((tm, LANE), lambda i, j: (i % pos_blocks, 0))
    kv_blk = q_lora // kv_lora
    pe_blk = (q_lora + kv_lora) // LANE
    return pl.pallas_call(
        functools.partial(_mla_kv_kernel, kv_lora=kv_lora),
        grid=(m // tm, n // tn),
        in_specs=[pl.BlockSpec((tm, kv_lora), lambda i, j: (i, kv_blk)),
                  pl.BlockSpec((tm, LANE), lambda i, j: (i, pe_blk)),
                  pl.BlockSpec((1, kv_lora), lambda i, j: (0, 0)),
                  pl.BlockSpec((kv_lora + LANE, tn), lambda i, j: (0, j)),
                  tab_spec, tab_spec, tab_spec],
        out_specs=pl.BlockSpec((tm, tn), lambda i, j: (i, j)),
        out_shape=jax.ShapeDtypeStruct((m, n), BF16),
        scratch_shapes=[pltpu.VMEM((tm, kv_lora + LANE), BF16)],
        compiler_params=_params("parallel", "arbitrary"),
        name="mla_kv_up",
    )(lat, lat, g.reshape(1, kv_lora).astype(F32), w_ext, *rope_tabs)


def _lane_tile(x, n):
    return jnp.concatenate([x] * n, axis=1)


def _softmax_init(m_sc, acc_sc):
    m_sc[...] = jnp.full(m_sc.shape, -jnp.inf, F32)
    acc_sc[...] = jnp.zeros(acc_sc.shape, F32)


def _softmax_update(s, vx, m_sc, acc_sc, c=None):
    tk = s.shape[1]
    m_prev = m_sc[...]
    m_cur = jnp.max(s, axis=1, keepdims=True)
    if c is not None:
        m_cur = m_cur + c
    m_new = jnp.maximum(m_prev, m_cur)
    alpha = jnp.exp2(m_prev - m_new)
    shift = m_new if c is None else m_new - c
    p = jnp.exp2(s - _lane_tile(shift, tk // LANE))
    acc_sc[...] = (_lane_tile(alpha, 2) * acc_sc[...]
                   + jnp.dot(p.astype(BF16), vx, preferred_element_type=F32))
    m_sc[...] = m_new


def _extend_v(v_ref, vx_sc, tk):
    width = v_ref.shape[1]

    def body(j, carry):
        r = pl.ds(pl.multiple_of(j * tk, tk), tk)
        vx_sc[r, :width] = v_ref[r, :]
        vx_sc[r, width:] = jnp.ones((tk, vx_sc.shape[1] - width), vx_sc.dtype)
        return carry

    lax.fori_loop(0, v_ref.shape[0] // tk, body, 0)


def _chunk_rows(j, tk):
    start = j * tk
    return pl.ds(start if isinstance(j, int) else pl.multiple_of(start, tk), tk)


def _qk(q, k):
    return lax.dot_general(q, k, (((1,), (1,)), ((), ())), preferred_element_type=F32)


def _chunk_loop(lo, hi, scores, s_bufs, consume):
    def by_parity(j, fn):
        for parity in (0, 1):
            @pl.when((j & 1) == parity)
            def _():
                fn(j, s_bufs[parity], s_bufs[1 - parity])

    def one(j, cur, nxt):
        nxt[...] = scores(j + 1)
        consume(j, cur)

    def two(j, cur, nxt):
        nxt[...] = scores(j + 1)
        consume(j, cur)
        cur[...] = scores(j + 2)
        consume(j + 1, nxt)

    pairs = lax.shift_right_logical(jnp.maximum(hi - lo, 0), 1)

    def body(t, carry):
        by_parity(lo + 2 * t, two)
        return carry

    lax.fori_loop(0, pairs, body, 0)
    rest = lo + 2 * pairs

    @pl.when(rest < hi)
    def _():
        by_parity(rest, one)


def _first_chunk(s_bufs, scores):
    first = functools.reduce(jnp.logical_and, [pl.program_id(a) == 0 for a in range(3)])

    @pl.when(first)
    def _():
        s_bufs[0][...] = scores(0)


def _last_chunk(n_chunks, next_scores, s_bufs, consume):
    s_bufs[0][...] = next_scores()
    consume(n_chunks - 1, s_bufs[1])


def _next_step(b, h, i, batch, heads, nq):
    n = jnp.minimum((b * heads + h) * nq + i + 1, batch * heads * nq - 1)
    bh = n // nq
    return bh // heads, bh % heads, n % nq


def _mla_attn_kernel(q_ref, k_ref, v_ref, qn_ref, kn_ref, o_ref, vx_sc, sa_sc, sb_sc, m_sc, acc_sc,
                     *, tq, tk):
    @pl.when(pl.program_id(2) == 0)
    def _():
        _extend_v(v_ref, vx_sc, tk)

    rows = lambda j: _chunk_rows(j, tk)
    nk = k_ref.shape[0] // tk
    s_bufs = (sa_sc, sb_sc)

    def consume(j, s_ref):
        _softmax_update(s_ref[...], vx_sc[rows(j), :], m_sc, acc_sc)

    n_tiles = q_ref.shape[0] // tq
    for t in range(n_tiles):
        q_rows = pl.ds(t * tq, tq)
        scores = lambda j, q_rows=q_rows: _qk(q_ref[q_rows, :], k_ref[rows(j), :])
        if t == 0:
            _first_chunk(s_bufs, scores)
        _softmax_init(m_sc, acc_sc)
        _chunk_loop(0, nk - 1, scores, s_bufs, consume)
        if t + 1 < n_tiles:
            next_scores = lambda t=t: _qk(q_ref[pl.ds((t + 1) * tq, tq), :], k_ref[rows(0), :])
        else:
            next_scores = lambda: _qk(qn_ref[...], kn_ref[...])
        _last_chunk(nk, next_scores, s_bufs, consume)
        acc = acc_sc[...]
        o_ref[q_rows, :] = (acc[:, :V_MLA] / acc[:, V_MLA:]).astype(o_ref.dtype)


def _tiles_per_step(nq, nk, max_chunks, max_tiles):
    return next(t for t in (4, 2, 1)
                if t <= max_tiles and nq % t == 0 and (t == 1 or t * nk <= max_chunks))


def mla_attention(q, kv, batch, seq, heads):
    tq = _tile("mla_tq", seq)
    tk = _tile("mla_tk", seq)
    nq, nk = seq // tq, seq // tk
    assert nk % 2 == 0
    per_step = _tiles_per_step(nq, nk, TILES["mla_chunks_per_step"], max_tiles=2)
    steps = nq // per_step
    v_blk0 = heads * MLA_QK_PAD // V_MLA
    nxt = lambda b, h, i: _next_step(b, h, i, batch, heads, steps)

    def q_next(b, h, i):
        b2, h2, i2 = nxt(b, h, i)
        return b2 * nq + i2 * per_step, h2

    def k_next(b, h, i):
        b2, h2, _ = nxt(b, h, i)
        return b2 * nk, h2

    return pl.pallas_call(
        functools.partial(_mla_attn_kernel, tq=tq, tk=tk),
        grid=(batch, heads, steps),
        in_specs=[pl.BlockSpec((per_step * tq, MLA_QK_PAD), lambda b, h, i: (b * steps + i, h)),
                  pl.BlockSpec((seq, MLA_QK_PAD), lambda b, h, i: (b, h)),
                  pl.BlockSpec((seq, V_MLA), lambda b, h, i: (b, v_blk0 + h)),
                  pl.BlockSpec((tq, MLA_QK_PAD), q_next),
                  pl.BlockSpec((tk, MLA_QK_PAD), k_next)],
        out_specs=pl.BlockSpec((per_step * tq, V_MLA), lambda b, h, i: (b * steps + i, h)),
        out_shape=jax.ShapeDtypeStruct((batch * seq, heads * V_MLA), BF16),
        scratch_shapes=[pltpu.VMEM((seq, 2 * V_MLA), BF16),
                        pltpu.VMEM((tq, tk), F32), pltpu.VMEM((tq, tk), F32),
                        pltpu.VMEM((tq, LANE), F32), pltpu.VMEM((tq, 2 * V_MLA), F32)],
        compiler_params=_params("arbitrary", "arbitrary", "arbitrary"),
        name="mla_attention",
    )(q, kv, kv, q, kv)


def _first_biased_chunk(i, tq, tk, nk, n_bias):
    return jnp.minimum(jnp.maximum(i * tq - (MAX_DISTANCE - 1), 0) // tk, nk - n_bias)


def _diff_attn_kernel(tab_ref, far_ref, ids_ref, q_ref, k_ref, v_ref, qn_ref, kn_ref, lq1_ref, lk1_ref, lq2_ref,
                      lk2_ref, g_ref, o_ref, q2_sc, vx_sc, bias_sc, sa_sc, sb_sc, m_sc, acc_sc, *, tq, tk,
                      step, off0, n_bias, lambda_init):
    h, i = pl.program_id(1), pl.program_id(2)
    seq = k_ref.shape[0]
    nk = seq // tk

    @pl.when(i == 0)
    def _():
        _extend_v(v_ref, vx_sc, tk)
        for t in range(bias_sc.shape[0]):
            ids = ids_ref[t]
            f = jnp.zeros(ids.shape, F32)
            for b in range(N_BUCKETS):
                f = jnp.where(ids == b, tab_ref[h, b], f)
            f = f * LOG2E
            bias_sc[t] = pltpu.roll(jnp.broadcast_to(f, (tq, tq + tk)), tk + 1, 1,
                                    stride=1, stride_axis=0)[:, :tk]

    def stacked(q):
        lane = lax.broadcasted_iota(jnp.int32, q.shape, 1)
        zero = jnp.zeros_like(q)
        return jnp.concatenate([jnp.where(lane < DH_DIFF, q, zero),
                                jnp.where(lane >= DH_DIFF, q, zero)], axis=0)

    rows = lambda j: _chunk_rows(j, tk)
    c_left = tab_ref[h, far_ref[0]] * LOG2E
    c_right = tab_ref[h, far_ref[1]] * LOG2E
    lam = (jnp.exp(jnp.sum(lq1_ref[...] * lk1_ref[...], axis=1, keepdims=True))
           - jnp.exp(jnp.sum(lq2_ref[...] * lk2_ref[...], axis=1, keepdims=True)) + lambda_init)
    s_bufs = (sa_sc, sb_sc)

    def tile(t, n_tiles):
        it = i * n_tiles + t
        q_rows = pl.ds(t * tq, tq)
        j0 = _first_biased_chunk(it, tq, tk, nk, n_bias)
        q2_sc[...] = stacked(q_ref[q_rows, :])
        _softmax_init(m_sc, acc_sc)
        scores = lambda j: _qk(q2_sc[...], k_ref[rows(j), :])

        def chunk_at(d):
            j = j0 + d
            return j if d < n_bias else jnp.where(j >= nk, j - nk, j)

        def consume(d, s_ref):
            j = chunk_at(d)
            vx = vx_sc[rows(j), :]
            if d < n_bias:
                bias = bias_sc[(j * tk - it * tq - off0) // step]
                s = s_ref[...] + jnp.concatenate([bias, bias], axis=0)
                _softmax_update(s, vx, m_sc, acc_sc)
            else:
                _softmax_update(s_ref[...], vx, m_sc, acc_sc, jnp.where(j < j0, c_left, c_right))

        if t == 0:
            _first_chunk(s_bufs, lambda _: scores(j0))
        for d in range(nk - 1):
            s_bufs[(d + 1) % 2][...] = scores(chunk_at(d + 1))
            consume(d, s_bufs[d % 2])
        if t + 1 < n_tiles:
            j0_next = _first_biased_chunk(it + 1, tq, tk, nk, n_bias)
            next_scores = lambda: _qk(stacked(q_ref[pl.ds((t + 1) * tq, tq), :]),
                                      k_ref[rows(j0_next), :])
        else:
            next_scores = lambda: _qk(stacked(qn_ref[...]), kn_ref[...])
        _last_chunk(nk, next_scores, s_bufs, consume)

        acc = acc_sc[...]
        o = acc[:, :V_DIFF] / acc[:, V_DIFF:]
        a = o[:tq] - lam * o[tq:]
        o_ref[q_rows, :] = (_rms(a, g_ref[...]) * (1.0 - lambda_init)).astype(o_ref.dtype)

    n_tiles = q_ref.shape[0] // tq
    for t in range(n_tiles):
        tile(t, n_tiles)


def _t5_bucket(rel):
    nb = N_BUCKETS // 2
    max_exact = nb // 2
    ret = (rel > 0).astype(jnp.int32) * nb
    n = jnp.abs(rel)
    nf = jnp.maximum(n, max_exact).astype(F32)
    large = max_exact + (jnp.log(nf / max_exact) / math.log(MAX_DISTANCE / max_exact)
                         * (nb - max_exact)).astype(jnp.int32)
    large = jnp.minimum(large, nb - 1)
    return ret + jnp.where(n < max_exact, n, large)


def diff_attention(qkv, rel_bias, lams, subln_g, lambda_init, batch, seq, heads):
    tq = _tile("diff_tq", seq)
    tk = _tile("diff_tk", seq)
    nq = seq // tq
    step = math.gcd(tq, tk)
    nk = seq // tk
    near_lo, near_hi = -(tk - 1 + MAX_DISTANCE), tq - 1 + MAX_DISTANCE
    n_bias = -(-(near_hi - near_lo - tk) // tk) + 1
    assert nk % 2 == 0 and nk >= n_bias
    off0 = min((near_lo // step + 1) * step, tq - n_bias * tk)
    offs = list(range(off0, max(near_hi - 1, (n_bias - 1) * tk) + 1, step))
    win = (jnp.array(offs, jnp.int32)[:, None] - (tq - 1)
           + jnp.arange(tq + tk, dtype=jnp.int32)[None, :])
    ids = _t5_bucket(win).reshape(len(offs), 1, tq + tk)
    far_ids = _t5_bucket(jnp.array([-MAX_DISTANCE, MAX_DISTANCE], jnp.int32))
    vec = lambda: pl.BlockSpec((1, DH_DIFF), lambda b, h, i: (0, 0))
    kern = functools.partial(_diff_attn_kernel, tq=tq, tk=tk, step=step, off0=off0, n_bias=n_bias,
                             lambda_init=lambda_init)
    per_step = _tiles_per_step(nq, nk, TILES["diff_chunks_per_step"], max_tiles=4)
    steps = nq // per_step
    nxt = lambda b, h, i: _next_step(b, h, i, batch, heads, steps)

    def q_next(b, h, i):
        b2, h2, i2 = nxt(b, h, i)
        return b2 * nq + i2 * per_step, h2

    def k_next(b, h, i):
        b2, h2, i2 = nxt(b, h, i)
        return b2 * nk + _first_biased_chunk(i2 * per_step, tq, tk, nk, n_bias), heads + h2

    return pl.pallas_call(
        kern,
        grid=(batch, heads, steps),
        in_specs=[pl.BlockSpec(memory_space=pltpu.SMEM),
                  pl.BlockSpec(memory_space=pltpu.SMEM),
                  pl.BlockSpec(ids.shape, lambda b, h, i: (0, 0, 0)),
                  pl.BlockSpec((per_step * tq, V_DIFF), lambda b, h, i: (b * steps + i, h)),
                  pl.BlockSpec((seq, V_DIFF), lambda b, h, i: (b, heads + h)),
                  pl.BlockSpec((seq, V_DIFF), lambda b, h, i: (b, 2 * heads + h)),
                  pl.BlockSpec((tq, V_DIFF), q_next),
                  pl.BlockSpec((tk, V_DIFF), k_next),
                  vec(), vec(), vec(), vec(),
                  pl.BlockSpec((1, V_DIFF), lambda b, h, i: (0, 0))],
        out_specs=pl.BlockSpec((per_step * tq, V_DIFF), lambda b, h, i: (b * steps + i, h)),
        out_shape=jax.ShapeDtypeStruct((batch * seq, heads * V_DIFF), BF16),
        scratch_shapes=[pltpu.VMEM((2 * tq, V_DIFF), BF16), pltpu.VMEM((seq, 2 * V_DIFF), BF16),
                        pltpu.VMEM((len(offs), tq, tk), F32),
                        pltpu.VMEM((2 * tq, tk), F32), pltpu.VMEM((2 * tq, tk), F32),
                        pltpu.VMEM((2 * tq, LANE), F32), pltpu.VMEM((2 * tq, 2 * V_DIFF), F32)],
        compiler_params=_params("arbitrary", "arbitrary", "arbitrary"),
        name="diff_attention",
    )(rel_bias.T.astype(F32), far_ids, ids, qkv, qkv, qkv, qkv, qkv,
      *[v.reshape(1, DH_DIFF).astype(F32) for v in lams], subln_g.reshape(1, V_DIFF).astype(F32))


def _ffn_up_kernel(a_ref, prev_ref, next_ref, wg_ref, wu_ref, cwg_ref, cwu_ref, cbg_ref, cbu_ref,
                   o_ref, a_sc, *, tm, tiles_per_seq):
    i = pl.program_id(0)

    @pl.when(pl.program_id(1) == 0)
    def _():
        pos = i % tiles_per_seq
        prev = prev_ref[...]
        nxt = next_ref[...]
        a_sc[:HALO] = jnp.where(pos == 0, jnp.zeros_like(prev), prev)
        a_sc[HALO:HALO + tm] = a_ref[...]
        a_sc[HALO + tm:] = jnp.where(pos == tiles_per_seq - 1, jnp.zeros_like(nxt), nxt)

    def conv(w_ref, cw_ref, cb_ref):
        r = jnp.dot(a_sc[...], w_ref[...], preferred_element_type=F32)
        rows = r.shape[0]
        r_prev = pltpu.roll(r, 1, 0)[HALO:HALO + tm]
        r_next = pltpu.roll(r, rows - 1, 0)[HALO:HALO + tm]
        cw = cw_ref[...]
        return (r_prev * cw[0:1] + r[HALO:HALO + tm] * cw[1:2] + r_next * cw[2:3]) + cb_ref[...]

    gate = conv(wg_ref, cwg_ref, cbg_ref)
    up = conv(wu_ref, cwu_ref, cbu_ref)
    o_ref[...] = (gate * (1.0 / (1.0 + jnp.exp(-gate))) * up).astype(o_ref.dtype)


def ffn_up(h, w_up, conv_w, conv_b, seq):
    m, d = h.shape
    d_ff = w_up.shape[1] // 2
    tm = _tile("ffn_tm", seq)
    tn = _tile("ffn_tn", d_ff)
    nb = d_ff // tn
    hb = tm // HALO
    last_hb = m // HALO - 1
    kern = functools.partial(_ffn_up_kernel, tm=tm, tiles_per_seq=seq // tm)
    return pl.pallas_call(
        kern,
        grid=(m // tm, nb),
        in_specs=[pl.BlockSpec((tm, d), lambda i, j: (i, 0)),
                  pl.BlockSpec((HALO, d), lambda i, j: (jnp.maximum(i * hb - 1, 0), 0)),
                  pl.BlockSpec((HALO, d), lambda i, j: (jnp.minimum((i + 1) * hb, last_hb), 0)),
                  pl.BlockSpec((d, tn), lambda i, j: (0, j)),
                  pl.BlockSpec((d, tn), lambda i, j: (0, j + nb)),
                  pl.BlockSpec((3, tn), lambda i, j: (0, j)),
                  pl.BlockSpec((3, tn), lambda i, j: (0, j + nb)),
                  pl.BlockSpec((1, tn), lambda i, j: (0, j)),
                  pl.BlockSpec((1, tn), lambda i, j: (0, j + nb))],
        out_specs=pl.BlockSpec((tm, tn), lambda i, j: (i, j)),
        out_shape=jax.ShapeDtypeStruct((m, d_ff), BF16),
        scratch_shapes=[pltpu.VMEM((tm + 2 * HALO, d), BF16)],
        compiler_params=_params("parallel", "arbitrary"),
        name="ffn_up_conv_silu",
    )(h, h, h, w_up, w_up, conv_w, conv_w, conv_b, conv_b)


def _prep_w_in(w_in, d_model, heads):
    dq = heads * 2 * DH_DIFF
    dv = heads * V_DIFF
    qkv_cols = 2 * dq + dv
    lat_cols = w_in.shape[1] - qkv_cols - 2 * d_model
    lat_pad = -lat_cols % LANE
    row_scale = jnp.concatenate([jnp.full((dq,), DH_DIFF ** -0.5 * LOG2E, F32),
                                 jnp.ones((w_in.shape[1] - dq,), F32)])
    wt = (jnp.swapaxes(w_in, 0, 1) * row_scale[:, None]).astype(BF16)
    w_qkv = wt[:qkv_cols]
    w_lat = jnp.pad(wt[qkv_cols:qkv_cols + lat_cols], ((0, lat_pad), (0, 0)))
    w_gate = wt[qkv_cols + lat_cols:]
    return w_qkv, w_lat, w_gate


def _prep_w_q_up(w, heads):
    k = w.shape[0]
    w = w.reshape(k, heads, QK_NOPE + QK_ROPE)
    w = jnp.pad(w, ((0, 0), (0, 0), (0, MLA_QK_PAD - QK_NOPE - QK_ROPE)))
    return w.reshape(k, heads * MLA_QK_PAD).astype(BF16)


def _prep_w_kv_up(w, heads):
    kv = w.shape[0]
    w = w.reshape(kv, heads, QK_NOPE + V_MLA)
    top_k = jnp.pad(w[:, :, :QK_NOPE], ((0, 0), (0, 0), (0, MLA_QK_PAD - QK_NOPE)))
    eye = jnp.pad(jnp.eye(QK_ROPE, dtype=F32), ((0, LANE - QK_ROPE), (QK_NOPE, MLA_QK_PAD - QK_NOPE - QK_ROPE)))
    bot_k = jnp.broadcast_to(eye[:, None, :], (LANE, heads, MLA_QK_PAD))
    w_k = jnp.concatenate([top_k, bot_k], axis=0).reshape(kv + LANE, heads * MLA_QK_PAD)
    w_v = jnp.pad(w[:, :, QK_NOPE:].reshape(kv, heads * V_MLA), ((0, LANE), (0, 0)))
    return jnp.concatenate([w_k, w_v], axis=1).astype(BF16)


def _rope_tables(seq):
    pos = jnp.arange(seq, dtype=F32)
    inv_freq = ROPE_THETA ** (-jnp.arange(0, QK_ROPE, 2, dtype=F32) / QK_ROPE)
    ang = pos[:, None] * inv_freq[None, :]
    cos, sin = jnp.cos(ang), jnp.sin(ang)
    half = QK_ROPE // 2
    z = lambda n: jnp.zeros((seq, n), F32)
    c = jnp.concatenate([cos, cos, z(LANE - QK_ROPE)], axis=1)
    s1 = jnp.concatenate([-sin, z(LANE - half)], axis=1)
    s2 = jnp.concatenate([z(half), sin, z(LANE - QK_ROPE)], axis=1)
    return c, s1, s2


def _encoder_layer(x, batch, seq, layer_idx, rel_bias, wts):
    (rms_attn_g, w_qkv, w_lat, w_gate, lams, diff_subln_g, mla_q_norm_g, w_q_up, mla_kv_norm_g,
     w_kv_ext, w_branch_a, w_branch_b, w_out, rms_ffn_g, w_ffn_up, conv_w, conv_b, w_ffn_down) = wts
    m, d_model = x.shape
    heads = w_branch_a.shape[0] // V_DIFF
    q_lora = mla_q_norm_g.shape[0]
    kv_lora = mla_kv_norm_g.shape[0]
    assert q_lora % kv_lora == 0 and (q_lora + kv_lora) % LANE == 0
    lambda_init = 0.8 - 0.6 * math.exp(-0.3 * layer_idx)
    tm = _tile("mm_tm", seq)

    h = rmsnorm(x, rms_attn_g, BF16)
    qkv = matmul_nt(h, w_qkv, BF16, tm, _tile("in_tn", w_qkv.shape[0]), "in_proj_qkv")
    lat = matmul_nt(h, w_lat, F32, _tile("lat_tm", seq), w_lat.shape[0], "in_proj_latent")
    gates = matmul_nt(h, w_gate, F32, tm, _tile("in_tn", d_model), "in_proj_gates")

    a_out = diff_attention(qkv, rel_bias, lams, diff_subln_g, lambda_init, batch, seq, heads)

    rope_tabs = _rope_tables(seq)
    q_mla = mla_q_up(lat, mla_q_norm_g, w_q_up, rope_tabs, seq, q_lora)
    kv_mla = mla_kv_up(lat, mla_kv_norm_g, w_kv_ext, rope_tabs, seq, q_lora, kv_lora)
    b_out = mla_attention(q_mla, kv_mla, batch, seq, heads)

    merged = gated_merge(a_out, b_out, w_branch_a, w_branch_b, gates, tm, _tile("mm_tn", d_model))
    x = matmul_residual(merged, w_out, x, tm, _tile("mm_tn", d_model), "out_proj")

    h2 = rmsnorm(x, rms_ffn_g, BF16)
    act = ffn_up(h2, w_ffn_up, conv_w, conv_b, seq)
    return matmul_residual(act, w_ffn_down, x, _tile("down_tm", seq), _tile("down_tn", d_model),
                           "ffn_down")


def kernel(x_prompt, x_sample, rel_bias, final_norm_g, rms_attn_g, w_in, lambda_q1, lambda_k1,
           lambda_q2, lambda_k2, diff_subln_g, mla_q_norm_g, w_mla_q_up, mla_kv_norm_g,
           w_mla_kv_up, w_branch_a, w_branch_b, w_out, rms_ffn_g, w_ffn_up, conv_w, conv_b,
           w_ffn_down):
    depth = w_in.shape[0]
    d_model = x_prompt.shape[-1]
    heads = w_branch_a.shape[1] // V_DIFF

    layers = []
    for l in range(depth):
        w_qkv, w_lat, w_gate = _prep_w_in(w_in[l], d_model, heads)
        layers.append((
            rms_attn_g[l], w_qkv, w_lat, w_gate,
            (lambda_q1[l], lambda_k1[l], lambda_q2[l], lambda_k2[l]), diff_subln_g[l],
            mla_q_norm_g[l], _prep_w_q_up(w_mla_q_up[l], heads),
            mla_kv_norm_g[l], _prep_w_kv_up(w_mla_kv_up[l], heads),
            w_branch_a[l].astype(BF16), w_branch_b[l].astype(BF16), w_out[l].astype(BF16),
            rms_ffn_g[l], w_ffn_up[l].astype(BF16), conv_w[l].astype(F32),
            conv_b[l].reshape(1, -1).astype(F32), w_ffn_down[l].astype(BF16)))

    def trunk(x):
        batch, seq, _ = x.shape
        y = x.reshape(batch * seq, d_model)
        for l in range(depth):
            y = _encoder_layer(y, batch, seq, l, rel_bias, layers[l])
        return rmsnorm(y, final_norm_g, x.dtype).reshape(x.shape)

    return (trunk(x_prompt), trunk(x_sample))
```

```python
import functools
import math

import jax
import jax.numpy as jnp
from jax import lax
from jax.experimental import pallas as pl
from jax.experimental.pallas import tpu as pltpu

DH_DIFF = 64
V_DIFF = 2 * DH_DIFF
QK_NOPE = 128
QK_ROPE = 64
V_MLA = 128
ROPE_THETA = 10000.0
N_BUCKETS = 32
MAX_DISTANCE = 128
EPS = 1e-6
LOG2E = math.log2(math.e)

LANE = 128
BF16_SUBLANE = 16
VMEM_LIMIT_BYTES = 56 * 1024 * 1024

MLA_QK_PAD = 2 * LANE
HALO = BF16_SUBLANE

TILES = dict(
    norm_tm=512,
    mm_tm=1024, mm_tn=512, in_tn=1024,
    lat_tm=512,
    qup_tm=1024, qup_tn=1024,
    kvup_tm=1024, kvup_tn=1024,
    ffn_tm=1024, ffn_tn=256,
    down_tm=512, down_tn=512,
    diff_tq=512, diff_tk=1024, diff_chunks_per_step=32,
    mla_tq=1024, mla_tk=2048, mla_chunks_per_step=8,
)

F32 = jnp.float32
BF16 = jnp.bfloat16


def _tile(name, dim):
    t = min(TILES[name], dim)
    assert dim % t == 0, (name, dim, t)
    return t


def _params(*sem):
    return pltpu.CompilerParams(dimension_semantics=sem, vmem_limit_bytes=VMEM_LIMIT_BYTES)


def _rms(x, g):
    return x * lax.rsqrt(jnp.mean(x * x, axis=-1, keepdims=True) + EPS) * g


def _rope128(p, c, s1, s2):
    return (p * c + pltpu.roll(p, LANE - QK_ROPE // 2, 1) * s1
            + pltpu.roll(p, QK_ROPE // 2, 1) * s2)


def _rmsnorm_kernel(x_ref, g_ref, o_ref):
    o_ref[...] = _rms(x_ref[...].astype(F32), g_ref[...]).astype(o_ref.dtype)


def rmsnorm(x, g, out_dtype):
    m, d = x.shape
    tm = _tile("norm_tm", m)
    return pl.pallas_call(
        _rmsnorm_kernel,
        grid=(m // tm,),
        in_specs=[pl.BlockSpec((tm, d), lambda i: (i, 0)),
                  pl.BlockSpec((1, d), lambda i: (0, 0))],
        out_specs=pl.BlockSpec((tm, d), lambda i: (i, 0)),
        out_shape=jax.ShapeDtypeStruct((m, d), out_dtype),
        compiler_params=_params("parallel"),
        name="rmsnorm",
    )(x, g.reshape(1, d).astype(F32))


def _mm_nt_kernel(a_ref, bt_ref, o_ref):
    o_ref[...] = _qk(a_ref[...], bt_ref[...]).astype(o_ref.dtype)


def matmul_nt(a, bt, out_dtype, tm, tn, name):
    m, k = a.shape
    n = bt.shape[0]
    return pl.pallas_call(
        _mm_nt_kernel,
        grid=(m // tm, n // tn),
        in_specs=[pl.BlockSpec((tm, k), lambda i, j: (i, 0)),
                  pl.BlockSpec((tn, k), lambda i, j: (j, 0))],
        out_specs=pl.BlockSpec((tm, tn), lambda i, j: (i, j)),
        out_shape=jax.ShapeDtypeStruct((m, n), out_dtype),
        compiler_params=_params("parallel", "arbitrary"),
        name=name,
    )(a, bt)


def _mm_res_kernel(a_ref, b_ref, r_ref, o_ref):
    o_ref[...] = r_ref[...] + jnp.dot(a_ref[...], b_ref[...], preferred_element_type=F32)


def matmul_residual(a, b, res, tm, tn, name):
    m, k = a.shape
    n = b.shape[1]
    return pl.pallas_call(
        _mm_res_kernel,
        grid=(m // tm, n // tn),
        in_specs=[pl.BlockSpec((tm, k), lambda i, j: (i, 0)),
                  pl.BlockSpec((k, tn), lambda i, j: (0, j)),
                  pl.BlockSpec((tm, tn), lambda i, j: (i, j))],
        out_specs=pl.BlockSpec((tm, tn), lambda i, j: (i, j)),
        out_shape=jax.ShapeDtypeStruct((m, n), F32),
        compiler_params=_params("parallel", "arbitrary"),
        name=name,
    )(a, b, res)


def _merge_kernel(a_ref, b_ref, wa_ref, wb_ref, ga_ref, gb_ref, o_ref):
    ya = jnp.dot(a_ref[...], wa_ref[...], preferred_element_type=F32)
    yb = jnp.dot(b_ref[...], wb_ref[...], preferred_element_type=F32)
    sa = 1.0 / (1.0 + jnp.exp(-ga_ref[...]))
    sb = 1.0 / (1.0 + jnp.exp(-gb_ref[...]))
    o_ref[...] = (sa * ya + sb * yb).astype(o_ref.dtype)


def gated_merge(a, b, wa, wb, gates, tm, tn):
    m, ka = a.shape
    kb = b.shape[1]
    n = wa.shape[1]
    nb = n // tn
    return pl.pallas_call(
        _merge_kernel,
        grid=(m // tm, nb),
        in_specs=[pl.BlockSpec((tm, ka), lambda i, j: (i, 0)),
                  pl.BlockSpec((tm, kb), lambda i, j: (i, 0)),
                  pl.BlockSpec((ka, tn), lambda i, j: (0, j)),
                  pl.BlockSpec((kb, tn), lambda i, j: (0, j)),
                  pl.BlockSpec((tm, tn), lambda i, j: (i, j)),
                  pl.BlockSpec((tm, tn), lambda i, j: (i, j + nb))],
        out_specs=pl.BlockSpec((tm, tn), lambda i, j: (i, j)),
        out_shape=jax.ShapeDtypeStruct((m, n), BF16),
        compiler_params=_params("parallel", "arbitrary"),
        name="gated_merge",
    )(a, b, wa, wb, gates, gates)


def _mla_q_kernel(lat_ref, g_ref, w_ref, c_ref, s1_ref, s2_ref, o_ref, a_sc, *, heads, scale):
    @pl.when(pl.program_id(1) == 0)
    def _():
        a_sc[...] = _rms(lat_ref[...], g_ref[...]).astype(BF16)

    acc = jnp.dot(a_sc[...], w_ref[...], preferred_element_type=F32)
    c, s1, s2 = c_ref[...], s1_ref[...], s2_ref[...]
    for h in range(heads):
        lo = h * MLA_QK_PAD
        o_ref[:, lo:lo + QK_NOPE] = (acc[:, lo:lo + QK_NOPE] * scale).astype(o_ref.dtype)
        rot = _rope128(acc[:, lo + QK_NOPE:lo + MLA_QK_PAD], c, s1, s2)
        o_ref[:, lo + QK_NOPE:lo + MLA_QK_PAD] = (rot * scale).astype(o_ref.dtype)


def mla_q_up(lat, g, w, rope_tabs, seq, q_lora):
    m = lat.shape[0]
    n = w.shape[1]
    tm = _tile("qup_tm", seq)
    tn = _tile("qup_tn", n)
    pos_blocks = seq // tm
    tab_spec = pl.BlockSpec((tm, LANE), lambda i, j: (i % pos_blocks, 0))
    kern = functools.partial(_mla_q_kernel, heads=tn // MLA_QK_PAD,
                             scale=(QK_NOPE + QK_ROPE) ** -0.5 * LOG2E)
    return pl.pallas_call(
        kern,
        grid=(m // tm, n // tn),
        in_specs=[pl.BlockSpec((tm, q_lora), lambda i, j: (i, 0)),
                  pl.BlockSpec((1, q_lora), lambda i, j: (0, 0)),
                  pl.BlockSpec((q_lora, tn), lambda i, j: (0, j)),
                  tab_spec, tab_spec, tab_spec],
        out_specs=pl.BlockSpec((tm, tn), lambda i, j: (i, j)),
        out_shape=jax.ShapeDtypeStruct((m, n), BF16),
        scratch_shapes=[pltpu.VMEM((tm, q_lora), BF16)],
        compiler_params=_params("parallel", "arbitrary"),
        name="mla_q_up",
    )(lat, g.reshape(1, q_lora).astype(F32), w, *rope_tabs)


def _mla_kv_kernel(kv_ref, pe_ref, g_ref, w_ref, c_ref, s1_ref, s2_ref, o_ref, a_sc, *, kv_lora):
    @pl.when(pl.program_id(1) == 0)
    def _():
        a_sc[:, :kv_lora] = _rms(kv_ref[...], g_ref[...]).astype(BF16)
        rot = _rope128(pe_ref[...], c_ref[...], s1_ref[...], s2_ref[...])
        a_sc[:, kv_lora:] = rot.astype(BF16)

    o_ref[...] = jnp.dot(a_sc[...], w_ref[...], preferred_element_type=F32).astype(o_ref.dtype)


def mla_kv_up(lat, g, w_ext, rope_tabs, seq, q_lora, kv_lora):
    m = lat.shape[0]
    n = w_ext.shape[1]
    tm = _tile("kvup_tm", seq)
    tn = _tile("kvup_tn", n)
    pos_blocks = seq // tm
    tab_spec = pl.BlockSpec((tm, LANE), lambda i, j: (i % pos_blocks, 0))
    kv_blk = q_lora // kv_lora
    pe_blk = (q_lora + kv_lora) // LANE
    return pl.pallas_call(
        functools.partial(_mla_kv_kernel, kv_lora=kv_lora),
        grid=(m // tm, n // tn),
        in_specs=[pl.BlockSpec((tm, kv_lora), lambda i, j: (i, kv_blk)),
                  pl.BlockSpec((tm, LANE), lambda i, j: (i, pe_blk)),
                  pl.BlockSpec((1, kv_lora), lambda i, j: (0, 0)),
                  pl.BlockSpec((kv_lora + LANE, tn), lambda i, j: (0, j)),
                  tab_spec, tab_spec, tab_spec],
        out_specs=pl.BlockSpec((tm, tn), lambda i, j: (i, j)),
        out_shape=jax.ShapeDtypeStruct((m, n), BF16),
        scratch_shapes=[pltpu.VMEM((tm, kv_lora + LANE), BF16)],
        compiler_params=_params("parallel", "arbitrary"),
        name="mla_kv_up",
    )(lat, lat, g.reshape(1, kv_lora).astype(F32), w_ext, *rope_tabs)


def _lane_tile(x, n):
    return jnp.concatenate([x] * n, axis=1)


def _softmax_init(m_sc, acc_sc):
    m_sc[...] = jnp.full(m_sc.shape, -jnp.inf, F32)
    acc_sc[...] = jnp.zeros(acc_sc.shape, F32)


def _softmax_update(s, vx, m_sc, acc_sc, c=None):
    tk = s.shape[1]
    m_prev = m_sc[...]
    m_cur = jnp.max(s, axis=1, keepdims=True)
    if c is not None:
        m_cur = m_cur + c
    m_new = jnp.maximum(m_prev, m_cur)
    alpha = jnp.exp2(m_prev - m_new)
    shift = m_new if c is None else m_new - c
    p = jnp.exp2(s - _lane_tile(shift, tk // LANE))
    acc_sc[...] = (_lane_tile(alpha, 2) * acc_sc[...]
                   + jnp.dot(p.astype(BF16), vx, preferred_element_type=F32))
    m_sc[...] = m_new


def _extend_v(v_ref, vx_sc, tk):
    width = v_ref.shape[1]

    def body(j, carry):
        r = pl.ds(pl.multiple_of(j * tk, tk), tk)
        vx_sc[r, :width] = v_ref[r, :]
        vx_sc[r, width:] = jnp.ones((tk, vx_sc.shape[1] - width), vx_sc.dtype)
        return carry

    lax.fori_loop(0, v_ref.shape[0] // tk, body, 0)


def _chunk_rows(j, tk):
    start = j * tk
    return pl.ds(start if isinstance(j, int) else pl.multiple_of(start, tk), tk)


def _qk(q, k):
    return lax.dot_general(q, k, (((1,), (1,)), ((), ())), preferred_element_type=F32)


def _chunk_loop(lo, hi, scores, s_bufs, consume):
    def by_parity(j, fn):
        for parity in (0, 1):
            @pl.when((j & 1) == parity)
            def _():
                fn(j, s_bufs[parity], s_bufs[1 - parity])

    def one(j, cur, nxt):
        nxt[...] = scores(j + 1)
        consume(j, cur)

    def two(j, cur, nxt):
        nxt[...] = scores(j + 1)
        consume(j, cur)
        cur[...] = scores(j + 2)
        consume(j + 1, nxt)

    pairs = lax.shift_right_logical(jnp.maximum(hi - lo, 0), 1)

    def body(t, carry):
        by_parity(lo + 2 * t, two)
        return carry

    lax.fori_loop(0, pairs, body, 0)
    rest = lo + 2 * pairs

    @pl.when(rest < hi)
    def _():
        by_parity(rest, one)


def _first_chunk(s_bufs, scores):
    first = functools.reduce(jnp.logical_and, [pl.program_id(a) == 0 for a in range(3)])

    @pl.when(first)
    def _():
        s_bufs[0][...] = scores(0)


def _last_chunk(n_chunks, next_scores, s_bufs, consume):
    s_bufs[0][...] = next_scores()
    consume(n_chunks - 1, s_bufs[1])


def _next_step(b, h, i, batch, heads, nq):
    n = jnp.minimum((b * heads + h) * nq + i + 1, batch * heads * nq - 1)
    bh = n // nq
    return bh // heads, bh % heads, n % nq


def _mla_attn_kernel(q_ref, k_ref, v_ref, qn_ref, kn_ref, o_ref, vx_sc, sa_sc, sb_sc, m_sc, acc_sc,
                     *, tq, tk):
    @pl.when(pl.program_id(2) == 0)
    def _():
        _extend_v(v_ref, vx_sc, tk)

    rows = lambda j: _chunk_rows(j, tk)
    nk = k_ref.shape[0] // tk
    s_bufs = (sa_sc, sb_sc)

    def consume(j, s_ref):
        _softmax_update(s_ref[...], vx_sc[rows(j), :], m_sc, acc_sc)

    n_tiles = q_ref.shape[0] // tq
    for t in range(n_tiles):
        q_rows = pl.ds(t * tq, tq)
        scores = lambda j, q_rows=q_rows: _qk(q_ref[q_rows, :], k_ref[rows(j), :])
        if t == 0:
            _first_chunk(s_bufs, scores)
        _softmax_init(m_sc, acc_sc)
        _chunk_loop(0, nk - 1, scores, s_bufs, consume)
        if t + 1 < n_tiles:
            next_scores = lambda t=t: _qk(q_ref[pl.ds((t + 1) * tq, tq), :], k_ref[rows(0), :])
        else:
            next_scores = lambda: _qk(qn_ref[...], kn_ref[...])
        _last_chunk(nk, next_scores, s_bufs, consume)
        acc = acc_sc[...]
        o_ref[q_rows, :] = (acc[:, :V_MLA] / acc[:, V_MLA:]).astype(o_ref.dtype)


def _tiles_per_step(nq, nk, max_chunks, max_tiles):
    return next(t for t in (4, 2, 1)
                if t <= max_tiles and nq % t == 0 and (t == 1 or t * nk <= max_chunks))


def mla_attention(q, kv, batch, seq, heads):
    tq = _tile("mla_tq", seq)
    tk = _tile("mla_tk", seq)
    nq, nk = seq // tq, seq // tk
    assert nk % 2 == 0
    per_step = _tiles_per_step(nq, nk, TILES["mla_chunks_per_step"], max_tiles=2)
    steps = nq // per_step
    v_blk0 = heads * MLA_QK_PAD // V_MLA
    nxt = lambda b, h, i: _next_step(b, h, i, batch, heads, steps)

    def q_next(b, h, i):
        b2, h2, i2 = nxt(b, h, i)
        return b2 * nq + i2 * per_step, h2

    def k_next(b, h, i):
        b2, h2, _ = nxt(b, h, i)
        return b2 * nk, h2

    return pl.pallas_call(
        functools.partial(_mla_attn_kernel, tq=tq, tk=tk),
        grid=(batch, heads, steps),
        in_specs=[pl.BlockSpec((per_step * tq, MLA_QK_PAD), lambda b, h, i: (b * steps + i, h)),
                  pl.BlockSpec((seq, MLA_QK_PAD), lambda b, h, i: (b, h)),
                  pl.BlockSpec((seq, V_MLA), lambda b, h, i: (b, v_blk0 + h)),
                  pl.BlockSpec((tq, MLA_QK_PAD), q_next),
                  pl.BlockSpec((tk, MLA_QK_PAD), k_next)],
        out_specs=pl.BlockSpec((per_step * tq, V_MLA), lambda b, h, i: (b * steps + i, h)),
        out_shape=jax.ShapeDtypeStruct((batch * seq, heads * V_MLA), BF16),
        scratch_shapes=[pltpu.VMEM((seq, 2 * V_MLA), BF16),
                        pltpu.VMEM((tq, tk), F32), pltpu.VMEM((tq, tk), F32),
                        pltpu.VMEM((tq, LANE), F32), pltpu.VMEM((tq, 2 * V_MLA), F32)],
        compiler_params=_params("arbitrary", "arbitrary", "arbitrary"),
        name="mla_attention",
    )(q, kv, kv, q, kv)


def _first_biased_chunk(i, tq, tk, nk, n_bias):
    return jnp.minimum(jnp.maximum(i * tq - (MAX_DISTANCE - 1), 0) // tk, nk - n_bias)


def _diff_attn_kernel(tab_ref, far_ref, ids_ref, q_ref, k_ref, v_ref, qn_ref, kn_ref, lq1_ref, lk1_ref, lq2_ref,
                      lk2_ref, g_ref, o_ref, q2_sc, vx_sc, bias_sc, sa_sc, sb_sc, m_sc, acc_sc, *, tq, tk,
                      step, off0, n_bias, lambda_init):
    h, i = pl.program_id(1), pl.program_id(2)
    seq = k_ref.shape[0]
    nk = seq // tk

    @pl.when(i == 0)
    def _():
        _extend_v(v_ref, vx_sc, tk)
        for t in range(bias_sc.shape[0]):
            ids = ids_ref[t]
            f = jnp.zeros(ids.shape, F32)
            for b in range(N_BUCKETS):
                f = jnp.where(ids == b, tab_ref[h, b], f)
            f = f * LOG2E
            bias_sc[t] = pltpu.roll(jnp.broadcast_to(f, (tq, tq + tk)), tk + 1, 1,
                                    stride=1, stride_axis=0)[:, :tk]

    def stacked(q):
        lane = lax.broadcasted_iota(jnp.int32, q.shape, 1)
        zero = jnp.zeros_like(q)
        return jnp.concatenate([jnp.where(lane < DH_DIFF, q, zero),
                                jnp.where(lane >= DH_DIFF, q, zero)], axis=0)

    rows = lambda j: _chunk_rows(j, tk)
    c_left = tab_ref[h, far_ref[0]] * LOG2E
    c_right = tab_ref[h, far_ref[1]] * LOG2E
    lam = (jnp.exp(jnp.sum(lq1_ref[...] * lk1_ref[...], axis=1, keepdims=True))
           - jnp.exp(jnp.sum(lq2_ref[...] * lk2_ref[...], axis=1, keepdims=True)) + lambda_init)
    s_bufs = (sa_sc, sb_sc)

    def tile(t, n_tiles):
        it = i * n_tiles + t
        q_rows = pl.ds(t * tq, tq)
        j0 = _first_biased_chunk(it, tq, tk, nk, n_bias)
        q2_sc[...] = stacked(q_ref[q_rows, :])
        _softmax_init(m_sc, acc_sc)
        scores = lambda j: _qk(q2_sc[...], k_ref[rows(j), :])

        def chunk_at(d):
            j = j0 + d
            return j if d < n_bias else jnp.where(j >= nk, j - nk, j)

        def consume(d, s_ref):
            j = chunk_at(d)
            vx = vx_sc[rows(j), :]
            if d < n_bias:
                bias = bias_sc[(j * tk - it * tq - off0) // step]
                s = s_ref[...] + jnp.concatenate([bias, bias], axis=0)
                _softmax_update(s, vx, m_sc, acc_sc)
            else:
                _softmax_update(s_ref[...], vx, m_sc, acc_sc, jnp.where(j < j0, c_left, c_right))

        if t == 0:
            _first_chunk(s_bufs, lambda _: scores(j0))
        for d in range(nk - 1):
            s_bufs[(d + 1) % 2][...] = scores(chunk_at(d + 1))
            consume(d, s_bufs[d % 2])
        if t + 1 < n_tiles:
            j0_next = _first_biased_chunk(it + 1, tq, tk, nk, n_bias)
            next_scores = lambda: _qk(stacked(q_ref[pl.ds((t + 1) * tq, tq), :]),
                                      k_ref[rows(j0_next), :])
        else:
            next_scores = lambda: _qk(stacked(qn_ref[...]), kn_ref[...])
        _last_chunk(nk, next_scores, s_bufs, consume)

        acc = acc_sc[...]
        o = acc[:, :V_DIFF] / acc[:, V_DIFF:]
        a = o[:tq] - lam * o[tq:]
        o_ref[q_rows, :] = (_rms(a, g_ref[...]) * (1.0 - lambda_init)).astype(o_ref.dtype)

    n_tiles = q_ref.shape[0] // tq
    for t in range(n_tiles):
        tile(t, n_tiles)


def _t5_bucket(rel):
    nb = N_BUCKETS // 2
    max_exact = nb // 2
    ret = (rel > 0).astype(jnp.int32) * nb
    n = jnp.abs(rel)
    nf = jnp.maximum(n, max_exact).astype(F32)
    large = max_exact + (jnp.log(nf / max_exact) / math.log(MAX_DISTANCE / max_exact)
                         * (nb - max_exact)).astype(jnp.int32)
    large = jnp.minimum(large, nb - 1)
    return ret + jnp.where(n < max_exact, n, large)


def diff_attention(qkv, rel_bias, lams, subln_g, lambda_init, batch, seq, heads):
    tq = _tile("diff_tq", seq)
    tk = _tile("diff_tk", seq)
    nq = seq // tq
    step = math.gcd(tq, tk)
    nk = seq // tk
    near_lo, near_hi = -(tk - 1 + MAX_DISTANCE), tq - 1 + MAX_DISTANCE
    n_bias = -(-(near_hi - near_lo - tk) // tk) + 1
    assert nk % 2 == 0 and nk >= n_bias
    off0 = min((near_lo // step + 1) * step, tq - n_bias * tk)
    offs = list(range(off0, max(near_hi - 1, (n_bias - 1) * tk) + 1, step))
    win = (jnp.array(offs, jnp.int32)[:, None] - (tq - 1)
           + jnp.arange(tq + tk, dtype=jnp.int32)[None, :])
    ids = _t5_bucket(win).reshape(len(offs), 1, tq + tk)
    far_ids = _t5_bucket(jnp.array([-MAX_DISTANCE, MAX_DISTANCE], jnp.int32))
    vec = lambda: pl.BlockSpec((1, DH_DIFF), lambda b, h, i: (0, 0))
    kern = functools.partial(_diff_attn_kernel, tq=tq, tk=tk, step=step, off0=off0, n_bias=n_bias,
                             lambda_init=lambda_init)
    per_step = _tiles_per_step(nq, nk, TILES["diff_chunks_per_step"], max_tiles=4)
    steps = nq // per_step
    nxt = lambda b, h, i: _next_step(b, h, i, batch, heads, steps)

    def q_next(b, h, i):
        b2, h2, i2 = nxt(b, h, i)
        return b2 * nq + i2 * per_step, h2

    def k_next(b, h, i):
        b2, h2, i2 = nxt(b, h, i)
        return b2 * nk + _first_biased_chunk(i2 * per_step, tq, tk, nk, n_bias), heads + h2

    return pl.pallas_call(
        kern,
        grid=(batch, heads, steps),
        in_specs=[pl.BlockSpec(memory_space=pltpu.SMEM),
                  pl.BlockSpec(memory_space=pltpu.SMEM),
                  pl.BlockSpec(ids.shape, lambda b, h, i: (0, 0, 0)),
                  pl.BlockSpec((per_step * tq, V_DIFF), lambda b, h, i: (b * steps + i, h)),
                  pl.BlockSpec((seq, V_DIFF), lambda b, h, i: (b, heads + h)),
                  pl.BlockSpec((seq, V_DIFF), lambda b, h, i: (b, 2 * heads + h)),
                  pl.BlockSpec((tq, V_DIFF), q_next),
                  pl.BlockSpec((tk, V_DIFF), k_next),
                  vec(), vec(), vec(), vec(),
                  pl.BlockSpec((1, V_DIFF), lambda b, h, i: (0, 0))],
        out_specs=pl.BlockSpec((per_step * tq, V_DIFF), lambda b, h, i: (b * steps + i, h)),
        out_shape=jax.ShapeDtypeStruct((batch * seq, heads * V_DIFF), BF16),
        scratch_shapes=[pltpu.VMEM((2 * tq, V_DIFF), BF16), pltpu.VMEM((seq, 2 * V_DIFF), BF16),
                        pltpu.VMEM((len(offs), tq, tk), F32),
                        pltpu.VMEM((2 * tq, tk), F32), pltpu.VMEM((2 * tq, tk), F32),
                        pltpu.VMEM((2 * tq, LANE), F32), pltpu.VMEM((2 * tq, 2 * V_DIFF), F32)],
        compiler_params=_params("arbitrary", "arbitrary", "arbitrary"),
        name="diff_attention",
    )(rel_bias.T.astype(F32), far_ids, ids, qkv, qkv, qkv, qkv, qkv,
      *[v.reshape(1, DH_DIFF).astype(F32) for v in lams], subln_g.reshape(1, V_DIFF).astype(F32))


def _ffn_up_kernel(a_ref, prev_ref, next_ref, wg_ref, wu_ref, cwg_ref, cwu_ref, cbg_ref, cbu_ref,
                   o_ref, a_sc, *, tm, tiles_per_seq):
    i = pl.program_id(0)

    @pl.when(pl.program_id(1) == 0)
    def _():
        pos = i % tiles_per_seq
        prev = prev_ref[...]
        nxt = next_ref[...]
        a_sc[:HALO] = jnp.where(pos == 0, jnp.zeros_like(prev), prev)
        a_sc[HALO:HALO + tm] = a_ref[...]
        a_sc[HALO + tm:] = jnp.where(pos == tiles_per_seq - 1, jnp.zeros_like(nxt), nxt)

    def conv(w_ref, cw_ref, cb_ref):
        r = jnp.dot(a_sc[...], w_ref[...], preferred_element_type=F32)
        rows = r.shape[0]
        r_prev = pltpu.roll(r, 1, 0)[HALO:HALO + tm]
        r_next = pltpu.roll(r, rows - 1, 0)[HALO:HALO + tm]
        cw = cw_ref[...]
        return (r_prev * cw[0:1] + r[HALO:HALO + tm] * cw[1:2] + r_next * cw[2:3]) + cb_ref[...]

    gate = conv(wg_ref, cwg_ref, cbg_ref)
    up = conv(wu_ref, cwu_ref, cbu_ref)
    o_ref[...] = (gate * (1.0 / (1.0 + jnp.exp(-gate))) * up).astype(o_ref.dtype)


def ffn_up(h, w_up, conv_w, conv_b, seq):
    m, d = h.shape
    d_ff = w_up.shape[1] // 2
    tm = _tile("ffn_tm", seq)
    tn = _tile("ffn_tn", d_ff)
    nb = d_ff // tn
    hb = tm // HALO
    last_hb = m // HALO - 1
    kern = functools.partial(_ffn_up_kernel, tm=tm, tiles_per_seq=seq // tm)
    return pl.pallas_call(
        kern,
        grid=(m // tm, nb),
        in_specs=[pl.BlockSpec((tm, d), lambda i, j: (i, 0)),
                  pl.BlockSpec((HALO, d), lambda i, j: (jnp.maximum(i * hb - 1, 0), 0)),
                  pl.BlockSpec((HALO, d), lambda i, j: (jnp.minimum((i + 1) * hb, last_hb), 0)),
                  pl.BlockSpec((d, tn), lambda i, j: (0, j)),
                  pl.BlockSpec((d, tn), lambda i, j: (0, j + nb)),
                  pl.BlockSpec((3, tn), lambda i, j: (0, j)),
                  pl.BlockSpec((3, tn), lambda i, j: (0, j + nb)),
                  pl.BlockSpec((1, tn), lambda i, j: (0, j)),
                  pl.BlockSpec((1, tn), lambda i, j: (0, j + nb))],
        out_specs=pl.BlockSpec((tm, tn), lambda i, j: (i, j)),
        out_shape=jax.ShapeDtypeStruct((m, d_ff), BF16),
        scratch_shapes=[pltpu.VMEM((tm + 2 * HALO, d), BF16)],
        compiler_params=_params("parallel", "arbitrary"),
        name="ffn_up_conv_silu",
    )(h, h, h, w_up, w_up, conv_w, conv_w, conv_b, conv_b)


def _prep_w_in(w_in, d_model, heads):
    dq = heads * 2 * DH_DIFF
    dv = heads * V_DIFF
    qkv_cols = 2 * dq + dv
    lat_cols = w_in.shape[1] - qkv_cols - 2 * d_model
    lat_pad = -lat_cols % LANE
    row_scale = jnp.concatenate([jnp.full((dq,), DH_DIFF ** -0.5 * LOG2E, F32),
                                 jnp.ones((w_in.shape[1] - dq,), F32)])
    wt = (jnp.swapaxes(w_in, 0, 1) * row_scale[:, None]).astype(BF16)
    w_qkv = wt[:qkv_cols]
    w_lat = jnp.pad(wt[qkv_cols:qkv_cols + lat_cols], ((0, lat_pad), (0, 0)))
    w_gate = wt[qkv_cols + lat_cols:]
    return w_qkv, w_lat, w_gate


def _prep_w_q_up(w, heads):
    k = w.shape[0]
    w = w.reshape(k, heads, QK_NOPE + QK_ROPE)
    w = jnp.pad(w, ((0, 0), (0, 0), (0, MLA_QK_PAD - QK_NOPE - QK_ROPE)))
    return w.reshape(k, heads * MLA_QK_PAD).astype(BF16)


def _prep_w_kv_up(w, heads):
    kv = w.shape[0]
    w = w.reshape(kv, heads, QK_NOPE + V_MLA)
    top_k = jnp.pad(w[:, :, :QK_NOPE], ((0, 0), (0, 0), (0, MLA_QK_PAD - QK_NOPE)))
    eye = jnp.pad(jnp.eye(QK_ROPE, dtype=F32), ((0, LANE - QK_ROPE), (QK_NOPE, MLA_QK_PAD - QK_NOPE - QK_ROPE)))
    bot_k = jnp.broadcast_to(eye[:, None, :], (LANE, heads, MLA_QK_PAD))
    w_k = jnp.concatenate([top_k, bot_k], axis=0).reshape(kv + LANE, heads * MLA_QK_PAD)
    w_v = jnp.pad(w[:, :, QK_NOPE:].reshape(kv, heads * V_MLA), ((0, LANE), (0, 0)))
    return jnp.concatenate([w_k, w_v], axis=1).astype(BF16)


def _rope_tables(seq):
    pos = jnp.arange(seq, dtype=F32)
    inv_freq = ROPE_THETA ** (-jnp.arange(0, QK_ROPE, 2, dtype=F32) / QK_ROPE)
    ang = pos[:, None] * inv_freq[None, :]
    cos, sin = jnp.cos(ang), jnp.sin(ang)
    half = QK_ROPE // 2
    z = lambda n: jnp.zeros((seq, n), F32)
    c = jnp.concatenate([cos, cos, z(LANE - QK_ROPE)], axis=1)
    s1 = jnp.concatenate([-sin, z(LANE - half)], axis=1)
    s2 = jnp.concatenate([z(half), sin, z(LANE - QK_ROPE)], axis=1)
    return c, s1, s2


def _encoder_layer(x, batch, seq, layer_idx, rel_bias, wts):
    (rms_attn_g, w_qkv, w_lat, w_gate, lams, diff_subln_g, mla_q_norm_g, w_q_up, mla_kv_norm_g,
     w_kv_ext, w_branch_a, w_branch_b, w_out, rms_ffn_g, w_ffn_up, conv_w, conv_b, w_ffn_down) = wts
    m, d_model = x.shape
    heads = w_branch_a.shape[0] // V_DIFF
    q_lora = mla_q_norm_g.shape[0]
    kv_lora = mla_kv_norm_g.shape[0]
    assert q_lora % kv_lora == 0 and (q_lora + kv_lora) % LANE == 0
    lambda_init = 0.8 - 0.6 * math.exp(-0.3 * layer_idx)
    tm = _tile("mm_tm", seq)

    h = rmsnorm(x, rms_attn_g, BF16)
    qkv = matmul_nt(h, w_qkv, BF16, tm, _tile("in_tn", w_qkv.shape[0]), "in_proj_qkv")
    lat = matmul_nt(h, w_lat, F32, _tile("lat_tm", seq), w_lat.shape[0], "in_proj_latent")
    gates = matmul_nt(h, w_gate, F32, tm, _tile("in_tn", d_model), "in_proj_gates")

    a_out = diff_attention(qkv, rel_bias, lams, diff_subln_g, lambda_init, batch, seq, heads)

    rope_tabs = _rope_tables(seq)
    q_mla = mla_q_up(lat, mla_q_norm_g, w_q_up, rope_tabs, seq, q_lora)
    kv_mla = mla_kv_up(lat, mla_kv_norm_g, w_kv_ext, rope_tabs, seq, q_lora, kv_lora)
    b_out = mla_attention(q_mla, kv_mla, batch, seq, heads)

    merged = gated_merge(a_out, b_out, w_branch_a, w_branch_b, gates, tm, _tile("mm_tn", d_model))
    x = matmul_residual(merged, w_out, x, tm, _tile("mm_tn", d_model), "out_proj")

    h2 = rmsnorm(x, rms_ffn_g, BF16)
    act = ffn_up(h2, w_ffn_up, conv_w, conv_b, seq)
    return matmul_residual(act, w_ffn_down, x, _tile("down_tm", seq), _tile("down_tn", d_model),
                           "ffn_down")


def kernel(x_prompt, x_sample, rel_bias, final_norm_g, rms_attn_g, w_in, lambda_q1, lambda_k1,
           lambda_q2, lambda_k2, diff_subln_g, mla_q_norm_g, w_mla_q_up, mla_kv_norm_g,
           w_mla_kv_up, w_branch_a, w_branch_b, w_out, rms_ffn_g, w_ffn_up, conv_w, conv_b,
           w_ffn_down):
    depth = w_in.shape[0]
    d_model = x_prompt.shape[-1]
    heads = w_branch_a.shape[1] // V_DIFF

    layers = []
    for l in range(depth):
        w_qkv, w_lat, w_gate = _prep_w_in(w_in[l], d_model, heads)
        layers.append((
            rms_attn_g[l], w_qkv, w_lat, w_gate,
            (lambda_q1[l], lambda_k1[l], lambda_q2[l], lambda_k2[l]), diff_subln_g[l],
            mla_q_norm_g[l], _prep_w_q_up(w_mla_q_up[l], heads),
            mla_kv_norm_g[l], _prep_w_kv_up(w_mla_kv_up[l], heads),
            w_branch_a[l].astype(BF16), w_branch_b[l].astype(BF16), w_out[l].astype(BF16),
            rms_ffn_g[l], w_ffn_up[l].astype(BF16), conv_w[l].astype(F32),
            conv_b[l].reshape(1, -1).astype(F32), w_ffn_down[l].astype(BF16)))

    def trunk(x):
        batch, seq, _ = x.shape
        y = x.reshape(batch * seq, d_model)
        for l in range(depth):
            y = _encoder_layer(y, batch, seq, l, rel_bias, layers[l])
        return rmsnorm(y, final_norm_g, x.dtype).reshape(x.shape)

    return (trunk(x_prompt), trunk(x_sample))
```

```python
import functools
import math

import jax
import jax.numpy as jnp
from jax import lax
from jax.experimental import pallas as pl
from jax.experimental.pallas import tpu as pltpu

DH_DIFF = 64
V_DIFF = 2 * DH_DIFF
QK_NOPE = 128
QK_ROPE = 64
V_MLA = 128
ROPE_THETA = 10000.0
N_BUCKETS = 32
MAX_DISTANCE = 128
EPS = 1e-6
LOG2E = math.log2(math.e)

LANE = 128
BF16_SUBLANE = 16
VMEM_LIMIT_BYTES = 56 * 1024 * 1024

MLA_QK_PAD = 2 * LANE
HALO = BF16_SUBLANE

TILES = dict(
    norm_tm=512,
    mm_tm=1024, mm_tn=512, in_tn=1024,
    lat_tm=512,
    qup_tm=1024, qup_tn=2048,
    kvup_tm=1024, kvup_tn=2048,
    ffn_tm=1024, ffn_tn=256,
    down_tm=512, down_tn=512,
    diff_tq=512, diff_tk=1024, diff_chunks_per_step=16,
    mla_tq=1024, mla_tk=2048, mla_chunks_per_step=8,
)

F32 = jnp.float32
BF16 = jnp.bfloat16


def _tile(name, dim):
    t = min(TILES[name], dim)
    assert dim % t == 0, (name, dim, t)
    return t


def _params(*sem):
    return pltpu.CompilerParams(dimension_semantics=sem, vmem_limit_bytes=VMEM_LIMIT_BYTES)


def _rms(x, g):
    return x * lax.rsqrt(jnp.mean(x * x, axis=-1, keepdims=True) + EPS) * g


def _rope128(p, c, s1, s2):
    return (p * c + pltpu.roll(p, LANE - QK_ROPE // 2, 1) * s1
            + pltpu.roll(p, QK_ROPE // 2, 1) * s2)


def _rmsnorm_kernel(x_ref, g_ref, o_ref):
    o_ref[...] = _rms(x_ref[...].astype(F32), g_ref[...]).astype(o_ref.dtype)


def rmsnorm(x, g, out_dtype):
    m, d = x.shape
    tm = _tile("norm_tm", m)
    return pl.pallas_call(
        _rmsnorm_kernel,
        grid=(m // tm,),
        in_specs=[pl.BlockSpec((tm, d), lambda i: (i, 0)),
                  pl.BlockSpec((1, d), lambda i: (0, 0))],
        out_specs=pl.BlockSpec((tm, d), lambda i: (i, 0)),
        out_shape=jax.ShapeDtypeStruct((m, d), out_dtype),
        compiler_params=_params("parallel"),
        name="rmsnorm",
    )(x, g.reshape(1, d).astype(F32))


def _mm_nt_kernel(a_ref, bt_ref, o_ref):
    o_ref[...] = _qk(a_ref[...], bt_ref[...]).astype(o_ref.dtype)


def matmul_nt(a, bt, out_dtype, tm, tn, name):
    m, k = a.shape
    n = bt.shape[0]
    return pl.pallas_call(
        _mm_nt_kernel,
        grid=(m // tm, n // tn),
        in_specs=[pl.BlockSpec((tm, k), lambda i, j: (i, 0)),
                  pl.BlockSpec((tn, k), lambda i, j: (j, 0))],
        out_specs=pl.BlockSpec((tm, tn), lambda i, j: (i, j)),
        out_shape=jax.ShapeDtypeStruct((m, n), out_dtype),
        compiler_params=_params("parallel", "arbitrary"),
        name=name,
    )(a, bt)


def _mm_res_kernel(a_ref, b_ref, r_ref, o_ref):
    o_ref[...] = r_ref[...] + jnp.dot(a_ref[...], b_ref[...], preferred_element_type=F32)


def matmul_residual(a, b, res, tm, tn, name):
    m, k = a.shape
    n = b.shape[1]
    return pl.pallas_call(
        _mm_res_kernel,
        grid=(m // tm, n // tn),
        in_specs=[pl.BlockSpec((tm, k), lambda i, j: (i, 0)),
                  pl.BlockSpec((k, tn), lambda i, j: (0, j)),
                  pl.BlockSpec((tm, tn), lambda i, j: (i, j))],
        out_specs=pl.BlockSpec((tm, tn), lambda i, j: (i, j)),
        out_shape=jax.ShapeDtypeStruct((m, n), F32),
        compiler_params=_params("parallel", "arbitrary"),
        name=name,
    )(a, b, res)


def _merge_kernel(a_ref, b_ref, wa_ref, wb_ref, ga_ref, gb_ref, o_ref):
    ya = jnp.dot(a_ref[...], wa_ref[...], preferred_element_type=F32)
    yb = jnp.dot(b_ref[...], wb_ref[...], preferred_element_type=F32)
    sa = 1.0 / (1.0 + jnp.exp(-ga_ref[...]))
    sb = 1.0 / (1.0 + jnp.exp(-gb_ref[...]))
    o_ref[...] = (sa * ya + sb * yb).astype(o_ref.dtype)


def gated_merge(a, b, wa, wb, gates, tm, tn):
    m, ka = a.shape
    kb = b.shape[1]
    n = wa.shape[1]
    nb = n // tn
    return pl.pallas_call(
        _merge_kernel,
        grid=(m // tm, nb),
        in_specs=[pl.BlockSpec((tm, ka), lambda i, j: (i, 0)),
                  pl.BlockSpec((tm, kb), lambda i, j: (i, 0)),
                  pl.BlockSpec((ka, tn), lambda i, j: (0, j)),
                  pl.BlockSpec((kb, tn), lambda i, j: (0, j)),
                  pl.BlockSpec((tm, tn), lambda i, j: (i, j)),
                  pl.BlockSpec((tm, tn), lambda i, j: (i, j + nb))],
        out_specs=pl.BlockSpec((tm, tn), lambda i, j: (i, j)),
        out_shape=jax.ShapeDtypeStruct((m, n), BF16),
        compiler_params=_params("parallel", "arbitrary"),
        name="gated_merge",
    )(a, b, wa, wb, gates, gates)


def _mla_q_kernel(lat_ref, g_ref, w_ref, c_ref, s1_ref, s2_ref, o_ref, a_sc, *, heads, scale):
    @pl.when(pl.program_id(1) == 0)
    def _():
        a_sc[...] = _rms(lat_ref[...], g_ref[...]).astype(BF16)

    acc = jnp.dot(a_sc[...], w_ref[...], preferred_element_type=F32)
    c, s1, s2 = c_ref[...], s1_ref[...], s2_ref[...]
    for h in range(heads):
        lo = h * MLA_QK_PAD
        o_ref[:, lo:lo + QK_NOPE] = (acc[:, lo:lo + QK_NOPE] * scale).astype(o_ref.dtype)
        rot = _rope128(acc[:, lo + QK_NOPE:lo + MLA_QK_PAD], c, s1, s2)
        o_ref[:, lo + QK_NOPE:lo + MLA_QK_PAD] = (rot * scale).astype(o_ref.dtype)


def mla_q_up(lat, g, w, rope_tabs, seq, q_lora):
    m = lat.shape[0]
    n = w.shape[1]
    tm = _tile("qup_tm", seq)
    tn = _tile("qup_tn", n)
    pos_blocks = seq // tm
    tab_spec = pl.BlockSpec((tm, LANE), lambda i, j: (i % pos_blocks, 0))
    kern = functools.partial(_mla_q_kernel, heads=tn // MLA_QK_PAD,
                             scale=(QK_NOPE + QK_ROPE) ** -0.5 * LOG2E)
    return pl.pallas_call(
        kern,
        grid=(m // tm, n // tn),
        in_specs=[pl.BlockSpec((tm, q_lora), lambda i, j: (i, 0)),
                  pl.BlockSpec((1, q_lora), lambda i, j: (0, 0)),
                  pl.BlockSpec((q_lora, tn), lambda i, j: (0, j)),
                  tab_spec, tab_spec, tab_spec],
        out_specs=pl.BlockSpec((tm, tn), lambda i, j: (i, j)),
        out_shape=jax.ShapeDtypeStruct((m, n), BF16),
        scratch_shapes=[pltpu.VMEM((tm, q_lora), BF16)],
        compiler_params=_params("parallel", "arbitrary"),
        name="mla_q_up",
    )(lat, g.reshape(1, q_lora).astype(F32), w, *rope_tabs)


def _mla_kv_kernel(kv_ref, pe_ref, g_ref, w_ref, c_ref, s1_ref, s2_ref, o_ref, a_sc, *, kv_lora):
    @pl.when(pl.program_id(1) == 0)
    def _():
        a_sc[:, :kv_lora] = _rms(kv_ref[...], g_ref[...]).astype(BF16)
        rot = _rope128(pe_ref[...], c_ref[...], s1_ref[...], s2_ref[...])
        a_sc[:, kv_lora:] = rot.astype(BF16)

    o_ref[...] = jnp.dot(a_sc[...], w_ref[...], preferred_element_type=F32).astype(o_ref.dtype)


def mla_kv_up(lat, g, w_ext, rope_tabs, seq, q_lora, kv_lora):
    m = lat.shape[0]
    n = w_ext.shape[1]
    tm = _tile("kvup_tm", seq)
    tn = _tile("kvup_tn", n)
    pos_blocks = seq // tm
    tab_spec = pl.BlockSpec((tm, LANE), lambda i, j: (i % pos_blocks, 0))
    kv_blk = q_lora // kv_lora
    pe_blk = (q_lora + kv_lora) // LANE
    return pl.pallas_call(
        functools.partial(_mla_kv_kernel, kv_lora=kv_lora),
        grid=(m // tm, n // tn),
        in_specs=[pl.BlockSpec((tm, kv_lora), lambda i, j: (i, kv_blk)),
                  pl.BlockSpec((tm, LANE), lambda i, j: (i, pe_blk)),
                  pl.BlockSpec((1, kv_lora), lambda i, j: (0, 0)),
                  pl.BlockSpec((kv_lora + LANE, tn), lambda i, j: (0, j)),
                  tab_spec, tab_spec, tab_spec],
        out_specs=pl.BlockSpec((tm, tn), lambda i, j: (i, j)),
        out_shape=jax.ShapeDtypeStruct((m, n), BF16),
        scratch_shapes=[pltpu.VMEM((tm, kv_lora + LANE), BF16)],
        compiler_params=_params("parallel", "arbitrary"),
        name="mla_kv_up",
    )(lat, lat, g.reshape(1, kv_lora).astype(F32), w_ext, *rope_tabs)


def _lane_tile(x, n):
    return jnp.concatenate([x] * n, axis=1)


def _softmax_init(m_sc, acc_sc):
    m_sc[...] = jnp.full(m_sc.shape, -jnp.inf, F32)
    acc_sc[...] = jnp.zeros(acc_sc.shape, F32)


def _softmax_update(s, vx, m_sc, acc_sc, c=None):
    tk = s.shape[1]
    m_prev = m_sc[...]
    m_cur = jnp.max(s, axis=1, keepdims=True)
    if c is not None:
        m_cur = m_cur + c
    m_new = jnp.maximum(m_prev, m_cur)
    alpha = jnp.exp2(m_prev - m_new)
    shift = m_new if c is None else m_new - c
    p = jnp.exp2(s - _lane_tile(shift, tk // LANE))
    acc_sc[...] = (_lane_tile(alpha, 2) * acc_sc[...]
                   + jnp.dot(p.astype(BF16), vx, preferred_element_type=F32))
    m_sc[...] = m_new


def _extend_v(v_ref, vx_sc, tk):
    width = v_ref.shape[1]

    def body(j, carry):
        r = pl.ds(pl.multiple_of(j * tk, tk), tk)
        vx_sc[r, :width] = v_ref[r, :]
        vx_sc[r, width:] = jnp.ones((tk, vx_sc.shape[1] - width), vx_sc.dtype)
        return carry

    lax.fori_loop(0, v_ref.shape[0] // tk, body, 0)


def _chunk_rows(j, tk):
    start = j * tk
    return pl.ds(start if isinstance(j, int) else pl.multiple_of(start, tk), tk)


def _qk(q, k):
    return lax.dot_general(q, k, (((1,), (1,)), ((), ())), preferred_element_type=F32)


def _chunk_loop(lo, hi, scores, s_bufs, consume):
    def by_parity(j, fn):
        for parity in (0, 1):
            @pl.when((j & 1) == parity)
            def _():
                fn(j, s_bufs[parity], s_bufs[1 - parity])

    def one(j, cur, nxt):
        nxt[...] = scores(j + 1)
        consume(j, cur)

    def two(j, cur, nxt):
        nxt[...] = scores(j + 1)
        consume(j, cur)
        cur[...] = scores(j + 2)
        consume(j + 1, nxt)

    pairs = lax.shift_right_logical(jnp.maximum(hi - lo, 0), 1)

    def body(t, carry):
        by_parity(lo + 2 * t, two)
        return carry

    lax.fori_loop(0, pairs, body, 0)
    rest = lo + 2 * pairs

    @pl.when(rest < hi)
    def _():
        by_parity(rest, one)


def _first_chunk(s_bufs, scores):
    first = functools.reduce(jnp.logical_and, [pl.program_id(a) == 0 for a in range(3)])

    @pl.when(first)
    def _():
        s_bufs[0][...] = scores(0)


def _last_chunk(n_chunks, next_scores, s_bufs, consume):
    s_bufs[0][...] = next_scores()
    consume(n_chunks - 1, s_bufs[1])


def _next_step(b, h, i, batch, heads, nq):
    n = jnp.minimum((b * heads + h) * nq + i + 1, batch * heads * nq - 1)
    bh = n // nq
    return bh // heads, bh % heads, n % nq


def _mla_attn_kernel(q_ref, k_ref, v_ref, qn_ref, kn_ref, o_ref, vx_sc, sa_sc, sb_sc, m_sc, acc_sc,
                     *, tq, tk):
    @pl.when(pl.program_id(2) == 0)
    def _():
        _extend_v(v_ref, vx_sc, tk)

    rows = lambda j: _chunk_rows(j, tk)
    nk = k_ref.shape[0] // tk
    s_bufs = (sa_sc, sb_sc)

    def consume(j, s_ref):
        _softmax_update(s_ref[...], vx_sc[rows(j), :], m_sc, acc_sc)

    n_tiles = q_ref.shape[0] // tq
    for t in range(n_tiles):
        q_rows = pl.ds(t * tq, tq)
        scores = lambda j, q_rows=q_rows: _qk(q_ref[q_rows, :], k_ref[rows(j), :])
        if t == 0:
            _first_chunk(s_bufs, scores)
        _softmax_init(m_sc, acc_sc)
        _chunk_loop(0, nk - 1, scores, s_bufs, consume)
        if t + 1 < n_tiles:
            next_scores = lambda t=t: _qk(q_ref[pl.ds((t + 1) * tq, tq), :], k_ref[rows(0), :])
        else:
            next_scores = lambda: _qk(qn_ref[...], kn_ref[...])
        _last_chunk(nk, next_scores, s_bufs, consume)
        acc = acc_sc[...]
        o_ref[q_rows, :] = (acc[:, :V_MLA] / acc[:, V_MLA:]).astype(o_ref.dtype)


def _tiles_per_step(nq, nk, max_chunks, max_tiles):
    return next(t for t in (4, 2, 1)
                if t <= max_tiles and nq % t == 0 and (t == 1 or t * nk <= max_chunks))


def mla_attention(q, kv, batch, seq, heads):
    tq = _tile("mla_tq", seq)
    tk = _tile("mla_tk", seq)
    nq, nk = seq // tq, seq // tk
    assert nk % 2 == 0
    per_step = _tiles_per_step(nq, nk, TILES["mla_chunks_per_step"], max_tiles=2)
    steps = nq // per_step
    v_blk0 = heads * MLA_QK_PAD // V_MLA
    nxt = lambda b, h, i: _next_step(b, h, i, batch, heads, steps)

    def q_next(b, h, i):
        b2, h2, i2 = nxt(b, h, i)
        return b2 * nq + i2 * per_step, h2

    def k_next(b, h, i):
        b2, h2, _ = nxt(b, h, i)
        return b2 * nk, h2

    return pl.pallas_call(
        functools.partial(_mla_attn_kernel, tq=tq, tk=tk),
        grid=(batch, heads, steps),
        in_specs=[pl.BlockSpec((per_step * tq, MLA_QK_PAD), lambda b, h, i: (b * steps + i, h)),
                  pl.BlockSpec((seq, MLA_QK_PAD), lambda b, h, i: (b, h)),
                  pl.BlockSpec((seq, V_MLA), lambda b, h, i: (b, v_blk0 + h)),
                  pl.BlockSpec((tq, MLA_QK_PAD), q_next),
                  pl.BlockSpec((tk, MLA_QK_PAD), k_next)],
        out_specs=pl.BlockSpec((per_step * tq, V_MLA), lambda b, h, i: (b * steps + i, h)),
        out_shape=jax.ShapeDtypeStruct((batch * seq, heads * V_MLA), BF16),
        scratch_shapes=[pltpu.VMEM((seq, 2 * V_MLA), BF16),
                        pltpu.VMEM((tq, tk), F32), pltpu.VMEM((tq, tk), F32),
                        pltpu.VMEM((tq, LANE), F32), pltpu.VMEM((tq, 2 * V_MLA), F32)],
        compiler_params=_params("arbitrary", "arbitrary", "arbitrary"),
        name="mla_attention",
    )(q, kv, kv, q, kv)


def _first_biased_chunk(i, tq, tk, nk, n_bias):
    return jnp.minimum(jnp.maximum(i * tq - (MAX_DISTANCE - 1), 0) // tk, nk - n_bias)


def _diff_attn_kernel(tab_ref, far_ref, ids_ref, q_ref, k_ref, v_ref, qn_ref, kn_ref, lq1_ref, lk1_ref, lq2_ref,
                      lk2_ref, g_ref, o_ref, q2_sc, vx_sc, bias_sc, sa_sc, sb_sc, m_sc, acc_sc, *, tq, tk,
                      step, off0, n_bias, lambda_init):
    h, i = pl.program_id(1), pl.program_id(2)
    seq = k_ref.shape[0]
    nk = seq // tk

    @pl.when(i == 0)
    def _():
        _extend_v(v_ref, vx_sc, tk)
        for t in range(bias_sc.shape[0]):
            ids = ids_ref[t]
            f = jnp.zeros(ids.shape, F32)
            for b in range(N_BUCKETS):
                f = jnp.where(ids == b, tab_ref[h, b], f)
            f = f * LOG2E
            bias_sc[t] = pltpu.roll(jnp.broadcast_to(f, (tq, tq + tk)), tk + 1, 1,
                                    stride=1, stride_axis=0)[:, :tk]

    def stacked(q):
        lane = lax.broadcasted_iota(jnp.int32, q.shape, 1)
        zero = jnp.zeros_like(q)
        return jnp.concatenate([jnp.where(lane < DH_DIFF, q, zero),
                                jnp.where(lane >= DH_DIFF, q, zero)], axis=0)

    rows = lambda j: _chunk_rows(j, tk)
    c_left = tab_ref[h, far_ref[0]] * LOG2E
    c_right = tab_ref[h, far_ref[1]] * LOG2E
    lam = (jnp.exp(jnp.sum(lq1_ref[...] * lk1_ref[...], axis=1, keepdims=True))
           - jnp.exp(jnp.sum(lq2_ref[...] * lk2_ref[...], axis=1, keepdims=True)) + lambda_init)
    s_bufs = (sa_sc, sb_sc)

    def tile(t, n_tiles):
        it = i * n_tiles + t
        q_rows = pl.ds(t * tq, tq)
        j0 = _first_biased_chunk(it, tq, tk, nk, n_bias)
        q2_sc[...] = stacked(q_ref[q_rows, :])
        _softmax_init(m_sc, acc_sc)
        scores = lambda j: _qk(q2_sc[...], k_ref[rows(j), :])

        def chunk_at(d):
            j = j0 + d
            return j if d < n_bias else jnp.where(j >= nk, j - nk, j)

        def consume(d, s_ref):
            j = chunk_at(d)
            vx = vx_sc[rows(j), :]
            if d < n_bias:
                bias = bias_sc[(j * tk - it * tq - off0) // step]
                s = s_ref[...] + jnp.concatenate([bias, bias], axis=0)
                _softmax_update(s, vx, m_sc, acc_sc)
            else:
                _softmax_update(s_ref[...], vx, m_sc, acc_sc, jnp.where(j < j0, c_left, c_right))

        if t == 0:
            _first_chunk(s_bufs, lambda _: scores(j0))
        for d in range(nk - 1):
            s_bufs[(d + 1) % 2][...] = scores(chunk_at(d + 1))
            consume(d, s_bufs[d % 2])
        if t + 1 < n_tiles:
            j0_next = _first_biased_chunk(it + 1, tq, tk, nk, n_bias)
            next_scores = lambda: _qk(stacked(q_ref[pl.ds((t + 1) * tq, tq), :]),
                                      k_ref[rows(j0_next), :])
        else:
            next_scores = lambda: _qk(stacked(qn_ref[...]), kn_ref[...])
        _last_chunk(nk, next_scores, s_bufs, consume)

        acc = acc_sc[...]
        o = acc[:, :V_DIFF] / acc[:, V_DIFF:]
        a = o[:tq] - lam * o[tq:]
        o_ref[q_rows, :] = (_rms(a, g_ref[...]) * (1.0 - lambda_init)).astype(o_ref.dtype)

    n_tiles = q_ref.shape[0] // tq
    for t in range(n_tiles):
        tile(t, n_tiles)


def _t5_bucket(rel):
    nb = N_BUCKETS // 2
    max_exact = nb // 2
    ret = (rel > 0).astype(jnp.int32) * nb
    n = jnp.abs(rel)
    nf = jnp.maximum(n, max_exact).astype(F32)
    large = max_exact + (jnp.log(nf / max_exact) / math.log(MAX_DISTANCE / max_exact)
                         * (nb - max_exact)).astype(jnp.int32)
    large = jnp.minimum(large, nb - 1)
    return ret + jnp.where(n < max_exact, n, large)


def diff_attention(qkv, rel_bias, lams, subln_g, lambda_init, batch, seq, heads):
    tq = _tile("diff_tq", seq)
    tk = _tile("diff_tk", seq)
    nq = seq // tq
    step = math.gcd(tq, tk)
    nk = seq // tk
    near_lo, near_hi = -(tk - 1 + MAX_DISTANCE), tq - 1 + MAX_DISTANCE
    n_bias = -(-(near_hi - near_lo - tk) // tk) + 1
    assert nk % 2 == 0 and nk >= n_bias
    off0 = min((near_lo // step + 1) * step, tq - n_bias * tk)
    offs = list(range(off0, max(near_hi - 1, (n_bias - 1) * tk) + 1, step))
    win = (jnp.array(offs, jnp.int32)[:, None] - (tq - 1)
           + jnp.arange(tq + tk, dtype=jnp.int32)[None, :])
    ids = _t5_bucket(win).reshape(len(offs), 1, tq + tk)
    far_ids = _t5_bucket(jnp.array([-MAX_DISTANCE, MAX_DISTANCE], jnp.int32))
    vec = lambda: pl.BlockSpec((1, DH_DIFF), lambda b, h, i: (0, 0))
    kern = functools.partial(_diff_attn_kernel, tq=tq, tk=tk, step=step, off0=off0, n_bias=n_bias,
                             lambda_init=lambda_init)
    per_step = _tiles_per_step(nq, nk, TILES["diff_chunks_per_step"], max_tiles=4)
    steps = nq // per_step
    nxt = lambda b, h, i: _next_step(b, h, i, batch, heads, steps)

    def q_next(b, h, i):
        b2, h2, i2 = nxt(b, h, i)
        return b2 * nq + i2 * per_step, h2

    def k_next(b, h, i):
        b2, h2, i2 = nxt(b, h, i)
        return b2 * nk + _first_biased_chunk(i2 * per_step, tq, tk, nk, n_bias), heads + h2

    return pl.pallas_call(
        kern,
        grid=(batch, heads, steps),
        in_specs=[pl.BlockSpec(memory_space=pltpu.SMEM),
                  pl.BlockSpec(memory_space=pltpu.SMEM),
                  pl.BlockSpec(ids.shape, lambda b, h, i: (0, 0, 0)),
                  pl.BlockSpec((per_step * tq, V_DIFF), lambda b, h, i: (b * steps + i, h)),
                  pl.BlockSpec((seq, V_DIFF), lambda b, h, i: (b, heads + h)),
                  pl.BlockSpec((seq, V_DIFF), lambda b, h, i: (b, 2 * heads + h)),
                  pl.BlockSpec((tq, V_DIFF), q_next),
                  pl.BlockSpec((tk, V_DIFF), k_next),
                  vec(), vec(), vec(), vec(),
                  pl.BlockSpec((1, V_DIFF), lambda b, h, i: (0, 0))],
        out_specs=pl.BlockSpec((per_step * tq, V_DIFF), lambda b, h, i: (b * steps + i, h)),
        out_shape=jax.ShapeDtypeStruct((batch * seq, heads * V_DIFF), BF16),
        scratch_shapes=[pltpu.VMEM((2 * tq, V_DIFF), BF16), pltpu.VMEM((seq, 2 * V_DIFF), BF16),
                        pltpu.VMEM((len(offs), tq, tk), F32),
                        pltpu.VMEM((2 * tq, tk), F32), pltpu.VMEM((2 * tq, tk), F32),
                        pltpu.VMEM((2 * tq, LANE), F32), pltpu.VMEM((2 * tq, 2 * V_DIFF), F32)],
        compiler_params=_params("arbitrary", "arbitrary", "arbitrary"),
        name="diff_attention",
    )(rel_bias.T.astype(F32), far_ids, ids, qkv, qkv, qkv, qkv, qkv,
      *[v.reshape(1, DH_DIFF).astype(F32) for v in lams], subln_g.reshape(1, V_DIFF).astype(F32))


def _ffn_up_kernel(a_ref, prev_ref, next_ref, wg_ref, wu_ref, cwg_ref, cwu_ref, cbg_ref, cbu_ref,
                   o_ref, a_sc, *, tm, tiles_per_seq):
    i = pl.program_id(0)

    @pl.when(pl.program_id(1) == 0)
    def _():
        pos = i % tiles_per_seq
        prev = prev_ref[...]
        nxt = next_ref[...]
        a_sc[:HALO] = jnp.where(pos == 0, jnp.zeros_like(prev), prev)
        a_sc[HALO:HALO + tm] = a_ref[...]
        a_sc[HALO + tm:] = jnp.where(pos == tiles_per_seq - 1, jnp.zeros_like(nxt), nxt)

    def conv(w_ref, cw_ref, cb_ref):
        r = jnp.dot(a_sc[...], w_ref[...], preferred_element_type=F32)
        rows = r.shape[0]
        r_prev = pltpu.roll(r, 1, 0)[HALO:HALO + tm]
        r_next = pltpu.roll(r, rows - 1, 0)[HALO:HALO + tm]
        cw = cw_ref[...]
        return (r_prev * cw[0:1] + r[HALO:HALO + tm] * cw[1:2] + r_next * cw[2:3]) + cb_ref[...]

    gate = conv(wg_ref, cwg_ref, cbg_ref)
    up = conv(wu_ref, cwu_ref, cbu_ref)
    o_ref[...] = (gate * (1.0 / (1.0 + jnp.exp(-gate))) * up).astype(o_ref.dtype)


def ffn_up(h, w_up, conv_w, conv_b, seq):
    m, d = h.shape
    d_ff = w_up.shape[1] // 2
    tm = _tile("ffn_tm", seq)
    tn = _tile("ffn_tn", d_ff)
    nb = d_ff // tn
    hb = tm // HALO
    last_hb = m // HALO - 1
    kern = functools.partial(_ffn_up_kernel, tm=tm, tiles_per_seq=seq // tm)
    return pl.pallas_call(
        kern,
        grid=(m // tm, nb),
        in_specs=[pl.BlockSpec((tm, d), lambda i, j: (i, 0)),
                  pl.BlockSpec((HALO, d), lambda i, j: (jnp.maximum(i * hb - 1, 0), 0)),
                  pl.BlockSpec((HALO, d), lambda i, j: (jnp.minimum((i + 1) * hb, last_hb), 0)),
                  pl.BlockSpec((d, tn), lambda i, j: (0, j)),
                  pl.BlockSpec((d, tn), lambda i, j: (0, j + nb)),
                  pl.BlockSpec((3, tn), lambda i, j: (0, j)),
                  pl.BlockSpec((3, tn), lambda i, j: (0, j + nb)),
                  pl.BlockSpec((1, tn), lambda i, j: (0, j)),
                  pl.BlockSpec((1, tn), lambda i, j: (0, j + nb))],
        out_specs=pl.BlockSpec((tm, tn), lambda i, j: (i, j)),
        out_shape=jax.ShapeDtypeStruct((m, d_ff), BF16),
        scratch_shapes=[pltpu.VMEM((tm + 2 * HALO, d), BF16)],
        compiler_params=_params("parallel", "arbitrary"),
        name="ffn_up_conv_silu",
    )(h, h, h, w_up, w_up, conv_w, conv_w, conv_b, conv_b)


def _prep_w_in(w_in, d_model, heads):
    dq = heads * 2 * DH_DIFF
    dv = heads * V_DIFF
    qkv_cols = 2 * dq + dv
    lat_cols = w_in.shape[1] - qkv_cols - 2 * d_model
    lat_pad = -lat_cols % LANE
    row_scale = jnp.concatenate([jnp.full((dq,), DH_DIFF ** -0.5 * LOG2E, F32),
                                 jnp.ones((w_in.shape[1] - dq,), F32)])
    wt = (jnp.swapaxes(w_in, 0, 1) * row_scale[:, None]).astype(BF16)
    w_qkv = wt[:qkv_cols]
    w_lat = jnp.pad(wt[qkv_cols:qkv_cols + lat_cols], ((0, lat_pad), (0, 0)))
    w_gate = wt[qkv_cols + lat_cols:]
    return w_qkv, w_lat, w_gate


def _prep_w_q_up(w, heads):
    k = w.shape[0]
    w = w.reshape(k, heads, QK_NOPE + QK_ROPE)
    w = jnp.pad(w, ((0, 0), (0, 0), (0, MLA_QK_PAD - QK_NOPE - QK_ROPE)))
    return w.reshape(k, heads * MLA_QK_PAD).astype(BF16)


def _prep_w_kv_up(w, heads):
    kv = w.shape[0]
    w = w.reshape(kv, heads, QK_NOPE + V_MLA)
    top_k = jnp.pad(w[:, :, :QK_NOPE], ((0, 0), (0, 0), (0, MLA_QK_PAD - QK_NOPE)))
    eye = jnp.pad(jnp.eye(QK_ROPE, dtype=F32), ((0, LANE - QK_ROPE), (QK_NOPE, MLA_QK_PAD - QK_NOPE - QK_ROPE)))
    bot_k = jnp.broadcast_to(eye[:, None, :], (LANE, heads, MLA_QK_PAD))
    w_k = jnp.concatenate([top_k, bot_k], axis=0).reshape(kv + LANE, heads * MLA_QK_PAD)
    w_v = jnp.pad(w[:, :, QK_NOPE:].reshape(kv, heads * V_MLA), ((0, LANE), (0, 0)))
    return jnp.concatenate([w_k, w_v], axis=1).astype(BF16)


def _rope_tables(seq):
    pos = jnp.arange(seq, dtype=F32)
    inv_freq = ROPE_THETA ** (-jnp.arange(0, QK_ROPE, 2, dtype=F32) / QK_ROPE)
    ang = pos[:, None] * inv_freq[None, :]
    cos, sin = jnp.cos(ang), jnp.sin(ang)
    half = QK_ROPE // 2
    z = lambda n: jnp.zeros((seq, n), F32)
    c = jnp.concatenate([cos, cos, z(LANE - QK_ROPE)], axis=1)
    s1 = jnp.concatenate([-sin, z(LANE - half)], axis=1)
    s2 = jnp.concatenate([z(half), sin, z(LANE - QK_ROPE)], axis=1)
    return c, s1, s2


def _encoder_layer(x, batch, seq, layer_idx, rel_bias, wts):
    (rms_attn_g, w_qkv, w_lat, w_gate, lams, diff_subln_g, mla_q_norm_g, w_q_up, mla_kv_norm_g,
     w_kv_ext, w_branch_a, w_branch_b, w_out, rms_ffn_g, w_ffn_up, conv_w, conv_b, w_ffn_down) = wts
    m, d_model = x.shape
    heads = w_branch_a.shape[0] // V_DIFF
    q_lora = mla_q_norm_g.shape[0]
    kv_lora = mla_kv_norm_g.shape[0]
    assert q_lora % kv_lora == 0 and (q_lora + kv_lora) % LANE == 0
    lambda_init = 0.8 - 0.6 * math.exp(-0.3 * layer_idx)
    tm = _tile("mm_tm", seq)

    h = rmsnorm(x, rms_attn_g, BF16)
    qkv = matmul_nt(h, w_qkv, BF16, tm, _tile("in_tn", w_qkv.shape[0]), "in_proj_qkv")
    lat = matmul_nt(h, w_lat, F32, _tile("lat_tm", seq), w_lat.shape[0], "in_proj_latent")
    gates = matmul_nt(h, w_gate, F32, tm, _tile("in_tn", d_model), "in_proj_gates")

    a_out = diff_attention(qkv, rel_bias, lams, diff_subln_g, lambda_init, batch, seq, heads)

    rope_tabs = _rope_tables(seq)
    q_mla = mla_q_up(lat, mla_q_norm_g, w_q_up, rope_tabs, seq, q_lora)
    kv_mla = mla_kv_up(lat, mla_kv_norm_g, w_kv_ext, rope_tabs, seq, q_lora, kv_lora)
    b_out = mla_attention(q_mla, kv_mla, batch, seq, heads)

    merged = gated_merge(a_out, b_out, w_branch_a, w_branch_b, gates, tm, _tile("mm_tn", d_model))
    x = matmul_residual(merged, w_out, x, tm, _tile("in_tn", d_model), "out_proj")

    h2 = rmsnorm(x, rms_ffn_g, BF16)
    act = ffn_up(h2, w_ffn_up, conv_w, conv_b, seq)
    return matmul_residual(act, w_ffn_down, x, _tile("down_tm", seq), _tile("down_tn", d_model),
                           "ffn_down")


def kernel(x_prompt, x_sample, rel_bias, final_norm_g, rms_attn_g, w_in, lambda_q1, lambda_k1,
           lambda_q2, lambda_k2, diff_subln_g, mla_q_norm_g, w_mla_q_up, mla_kv_norm_g,
           w_mla_kv_up, w_branch_a, w_branch_b, w_out, rms_ffn_g, w_ffn_up, conv_w, conv_b,
           w_ffn_down):
    depth = w_in.shape[0]
    d_model = x_prompt.shape[-1]
    heads = w_branch_a.shape[1] // V_DIFF

    layers = []
    for l in range(depth):
        w_qkv, w_lat, w_gate = _prep_w_in(w_in[l], d_model, heads)
        layers.append((
            rms_attn_g[l], w_qkv, w_lat, w_gate,
            (lambda_q1[l], lambda_k1[l], lambda_q2[l], lambda_k2[l]), diff_subln_g[l],
            mla_q_norm_g[l], _prep_w_q_up(w_mla_q_up[l], heads),
            mla_kv_norm_g[l], _prep_w_kv_up(w_mla_kv_up[l], heads),
            w_branch_a[l].astype(BF16), w_branch_b[l].astype(BF16), w_out[l].astype(BF16),
            rms_ffn_g[l], w_ffn_up[l].astype(BF16), conv_w[l].astype(F32),
            conv_b[l].reshape(1, -1).astype(F32), w_ffn_down[l].astype(BF16)))

    def trunk(x):
        batch, seq, _ = x.shape
        y = x.reshape(batch * seq, d_model)
        for l in range(depth):
            y = _encoder_layer(y, batch, seq, l, rel_bias, layers[l])
        return rmsnorm(y, final_norm_g, x.dtype).reshape(x.shape)

    return (trunk(x_prompt), trunk(x_sample))
```

```python
import functools
import math

import jax
import jax.numpy as jnp
from jax import lax
from jax.experimental import pallas as pl
from jax.experimental.pallas import tpu as pltpu

DH_DIFF = 64
V_DIFF = 2 * DH_DIFF
QK_NOPE = 128
QK_ROPE = 64
V_MLA = 128
ROPE_THETA = 10000.0
N_BUCKETS = 32
MAX_DISTANCE = 128
EPS = 1e-6
LOG2E = math.log2(math.e)

LANE = 128
BF16_SUBLANE = 16
VMEM_LIMIT_BYTES = 56 * 1024 * 1024
MERGE_VMEM_LIMIT_BYTES = 61 * 1024 * 1024

MLA_QK_PAD = 2 * LANE
HALO = BF16_SUBLANE

TILES = dict(
    norm_tm=512,
    mm_tm=1024, mm_tn=512, in_tn=1024,
    lat_tm=512,
    qup_tm=1024, qup_tn=2048,
    kvup_tm=1024, kvup_tn=2048,
    ffn_tm=1024, ffn_tn=256,
    down_tm=512, down_tn=512,
    diff_tq=512, diff_tk=1024, diff_chunks_per_step=16,
    mla_tq=1024, mla_tk=2048, mla_chunks_per_step=8,
)

F32 = jnp.float32
BF16 = jnp.bfloat16


def _tile(name, dim):
    t = min(TILES[name], dim)
    assert dim % t == 0, (name, dim, t)
    return t


def _params(*sem, vmem_limit_bytes=VMEM_LIMIT_BYTES):
    return pltpu.CompilerParams(dimension_semantics=sem, vmem_limit_bytes=vmem_limit_bytes)


def _rms(x, g):
    return x * lax.rsqrt(jnp.mean(x * x, axis=-1, keepdims=True) + EPS) * g


def _rope128(p, c, s1, s2):
    return (p * c + pltpu.roll(p, LANE - QK_ROPE // 2, 1) * s1
            + pltpu.roll(p, QK_ROPE // 2, 1) * s2)


def _rmsnorm_kernel(x_ref, g_ref, o_ref):
    o_ref[...] = _rms(x_ref[...].astype(F32), g_ref[...]).astype(o_ref.dtype)


def rmsnorm(x, g, out_dtype):
    m, d = x.shape
    tm = _tile("norm_tm", m)
    return pl.pallas_call(
        _rmsnorm_kernel,
        grid=(m // tm,),
        in_specs=[pl.BlockSpec((tm, d), lambda i: (i, 0)),
                  pl.BlockSpec((1, d), lambda i: (0, 0))],
        out_specs=pl.BlockSpec((tm, d), lambda i: (i, 0)),
        out_shape=jax.ShapeDtypeStruct((m, d), out_dtype),
        compiler_params=_params("parallel"),
        name="rmsnorm",
    )(x, g.reshape(1, d).astype(F32))


def _mm_nt_kernel(a_ref, bt_ref, o_ref):
    o_ref[...] = _qk(a_ref[...], bt_ref[...]).astype(o_ref.dtype)


def matmul_nt(a, bt, out_dtype, tm, tn, name):
    m, k = a.shape
    n = bt.shape[0]
    return pl.pallas_call(
        _mm_nt_kernel,
        grid=(m // tm, n // tn),
        in_specs=[pl.BlockSpec((tm, k), lambda i, j: (i, 0)),
                  pl.BlockSpec((tn, k), lambda i, j: (j, 0))],
        out_specs=pl.BlockSpec((tm, tn), lambda i, j: (i, j)),
        out_shape=jax.ShapeDtypeStruct((m, n), out_dtype),
        compiler_params=_params("parallel", "arbitrary"),
        name=name,
    )(a, bt)


def _mm_res_kernel(a_ref, b_ref, r_ref, o_ref):
    o_ref[...] = r_ref[...] + jnp.dot(a_ref[...], b_ref[...], preferred_element_type=F32)


def matmul_residual(a, b, res, tm, tn, name):
    m, k = a.shape
    n = b.shape[1]
    return pl.pallas_call(
        _mm_res_kernel,
        grid=(m // tm, n // tn),
        in_specs=[pl.BlockSpec((tm, k), lambda i, j: (i, 0)),
                  pl.BlockSpec((k, tn), lambda i, j: (0, j)),
                  pl.BlockSpec((tm, tn), lambda i, j: (i, j))],
        out_specs=pl.BlockSpec((tm, tn), lambda i, j: (i, j)),
        out_shape=jax.ShapeDtypeStruct((m, n), F32),
        compiler_params=_params("parallel", "arbitrary"),
        name=name,
    )(a, b, res)


def _merge_kernel(a_ref, b_ref, wa_ref, wb_ref, ga_ref, gb_ref, o_ref):
    ya = jnp.dot(a_ref[...], wa_ref[...], preferred_element_type=F32)
    yb = jnp.dot(b_ref[...], wb_ref[...], preferred_element_type=F32)
    sa = 1.0 / (1.0 + jnp.exp(-ga_ref[...]))
    sb = 1.0 / (1.0 + jnp.exp(-gb_ref[...]))
    o_ref[...] = (sa * ya + sb * yb).astype(o_ref.dtype)


def gated_merge(a, b, wa, wb, gates, tm, tn):
    m, ka = a.shape
    kb = b.shape[1]
    n = wa.shape[1]
    nb = n // tn
    return pl.pallas_call(
        _merge_kernel,
        grid=(m // tm, nb),
        in_specs=[pl.BlockSpec((tm, ka), lambda i, j: (i, 0)),
                  pl.BlockSpec((tm, kb), lambda i, j: (i, 0)),
                  pl.BlockSpec((ka, tn), lambda i, j: (0, j)),
                  pl.BlockSpec((kb, tn), lambda i, j: (0, j)),
                  pl.BlockSpec((tm, tn), lambda i, j: (i, j)),
                  pl.BlockSpec((tm, tn), lambda i, j: (i, j + nb))],
        out_specs=pl.BlockSpec((tm, tn), lambda i, j: (i, j)),
        out_shape=jax.ShapeDtypeStruct((m, n), BF16),
        compiler_params=_params("parallel", "arbitrary", vmem_limit_bytes=MERGE_VMEM_LIMIT_BYTES),
        name="gated_merge",
    )(a, b, wa, wb, gates, gates)


def _mla_q_kernel(lat_ref, g_ref, w_ref, c_ref, s1_ref, s2_ref, o_ref, a_sc, *, heads, scale):
    @pl.when(pl.program_id(1) == 0)
    def _():
        a_sc[...] = _rms(lat_ref[...], g_ref[...]).astype(BF16)

    acc = jnp.dot(a_sc[...], w_ref[...], preferred_element_type=F32)
    c, s1, s2 = c_ref[...], s1_ref[...], s2_ref[...]
    for h in range(heads):
        lo = h * MLA_QK_PAD
        o_ref[:, lo:lo + QK_NOPE] = (acc[:, lo:lo + QK_NOPE] * scale).astype(o_ref.dtype)
        rot = _rope128(acc[:, lo + QK_NOPE:lo + MLA_QK_PAD], c, s1, s2)
        o_ref[:, lo + QK_NOPE:lo + MLA_QK_PAD] = (rot * scale).astype(o_ref.dtype)


def mla_q_up(lat, g, w, rope_tabs, seq, q_lora):
    m = lat.shape[0]
    n = w.shape[1]
    tm = _tile("qup_tm", seq)
    tn = _tile("qup_tn", n)
    pos_blocks = seq // tm
    tab_spec = pl.BlockSpec((tm, LANE), lambda i, j: (i % pos_blocks, 0))
    kern = functools.partial(_mla_q_kernel, heads=tn // MLA_QK_PAD,
                             scale=(QK_NOPE + QK_ROPE) ** -0.5 * LOG2E)
    return pl.pallas_call(
        kern,
        grid=(m // tm, n // tn),
        in_specs=[pl.BlockSpec((tm, q_lora), lambda i, j: (i, 0)),
                  pl.BlockSpec((1, q_lora), lambda i, j: (0, 0)),
                  pl.BlockSpec((q_lora, tn), lambda i, j: (0, j)),
                  tab_spec, tab_spec, tab_spec],
        out_specs=pl.BlockSpec((tm, tn), lambda i, j: (i, j)),
        out_shape=jax.ShapeDtypeStruct((m, n), BF16),
        scratch_shapes=[pltpu.VMEM((tm, q_lora), BF16)],
        compiler_params=_params("parallel", "arbitrary"),
        name="mla_q_up",
    )(lat, g.reshape(1, q_lora).astype(F32), w, *rope_tabs)


def _mla_kv_kernel(kv_ref, pe_ref, g_ref, w_ref, c_ref, s1_ref, s2_ref, o_ref, a_sc, *, kv_lora):
    @pl.when(pl.program_id(1) == 0)
    def _():
        a_sc[:, :kv_lora] = _rms(kv_ref[...], g_ref[...]).astype(BF16)
        rot = _rope128(pe_ref[...], c_ref[...], s1_ref[...], s2_ref[...])
        a_sc[:, kv_lora:] = rot.astype(BF16)

    o_ref[...] = jnp.dot(a_sc[...], w_ref[...], preferred_element_type=F32).astype(o_ref.dtype)


def mla_kv_up(lat, g, w_ext, rope_tabs, seq, q_lora, kv_lora):
    m = lat.shape[0]
    n = w_ext.shape[1]
    tm = _tile("kvup_tm", seq)
    tn = _tile("kvup_tn", n)
    pos_blocks = seq // tm
    tab_spec = pl.BlockSpec((tm, LANE), lambda i, j: (i % pos_blocks, 0))
    kv_blk = q_lora // kv_lora
    pe_blk = (q_lora + kv_lora) // LANE
    return pl.pallas_call(
        functools.partial(_mla_kv_kernel, kv_lora=kv_lora),
        grid=(m // tm, n // tn),
        in_specs=[pl.BlockSpec((tm, kv_lora), lambda i, j: (i, kv_blk)),
                  pl.BlockSpec((tm, LANE), lambda i, j: (i, pe_blk)),
                  pl.BlockSpec((1, kv_lora), lambda i, j: (0, 0)),
                  pl.BlockSpec((kv_lora + LANE, tn), lambda i, j: (0, j)),
                  tab_spec, tab_spec, tab_spec],
        out_specs=pl.BlockSpec((tm, tn), lambda i, j: (i, j)),
        out_shape=jax.ShapeDtypeStruct((m, n), BF16),
        scratch_shapes=[pltpu.VMEM((tm, kv_lora + LANE), BF16)],
        compiler_params=_params("parallel", "arbitrary"),
        name="mla_kv_up",
    )(lat, lat, g.reshape(1, kv_lora).astype(F32), w_ext, *rope_tabs)


def _lane_tile(x, n):
    return jnp.concatenate([x] * n, axis=1)


def _softmax_init(m_sc, acc_sc):
    m_sc[...] = jnp.full(m_sc.shape, -jnp.inf, F32)
    acc_sc[...] = jnp.zeros(acc_sc.shape, F32)


def _softmax_update(s, vx, m_sc, acc_sc, c=None):
    tk = s.shape[1]
    m_prev = m_sc[...]
    m_cur = jnp.max(s, axis=1, keepdims=True)
    if c is not None:
        m_cur = m_cur + c
    m_new = jnp.maximum(m_prev, m_cur)
    alpha = jnp.exp2(m_prev - m_new)
    shift = m_new if c is None else m_new - c
    p = jnp.exp2(s - _lane_tile(shift, tk // LANE))
    acc_sc[...] = (_lane_tile(alpha, 2) * acc_sc[...]
                   + jnp.dot(p.astype(BF16), vx, preferred_element_type=F32))
    m_sc[...] = m_new


def _extend_v(v_ref, vx_sc, tk):
    width = v_ref.shape[1]

    def body(j, carry):
        r = pl.ds(pl.multiple_of(j * tk, tk), tk)
        vx_sc[r, :width] = v_ref[r, :]
        vx_sc[r, width:] = jnp.ones((tk, vx_sc.shape[1] - width), vx_sc.dtype)
        return carry

    lax.fori_loop(0, v_ref.shape[0] // tk, body, 0)


def _chunk_rows(j, tk):
    start = j * tk
    return pl.ds(start if isinstance(j, int) else pl.multiple_of(start, tk), tk)


def _qk(q, k):
    return lax.dot_general(q, k, (((1,), (1,)), ((), ())), preferred_element_type=F32)


def _chunk_loop(lo, hi, scores, s_bufs, consume):
    def by_parity(j, fn):
        for parity in (0, 1):
            @pl.when((j & 1) == parity)
            def _():
                fn(j, s_bufs[parity], s_bufs[1 - parity])

    def one(j, cur, nxt):
        nxt[...] = scores(j + 1)
        consume(j, cur)

    def two(j, cur, nxt):
        nxt[...] = scores(j + 1)
        consume(j, cur)
        cur[...] = scores(j + 2)
        consume(j + 1, nxt)

    pairs = lax.shift_right_logical(jnp.maximum(hi - lo, 0), 1)

    def body(t, carry):
        by_parity(lo + 2 * t, two)
        return carry

    lax.fori_loop(0, pairs, body, 0)
    rest = lo + 2 * pairs

    @pl.when(rest < hi)
    def _():
        by_parity(rest, one)


def _first_chunk(s_bufs, scores):
    first = functools.reduce(jnp.logical_and, [pl.program_id(a) == 0 for a in range(3)])

    @pl.when(first)
    def _():
        s_bufs[0][...] = scores(0)


def _last_chunk(n_chunks, next_scores, s_bufs, consume):
    s_bufs[0][...] = next_scores()
    consume(n_chunks - 1, s_bufs[1])


def _next_step(b, h, i, batch, heads, nq):
    n = jnp.minimum((b * heads + h) * nq + i + 1, batch * heads * nq - 1)
    bh = n // nq
    return bh // heads, bh % heads, n % nq


def _mla_attn_kernel(q_ref, k_ref, v_ref, qn_ref, kn_ref, o_ref, vx_sc, sa_sc, sb_sc, m_sc, acc_sc,
                     *, tq, tk):
    @pl.when(pl.program_id(2) == 0)
    def _():
        _extend_v(v_ref, vx_sc, tk)

    rows = lambda j: _chunk_rows(j, tk)
    nk = k_ref.shape[0] // tk
    s_bufs = (sa_sc, sb_sc)

    def consume(j, s_ref):
        _softmax_update(s_ref[...], vx_sc[rows(j), :], m_sc, acc_sc)

    n_tiles = q_ref.shape[0] // tq
    for t in range(n_tiles):
        q_rows = pl.ds(t * tq, tq)
        scores = lambda j, q_rows=q_rows: _qk(q_ref[q_rows, :], k_ref[rows(j), :])
        if t == 0:
            _first_chunk(s_bufs, scores)
        _softmax_init(m_sc, acc_sc)
        _chunk_loop(0, nk - 1, scores, s_bufs, consume)
        if t + 1 < n_tiles:
            next_scores = lambda t=t: _qk(q_ref[pl.ds((t + 1) * tq, tq), :], k_ref[rows(0), :])
        else:
            next_scores = lambda: _qk(qn_ref[...], kn_ref[...])
        _last_chunk(nk, next_scores, s_bufs, consume)
        acc = acc_sc[...]
        o_ref[q_rows, :] = (acc[:, :V_MLA] / acc[:, V_MLA:]).astype(o_ref.dtype)


def _tiles_per_step(nq, nk, max_chunks, max_tiles):
    return next(t for t in (4, 2, 1)
                if t <= max_tiles and nq % t == 0 and (t == 1 or t * nk <= max_chunks))


def mla_attention(q, kv, batch, seq, heads):
    tq = _tile("mla_tq", seq)
    tk = _tile("mla_tk", seq)
    nq, nk = seq // tq, seq // tk
    assert nk % 2 == 0
    per_step = _tiles_per_step(nq, nk, TILES["mla_chunks_per_step"], max_tiles=2)
    steps = nq // per_step
    v_blk0 = heads * MLA_QK_PAD // V_MLA
    nxt = lambda b, h, i: _next_step(b, h, i, batch, heads, steps)

    def q_next(b, h, i):
        b2, h2, i2 = nxt(b, h, i)
        return b2 * nq + i2 * per_step, h2

    def k_next(b, h, i):
        b2, h2, _ = nxt(b, h, i)
        return b2 * nk, h2

    return pl.pallas_call(
        functools.partial(_mla_attn_kernel, tq=tq, tk=tk),
        grid=(batch, heads, steps),
        in_specs=[pl.BlockSpec((per_step * tq, MLA_QK_PAD), lambda b, h, i: (b * steps + i, h)),
                  pl.BlockSpec((seq, MLA_QK_PAD), lambda b, h, i: (b, h)),
                  pl.BlockSpec((seq, V_MLA), lambda b, h, i: (b, v_blk0 + h)),
                  pl.BlockSpec((tq, MLA_QK_PAD), q_next),
                  pl.BlockSpec((tk, MLA_QK_PAD), k_next)],
        out_specs=pl.BlockSpec((per_step * tq, V_MLA), lambda b, h, i: (b * steps + i, h)),
        out_shape=jax.ShapeDtypeStruct((batch * seq, heads * V_MLA), BF16),
        scratch_shapes=[pltpu.VMEM((seq, 2 * V_MLA), BF16),
                        pltpu.VMEM((tq, tk), F32), pltpu.VMEM((tq, tk), F32),
                        pltpu.VMEM((tq, LANE), F32), pltpu.VMEM((tq, 2 * V_MLA), F32)],
        compiler_params=_params("arbitrary", "arbitrary", "arbitrary"),
        name="mla_attention",
    )(q, kv, kv, q, kv)


def _first_biased_chunk(i, tq, tk, nk, n_bias):
    return jnp.minimum(jnp.maximum(i * tq - (MAX_DISTANCE - 1), 0) // tk, nk - n_bias)


def _diff_attn_kernel(tab_ref, far_ref, ids_ref, q_ref, k_ref, v_ref, qn_ref, kn_ref, lq1_ref, lk1_ref, lq2_ref,
                      lk2_ref, g_ref, o_ref, q2_sc, vx_sc, bias_sc, sa_sc, sb_sc, m_sc, acc_sc, *, tq, tk,
                      step, off0, n_bias, lambda_init):
    h, i = pl.program_id(1), pl.program_id(2)
    seq = k_ref.shape[0]
    nk = seq // tk

    @pl.when(i == 0)
    def _():
        _extend_v(v_ref, vx_sc, tk)
        for t in range(bias_sc.shape[0]):
            ids = ids_ref[t]
            f = jnp.zeros(ids.shape, F32)
            for b in range(N_BUCKETS):
                f = jnp.where(ids == b, tab_ref[h, b], f)
            f = f * LOG2E
            bias_sc[t] = pltpu.roll(jnp.broadcast_to(f, (tq, tq + tk)), tk + 1, 1,
                                    stride=1, stride_axis=0)[:, :tk]

    def stacked(q):
        lane = lax.broadcasted_iota(jnp.int32, q.shape, 1)
        zero = jnp.zeros_like(q)
        return jnp.concatenate([jnp.where(lane < DH_DIFF, q, zero),
                                jnp.where(lane >= DH_DIFF, q, zero)], axis=0)

    rows = lambda j: _chunk_rows(j, tk)
    c_left = tab_ref[h, far_ref[0]] * LOG2E
    c_right = tab_ref[h, far_ref[1]] * LOG2E
    lam = (jnp.exp(jnp.sum(lq1_ref[...] * lk1_ref[...], axis=1, keepdims=True))
           - jnp.exp(jnp.sum(lq2_ref[...] * lk2_ref[...], axis=1, keepdims=True)) + lambda_init)
    s_bufs = (sa_sc, sb_sc)

    def tile(t, n_tiles):
        it = i * n_tiles + t
        q_rows = pl.ds(t * tq, tq)
        j0 = _first_biased_chunk(it, tq, tk, nk, n_bias)
        q2_sc[...] = stacked(q_ref[q_rows, :])
        _softmax_init(m_sc, acc_sc)
        scores = lambda j: _qk(q2_sc[...], k_ref[rows(j), :])

        def chunk_at(d):
            j = j0 + d
            return j if d < n_bias else jnp.where(j >= nk, j - nk, j)

        def consume(d, s_ref):
            j = chunk_at(d)
            vx = vx_sc[rows(j), :]
            if d < n_bias:
                bias = bias_sc[(j * tk - it * tq - off0) // step]
                s = s_ref[...] + jnp.concatenate([bias, bias], axis=0)
                _softmax_update(s, vx, m_sc, acc_sc)
            else:
                _softmax_update(s_ref[...], vx, m_sc, acc_sc, jnp.where(j < j0, c_left, c_right))

        if t == 0:
            _first_chunk(s_bufs, lambda _: scores(j0))
        for d in range(nk - 1):
            s_bufs[(d + 1) % 2][...] = scores(chunk_at(d + 1))
            consume(d, s_bufs[d % 2])
        if t + 1 < n_tiles:
            j0_next = _first_biased_chunk(it + 1, tq, tk, nk, n_bias)
            next_scores = lambda: _qk(stacked(q_ref[pl.ds((t + 1) * tq, tq), :]),
                                      k_ref[rows(j0_next), :])
        else:
            next_scores = lambda: _qk(stacked(qn_ref[...]), kn_ref[...])
        _last_chunk(nk, next_scores, s_bufs, consume)

        acc = acc_sc[...]
        o = acc[:, :V_DIFF] / acc[:, V_DIFF:]
        a = o[:tq] - lam * o[tq:]
        o_ref[q_rows, :] = (_rms(a, g_ref[...]) * (1.0 - lambda_init)).astype(o_ref.dtype)

    n_tiles = q_ref.shape[0] // tq
    for t in range(n_tiles):
        tile(t, n_tiles)


def _t5_bucket(rel):
    nb = N_BUCKETS // 2
    max_exact = nb // 2
    ret = (rel > 0).astype(jnp.int32) * nb
    n = jnp.abs(rel)
    nf = jnp.maximum(n, max_exact).astype(F32)
    large = max_exact + (jnp.log(nf / max_exact) / math.log(MAX_DISTANCE / max_exact)
                         * (nb - max_exact)).astype(jnp.int32)
    large = jnp.minimum(large, nb - 1)
    return ret + jnp.where(n < max_exact, n, large)


def diff_attention(qkv, rel_bias, lams, subln_g, lambda_init, batch, seq, heads):
    tq = _tile("diff_tq", seq)
    tk = _tile("diff_tk", seq)
    nq = seq // tq
    step = math.gcd(tq, tk)
    nk = seq // tk
    near_lo, near_hi = -(tk - 1 + MAX_DISTANCE), tq - 1 + MAX_DISTANCE
    n_bias = -(-(near_hi - near_lo - tk) // tk) + 1
    assert nk % 2 == 0 and nk >= n_bias
    off0 = min((near_lo // step + 1) * step, tq - n_bias * tk)
    offs = list(range(off0, max(near_hi - 1, (n_bias - 1) * tk) + 1, step))
    win = (jnp.array(offs, jnp.int32)[:, None] - (tq - 1)
           + jnp.arange(tq + tk, dtype=jnp.int32)[None, :])
    ids = _t5_bucket(win).reshape(len(offs), 1, tq + tk)
    far_ids = _t5_bucket(jnp.array([-MAX_DISTANCE, MAX_DISTANCE], jnp.int32))
    vec = lambda: pl.BlockSpec((1, DH_DIFF), lambda b, h, i: (0, 0))
    kern = functools.partial(_diff_attn_kernel, tq=tq, tk=tk, step=step, off0=off0, n_bias=n_bias,
                             lambda_init=lambda_init)
    per_step = _tiles_per_step(nq, nk, TILES["diff_chunks_per_step"], max_tiles=4)
    steps = nq // per_step
    nxt = lambda b, h, i: _next_step(b, h, i, batch, heads, steps)

    def q_next(b, h, i):
        b2, h2, i2 = nxt(b, h, i)
        return b2 * nq + i2 * per_step, h2

    def k_next(b, h, i):
        b2, h2, i2 = nxt(b, h, i)
        return b2 * nk + _first_biased_chunk(i2 * per_step, tq, tk, nk, n_bias), heads + h2

    return pl.pallas_call(
        kern,
        grid=(batch, heads, steps),
        in_specs=[pl.BlockSpec(memory_space=pltpu.SMEM),
                  pl.BlockSpec(memory_space=pltpu.SMEM),
                  pl.BlockSpec(ids.shape, lambda b, h, i: (0, 0, 0)),
                  pl.BlockSpec((per_step * tq, V_DIFF), lambda b, h, i: (b * steps + i, h)),
                  pl.BlockSpec((seq, V_DIFF), lambda b, h, i: (b, heads + h)),
                  pl.BlockSpec((seq, V_DIFF), lambda b, h, i: (b, 2 * heads + h)),
                  pl.BlockSpec((tq, V_DIFF), q_next),
                  pl.BlockSpec((tk, V_DIFF), k_next),
                  vec(), vec(), vec(), vec(),
                  pl.BlockSpec((1, V_DIFF), lambda b, h, i: (0, 0))],
        out_specs=pl.BlockSpec((per_step * tq, V_DIFF), lambda b, h, i: (b * steps + i, h)),
        out_shape=jax.ShapeDtypeStruct((batch * seq, heads * V_DIFF), BF16),
        scratch_shapes=[pltpu.VMEM((2 * tq, V_DIFF), BF16), pltpu.VMEM((seq, 2 * V_DIFF), BF16),
                        pltpu.VMEM((len(offs), tq, tk), F32),
                        pltpu.VMEM((2 * tq, tk), F32), pltpu.VMEM((2 * tq, tk), F32),
                        pltpu.VMEM((2 * tq, LANE), F32), pltpu.VMEM((2 * tq, 2 * V_DIFF), F32)],
        compiler_params=_params("arbitrary", "arbitrary", "arbitrary"),
        name="diff_attention",
    )(rel_bias.T.astype(F32), far_ids, ids, qkv, qkv, qkv, qkv, qkv,
      *[v.reshape(1, DH_DIFF).astype(F32) for v in lams], subln_g.reshape(1, V_DIFF).astype(F32))


def _ffn_up_kernel(a_ref, prev_ref, next_ref, wg_ref, wu_ref, cwg_ref, cwu_ref, cbg_ref, cbu_ref,
                   o_ref, a_sc, *, tm, tiles_per_seq):
    i = pl.program_id(0)

    @pl.when(pl.program_id(1) == 0)
    def _():
        pos = i % tiles_per_seq
        prev = prev_ref[...]
        nxt = next_ref[...]
        a_sc[:HALO] = jnp.where(pos == 0, jnp.zeros_like(prev), prev)
        a_sc[HALO:HALO + tm] = a_ref[...]
        a_sc[HALO + tm:] = jnp.where(pos == tiles_per_seq - 1, jnp.zeros_like(nxt), nxt)

    def conv(w_ref, cw_ref, cb_ref):
        r = jnp.dot(a_sc[...], w_ref[...], preferred_element_type=F32)
        rows = r.shape[0]
        r_prev = pltpu.roll(r, 1, 0)[HALO:HALO + tm]
        r_next = pltpu.roll(r, rows - 1, 0)[HALO:HALO + tm]
        cw = cw_ref[...]
        return (r_prev * cw[0:1] + r[HALO:HALO + tm] * cw[1:2] + r_next * cw[2:3]) + cb_ref[...]

    gate = conv(wg_ref, cwg_ref, cbg_ref)
    up = conv(wu_ref, cwu_ref, cbu_ref)
    o_ref[...] = (gate * (1.0 / (1.0 + jnp.exp(-gate))) * up).astype(o_ref.dtype)


def ffn_up(h, w_up, conv_w, conv_b, seq):
    m, d = h.shape
    d_ff = w_up.shape[1] // 2
    tm = _tile("ffn_tm", seq)
    tn = _tile("ffn_tn", d_ff)
    nb = d_ff // tn
    hb = tm // HALO
    last_hb = m // HALO - 1
    kern = functools.partial(_ffn_up_kernel, tm=tm, tiles_per_seq=seq // tm)
    return pl.pallas_call(
        kern,
        grid=(m // tm, nb),
        in_specs=[pl.BlockSpec((tm, d), lambda i, j: (i, 0)),
                  pl.BlockSpec((HALO, d), lambda i, j: (jnp.maximum(i * hb - 1, 0), 0)),
                  pl.BlockSpec((HALO, d), lambda i, j: (jnp.minimum((i + 1) * hb, last_hb), 0)),
                  pl.BlockSpec((d, tn), lambda i, j: (0, j)),
                  pl.BlockSpec((d, tn), lambda i, j: (0, j + nb)),
                  pl.BlockSpec((3, tn), lambda i, j: (0, j)),
                  pl.BlockSpec((3, tn), lambda i, j: (0, j + nb)),
                  pl.BlockSpec((1, tn), lambda i, j: (0, j)),
                  pl.BlockSpec((1, tn), lambda i, j: (0, j + nb))],
        out_specs=pl.BlockSpec((tm, tn), lambda i, j: (i, j)),
        out_shape=jax.ShapeDtypeStruct((m, d_ff), BF16),
        scratch_shapes=[pltpu.VMEM((tm + 2 * HALO, d), BF16)],
        compiler_params=_params("parallel", "arbitrary"),
        name="ffn_up_conv_silu",
    )(h, h, h, w_up, w_up, conv_w, conv_w, conv_b, conv_b)


def _prep_w_in(w_in, d_model, heads):
    dq = heads * 2 * DH_DIFF
    dv = heads * V_DIFF
    qkv_cols = 2 * dq + dv
    lat_cols = w_in.shape[1] - qkv_cols - 2 * d_model
    lat_pad = -lat_cols % LANE
    row_scale = jnp.concatenate([jnp.full((dq,), DH_DIFF ** -0.5 * LOG2E, F32),
                                 jnp.ones((w_in.shape[1] - dq,), F32)])
    wt = (jnp.swapaxes(w_in, 0, 1) * row_scale[:, None]).astype(BF16)
    w_qkv = wt[:qkv_cols]
    w_lat = jnp.pad(wt[qkv_cols:qkv_cols + lat_cols], ((0, lat_pad), (0, 0)))
    w_gate = wt[qkv_cols + lat_cols:]
    return w_qkv, w_lat, w_gate


def _prep_w_q_up(w, heads):
    k = w.shape[0]
    w = w.reshape(k, heads, QK_NOPE + QK_ROPE)
    w = jnp.pad(w, ((0, 0), (0, 0), (0, MLA_QK_PAD - QK_NOPE - QK_ROPE)))
    return w.reshape(k, heads * MLA_QK_PAD).astype(BF16)


def _prep_w_kv_up(w, heads):
    kv = w.shape[0]
    w = w.reshape(kv, heads, QK_NOPE + V_MLA)
    top_k = jnp.pad(w[:, :, :QK_NOPE], ((0, 0), (0, 0), (0, MLA_QK_PAD - QK_NOPE)))
    eye = jnp.pad(jnp.eye(QK_ROPE, dtype=F32), ((0, LANE - QK_ROPE), (QK_NOPE, MLA_QK_PAD - QK_NOPE - QK_ROPE)))
    bot_k = jnp.broadcast_to(eye[:, None, :], (LANE, heads, MLA_QK_PAD))
    w_k = jnp.concatenate([top_k, bot_k], axis=0).reshape(kv + LANE, heads * MLA_QK_PAD)
    w_v = jnp.pad(w[:, :, QK_NOPE:].reshape(kv, heads * V_MLA), ((0, LANE), (0, 0)))
    return jnp.concatenate([w_k, w_v], axis=1).astype(BF16)


def _rope_tables(seq):
    pos = jnp.arange(seq, dtype=F32)
    inv_freq = ROPE_THETA ** (-jnp.arange(0, QK_ROPE, 2, dtype=F32) / QK_ROPE)
    ang = pos[:, None] * inv_freq[None, :]
    cos, sin = jnp.cos(ang), jnp.sin(ang)
    half = QK_ROPE // 2
    z = lambda n: jnp.zeros((seq, n), F32)
    c = jnp.concatenate([cos, cos, z(LANE - QK_ROPE)], axis=1)
    s1 = jnp.concatenate([-sin, z(LANE - half)], axis=1)
    s2 = jnp.concatenate([z(half), sin, z(LANE - QK_ROPE)], axis=1)
    return c, s1, s2


def _encoder_layer(x, batch, seq, layer_idx, rel_bias, wts):
    (rms_attn_g, w_qkv, w_lat, w_gate, lams, diff_subln_g, mla_q_norm_g, w_q_up, mla_kv_norm_g,
     w_kv_ext, w_branch_a, w_branch_b, w_out, rms_ffn_g, w_ffn_up, conv_w, conv_b, w_ffn_down) = wts
    m, d_model = x.shape
    heads = w_branch_a.shape[0] // V_DIFF
    q_lora = mla_q_norm_g.shape[0]
    kv_lora = mla_kv_norm_g.shape[0]
    assert q_lora % kv_lora == 0 and (q_lora + kv_lora) % LANE == 0
    lambda_init = 0.8 - 0.6 * math.exp(-0.3 * layer_idx)
    tm = _tile("mm_tm", seq)

    h = rmsnorm(x, rms_attn_g, BF16)
    qkv = matmul_nt(h, w_qkv, BF16, tm, _tile("in_tn", w_qkv.shape[0]), "in_proj_qkv")
    lat = matmul_nt(h, w_lat, F32, _tile("lat_tm", seq), w_lat.shape[0], "in_proj_latent")
    gates = matmul_nt(h, w_gate, F32, tm, _tile("in_tn", d_model), "in_proj_gates")

    a_out = diff_attention(qkv, rel_bias, lams, diff_subln_g, lambda_init, batch, seq, heads)

    rope_tabs = _rope_tables(seq)
    q_mla = mla_q_up(lat, mla_q_norm_g, w_q_up, rope_tabs, seq, q_lora)
    kv_mla = mla_kv_up(lat, mla_kv_norm_g, w_kv_ext, rope_tabs, seq, q_lora, kv_lora)
    b_out = mla_attention(q_mla, kv_mla, batch, seq, heads)

    merged = gated_merge(a_out, b_out, w_branch_a, w_branch_b, gates, tm, _tile("in_tn", d_model))
    x = matmul_residual(merged, w_out, x, tm, _tile("in_tn", d_model), "out_proj")

    h2 = rmsnorm(x, rms_ffn_g, BF16)
    act = ffn_up(h2, w_ffn_up, conv_w, conv_b, seq)
    return matmul_residual(act, w_ffn_down, x, _tile("down_tm", seq), _tile("down_tn", d_model),
                           "ffn_down")


def kernel(x_prompt, x_sample, rel_bias, final_norm_g, rms_attn_g, w_in, lambda_q1, lambda_k1,
           lambda_q2, lambda_k2, diff_subln_g, mla_q_norm_g, w_mla_q_up, mla_kv_norm_g,
           w_mla_kv_up, w_branch_a, w_branch_b, w_out, rms_ffn_g, w_ffn_up, conv_w, conv_b,
           w_ffn_down):
    depth = w_in.shape[0]
    d_model = x_prompt.shape[-1]
    heads = w_branch_a.shape[1] // V_DIFF

    layers = []
    for l in range(depth):
        w_qkv, w_lat, w_gate = _prep_w_in(w_in[l], d_model, heads)
        layers.append((
            rms_attn_g[l], w_qkv, w_lat, w_gate,
            (lambda_q1[l], lambda_k1[l], lambda_q2[l], lambda_k2[l]), diff_subln_g[l],
            mla_q_norm_g[l], _prep_w_q_up(w_mla_q_up[l], heads),
            mla_kv_norm_g[l], _prep_w_kv_up(w_mla_kv_up[l], heads),
            w_branch_a[l].astype(BF16), w_branch_b[l].astype(BF16), w_out[l].astype(BF16),
            rms_ffn_g[l], w_ffn_up[l].astype(BF16), conv_w[l].astype(F32),
            conv_b[l].reshape(1, -1).astype(F32), w_ffn_down[l].astype(BF16)))

    def trunk(x):
        batch, seq, _ = x.shape
        y = x.reshape(batch * seq, d_model)
        for l in range(depth):
            y = _encoder_layer(y, batch, seq, l, rel_bias, layers[l])
        return rmsnorm(y, final_norm_g, x.dtype).reshape(x.shape)

    return (trunk(x_prompt), trunk(x_sample))
```

```python
import functools
import math

import jax
import jax.numpy as jnp
from jax import lax
from jax.experimental import pallas as pl
from jax.experimental.pallas import tpu as pltpu

DH_DIFF = 64
V_DIFF = 2 * DH_DIFF
QK_NOPE = 128
QK_ROPE = 64
V_MLA = 128
ROPE_THETA = 10000.0
N_BUCKETS = 32
MAX_DISTANCE = 128
EPS = 1e-6
LOG2E = math.log2(math.e)

LANE = 128
BF16_SUBLANE = 16
VMEM_LIMIT_BYTES = 56 * 1024 * 1024
MERGE_VMEM_LIMIT_BYTES = 61 * 1024 * 1024

MLA_QK_PAD = 2 * LANE
HALO = BF16_SUBLANE

TILES = dict(
    norm_tm=512,
    mm_tm=1024, mm_tn=512, in_tn=1024,
    lat_tm=512,
    qup_tm=1024, qup_tn=2048,
    kvup_tm=1024, kvup_tn=2048,
    ffn_tm=1024, ffn_tn=256,
    down_tm=512, down_tn=512,
    diff_tq=512, diff_tk=1024, diff_chunks_per_step=16,
    mla_tq=1024, mla_tk=2048, mla_chunks_per_step=8,
)

F32 = jnp.float32
BF16 = jnp.bfloat16


def _tile(name, dim):
    t = min(TILES[name], dim)
    assert dim % t == 0, (name, dim, t)
    return t


def _params(*sem, vmem_limit_bytes=VMEM_LIMIT_BYTES):
    return pltpu.CompilerParams(dimension_semantics=sem, vmem_limit_bytes=vmem_limit_bytes)


def _rms(x, g):
    return x * lax.rsqrt(jnp.mean(x * x, axis=-1, keepdims=True) + EPS) * g


def _rope128(p, c, s1, s2):
    return (p * c + pltpu.roll(p, LANE - QK_ROPE // 2, 1) * s1
            + pltpu.roll(p, QK_ROPE // 2, 1) * s2)


def _rmsnorm_kernel(x_ref, g_ref, o_ref):
    o_ref[...] = _rms(x_ref[...].astype(F32), g_ref[...]).astype(o_ref.dtype)


def rmsnorm(x, g, out_dtype):
    m, d = x.shape
    tm = _tile("norm_tm", m)
    return pl.pallas_call(
        _rmsnorm_kernel,
        grid=(m // tm,),
        in_specs=[pl.BlockSpec((tm, d), lambda i: (i, 0)),
                  pl.BlockSpec((1, d), lambda i: (0, 0))],
        out_specs=pl.BlockSpec((tm, d), lambda i: (i, 0)),
        out_shape=jax.ShapeDtypeStruct((m, d), out_dtype),
        compiler_params=_params("parallel"),
        name="rmsnorm",
    )(x, g.reshape(1, d).astype(F32))


def _mm_nt_kernel(a_ref, bt_ref, o_ref):
    o_ref[...] = _qk(a_ref[...], bt_ref[...]).astype(o_ref.dtype)


def _mm_nt_sigmoid_kernel(a_ref, bt_ref, o_ref):
    g = _qk(a_ref[...], bt_ref[...])
    o_ref[...] = (1.0 / (1.0 + jnp.exp(-g))).astype(o_ref.dtype)


def matmul_nt(a, bt, out_dtype, tm, tn, name, body=_mm_nt_kernel):
    m, k = a.shape
    n = bt.shape[0]
    return pl.pallas_call(
        body,
        grid=(m // tm, n // tn),
        in_specs=[pl.BlockSpec((tm, k), lambda i, j: (i, 0)),
                  pl.BlockSpec((tn, k), lambda i, j: (j, 0))],
        out_specs=pl.BlockSpec((tm, tn), lambda i, j: (i, j)),
        out_shape=jax.ShapeDtypeStruct((m, n), out_dtype),
        compiler_params=_params("parallel", "arbitrary"),
        name=name,
    )(a, bt)


def _mm_res_kernel(a_ref, b_ref, r_ref, o_ref):
    o_ref[...] = r_ref[...] + jnp.dot(a_ref[...], b_ref[...], preferred_element_type=F32)


def matmul_residual(a, b, res, tm, tn, name):
    m, k = a.shape
    n = b.shape[1]
    return pl.pallas_call(
        _mm_res_kernel,
        grid=(m // tm, n // tn),
        in_specs=[pl.BlockSpec((tm, k), lambda i, j: (i, 0)),
                  pl.BlockSpec((k, tn), lambda i, j: (0, j)),
                  pl.BlockSpec((tm, tn), lambda i, j: (i, j))],
        out_specs=pl.BlockSpec((tm, tn), lambda i, j: (i, j)),
        out_shape=jax.ShapeDtypeStruct((m, n), F32),
        compiler_params=_params("parallel", "arbitrary"),
        name=name,
    )(a, b, res)


def _merge_kernel(a_ref, b_ref, wa_ref, wb_ref, ga_ref, gb_ref, o_ref):
    ya = jnp.dot(a_ref[...], wa_ref[...], preferred_element_type=F32)
    yb = jnp.dot(b_ref[...], wb_ref[...], preferred_element_type=F32)
    sa = ga_ref[...].astype(F32)
    sb = gb_ref[...].astype(F32)
    o_ref[...] = (sa * ya + sb * yb).astype(o_ref.dtype)


def gated_merge(a, b, wa, wb, gates, tm, tn):
    m, ka = a.shape
    kb = b.shape[1]
    n = wa.shape[1]
    nb = n // tn
    return pl.pallas_call(
        _merge_kernel,
        grid=(m // tm, nb),
        in_specs=[pl.BlockSpec((tm, ka), lambda i, j: (i, 0)),
                  pl.BlockSpec((tm, kb), lambda i, j: (i, 0)),
                  pl.BlockSpec((ka, tn), lambda i, j: (0, j)),
                  pl.BlockSpec((kb, tn), lambda i, j: (0, j)),
                  pl.BlockSpec((tm, tn), lambda i, j: (i, j)),
                  pl.BlockSpec((tm, tn), lambda i, j: (i, j + nb))],
        out_specs=pl.BlockSpec((tm, tn), lambda i, j: (i, j)),
        out_shape=jax.ShapeDtypeStruct((m, n), BF16),
        compiler_params=_params("parallel", "arbitrary", vmem_limit_bytes=MERGE_VMEM_LIMIT_BYTES),
        name="gated_merge",
    )(a, b, wa, wb, gates, gates)


def _mla_q_kernel(lat_ref, g_ref, w_ref, c_ref, s1_ref, s2_ref, o_ref, a_sc, *, heads, scale):
    @pl.when(pl.program_id(1) == 0)
    def _():
        a_sc[...] = _rms(lat_ref[...], g_ref[...]).astype(BF16)

    acc = jnp.dot(a_sc[...], w_ref[...], preferred_element_type=F32)
    c, s1, s2 = c_ref[...], s1_ref[...], s2_ref[...]
    for h in range(heads):
        lo = h * MLA_QK_PAD
        o_ref[:, lo:lo + QK_NOPE] = (acc[:, lo:lo + QK_NOPE] * scale).astype(o_ref.dtype)
        rot = _rope128(acc[:, lo + QK_NOPE:lo + MLA_QK_PAD], c, s1, s2)
        o_ref[:, lo + QK_NOPE:lo + MLA_QK_PAD] = (rot * scale).astype(o_ref.dtype)


def mla_q_up(lat, g, w, rope_tabs, seq, q_lora):
    m = lat.shape[0]
    n = w.shape[1]
    tm = _tile("qup_tm", seq)
    tn = _tile("qup_tn", n)
    pos_blocks = seq // tm
    tab_spec = pl.BlockSpec((tm, LANE), lambda i, j: (i % pos_blocks, 0))
    kern = functools.partial(_mla_q_kernel, heads=tn // MLA_QK_PAD,
                             scale=(QK_NOPE + QK_ROPE) ** -0.5 * LOG2E)
    return pl.pallas_call(
        kern,
        grid=(m // tm, n // tn),
        in_specs=[pl.BlockSpec((tm, q_lora), lambda i, j: (i, 0)),
                  pl.BlockSpec((1, q_lora), lambda i, j: (0, 0)),
                  pl.BlockSpec((q_lora, tn), lambda i, j: (0, j)),
                  tab_spec, tab_spec, tab_spec],
        out_specs=pl.BlockSpec((tm, tn), lambda i, j: (i, j)),
        out_shape=jax.ShapeDtypeStruct((m, n), BF16),
        scratch_shapes=[pltpu.VMEM((tm, q_lora), BF16)],
        compiler_params=_params("parallel", "arbitrary"),
        name="mla_q_up",
    )(lat, g.reshape(1, q_lora).astype(F32), w, *rope_tabs)


def _mla_kv_kernel(kv_ref, pe_ref, g_ref, w_ref, c_ref, s1_ref, s2_ref, o_ref, a_sc, *, kv_lora):
    @pl.when(pl.program_id(1) == 0)
    def _():
        a_sc[:, :kv_lora] = _rms(kv_ref[...], g_ref[...]).astype(BF16)
        rot = _rope128(pe_ref[...], c_ref[...], s1_ref[...], s2_ref[...])
        a_sc[:, kv_lora:] = rot.astype(BF16)

    o_ref[...] = jnp.dot(a_sc[...], w_ref[...], preferred_element_type=F32).astype(o_ref.dtype)


def mla_kv_up(lat, g, w_ext, rope_tabs, seq, q_lora, kv_lora):
    m = lat.shape[0]
    n = w_ext.shape[1]
    tm = _tile("kvup_tm", seq)
    tn = _tile("kvup_tn", n)
    pos_blocks = seq // tm
    tab_spec = pl.BlockSpec((tm, LANE), lambda i, j: (i % pos_blocks, 0))
    kv_blk = q_lora // kv_lora
    pe_blk = (q_lora + kv_lora) // LANE
    return pl.pallas_call(
        functools.partial(_mla_kv_kernel, kv_lora=kv_lora),
        grid=(m // tm, n // tn),
        in_specs=[pl.BlockSpec((tm, kv_lora), lambda i, j: (i, kv_blk)),
                  pl.BlockSpec((tm, LANE), lambda i, j: (i, pe_blk)),
                  pl.BlockSpec((1, kv_lora), lambda i, j: (0, 0)),
                  pl.BlockSpec((kv_lora + LANE, tn), lambda i, j: (0, j)),
                  tab_spec, tab_spec, tab_spec],
        out_specs=pl.BlockSpec((tm, tn), lambda i, j: (i, j)),
        out_shape=jax.ShapeDtypeStruct((m, n), BF16),
        scratch_shapes=[pltpu.VMEM((tm, kv_lora + LANE), BF16)],
        compiler_params=_params("parallel", "arbitrary"),
        name="mla_kv_up",
    )(lat, lat, g.reshape(1, kv_lora).astype(F32), w_ext, *rope_tabs)


def _lane_tile(x, n):
    return jnp.concatenate([x] * n, axis=1)


def _softmax_init(m_sc, acc_sc):
    m_sc[...] = jnp.full(m_sc.shape, -jnp.inf, F32)
    acc_sc[...] = jnp.zeros(acc_sc.shape, F32)


def _softmax_update(s, vx, m_sc, acc_sc, c=None):
    tk = s.shape[1]
    m_prev = m_sc[...]
    m_cur = jnp.max(s, axis=1, keepdims=True)
    if c is not None:
        m_cur = m_cur + c
    m_new = jnp.maximum(m_prev, m_cur)
    alpha = jnp.exp2(m_prev - m_new)
    shift = m_new if c is None else m_new - c
    p = jnp.exp2(s - _lane_tile(shift, tk // LANE))
    acc_sc[...] = (_lane_tile(alpha, 2) * acc_sc[...]
                   + jnp.dot(p.astype(BF16), vx, preferred_element_type=F32))
    m_sc[...] = m_new


def _extend_v(v_ref, vx_sc, tk):
    width = v_ref.shape[1]

    def body(j, carry):
        r = pl.ds(pl.multiple_of(j * tk, tk), tk)
        vx_sc[r, :width] = v_ref[r, :]
        vx_sc[r, width:] = jnp.ones((tk, vx_sc.shape[1] - width), vx_sc.dtype)
        return carry

    lax.fori_loop(0, v_ref.shape[0] // tk, body, 0)


def _chunk_rows(j, tk):
    start = j * tk
    return pl.ds(start if isinstance(j, int) else pl.multiple_of(start, tk), tk)


def _qk(q, k):
    return lax.dot_general(q, k, (((1,), (1,)), ((), ())), preferred_element_type=F32)


def _chunk_loop(lo, hi, scores, s_bufs, consume):
    def by_parity(j, fn):
        for parity in (0, 1):
            @pl.when((j & 1) == parity)
            def _():
                fn(j, s_bufs[parity], s_bufs[1 - parity])

    def one(j, cur, nxt):
        nxt[...] = scores(j + 1)
        consume(j, cur)

    def two(j, cur, nxt):
        nxt[...] = scores(j + 1)
        consume(j, cur)
        cur[...] = scores(j + 2)
        consume(j + 1, nxt)

    pairs = lax.shift_right_logical(jnp.maximum(hi - lo, 0), 1)

    def body(t, carry):
        by_parity(lo + 2 * t, two)
        return carry

    lax.fori_loop(0, pairs, body, 0)
    rest = lo + 2 * pairs

    @pl.when(rest < hi)
    def _():
        by_parity(rest, one)


def _first_chunk(s_bufs, scores):
    first = functools.reduce(jnp.logical_and, [pl.program_id(a) == 0 for a in range(3)])

    @pl.when(first)
    def _():
        s_bufs[0][...] = scores(0)


def _last_chunk(n_chunks, next_scores, s_bufs, consume):
    s_bufs[0][...] = next_scores()
    consume(n_chunks - 1, s_bufs[1])


def _next_step(b, h, i, batch, heads, nq):
    n = jnp.minimum((b * heads + h) * nq + i + 1, batch * heads * nq - 1)
    bh = n // nq
    return bh // heads, bh % heads, n % nq


def _mla_attn_kernel(q_ref, k_ref, v_ref, qn_ref, kn_ref, o_ref, vx_sc, sa_sc, sb_sc, m_sc, acc_sc,
                     *, tq, tk):
    @pl.when(pl.program_id(2) == 0)
    def _():
        _extend_v(v_ref, vx_sc, tk)

    rows = lambda j: _chunk_rows(j, tk)
    nk = k_ref.shape[0] // tk
    s_bufs = (sa_sc, sb_sc)

    def consume(j, s_ref):
        _softmax_update(s_ref[...], vx_sc[rows(j), :], m_sc, acc_sc)

    n_tiles = q_ref.shape[0] // tq
    for t in range(n_tiles):
        q_rows = pl.ds(t * tq, tq)
        scores = lambda j, q_rows=q_rows: _qk(q_ref[q_rows, :], k_ref[rows(j), :])
        if t == 0:
            _first_chunk(s_bufs, scores)
        _softmax_init(m_sc, acc_sc)
        _chunk_loop(0, nk - 1, scores, s_bufs, consume)
        if t + 1 < n_tiles:
            next_scores = lambda t=t: _qk(q_ref[pl.ds((t + 1) * tq, tq), :], k_ref[rows(0), :])
        else:
            next_scores = lambda: _qk(qn_ref[...], kn_ref[...])
        _last_chunk(nk, next_scores, s_bufs, consume)
        acc = acc_sc[...]
        o_ref[q_rows, :] = (acc[:, :V_MLA] / acc[:, V_MLA:]).astype(o_ref.dtype)


def _tiles_per_step(nq, nk, max_chunks, max_tiles):
    return next(t for t in (4, 2, 1)
                if t <= max_tiles and nq % t == 0 and (t == 1 or t * nk <= max_chunks))


def mla_attention(q, kv, batch, seq, heads):
    tq = _tile("mla_tq", seq)
    tk = _tile("mla_tk", seq)
    nq, nk = seq // tq, seq // tk
    assert nk % 2 == 0
    per_step = _tiles_per_step(nq, nk, TILES["mla_chunks_per_step"], max_tiles=2)
    steps = nq // per_step
    v_blk0 = heads * MLA_QK_PAD // V_MLA
    nxt = lambda b, h, i: _next_step(b, h, i, batch, heads, steps)

    def q_next(b, h, i):
        b2, h2, i2 = nxt(b, h, i)
        return b2 * nq + i2 * per_step, h2

    def k_next(b, h, i):
        b2, h2, _ = nxt(b, h, i)
        return b2 * nk, h2

    return pl.pallas_call(
        functools.partial(_mla_attn_kernel, tq=tq, tk=tk),
        grid=(batch, heads, steps),
        in_specs=[pl.BlockSpec((per_step * tq, MLA_QK_PAD), lambda b, h, i: (b * steps + i, h)),
                  pl.BlockSpec((seq, MLA_QK_PAD), lambda b, h, i: (b, h)),
                  pl.BlockSpec((seq, V_MLA), lambda b, h, i: (b, v_blk0 + h)),
                  pl.BlockSpec((tq, MLA_QK_PAD), q_next),
                  pl.BlockSpec((tk, MLA_QK_PAD), k_next)],
        out_specs=pl.BlockSpec((per_step * tq, V_MLA), lambda b, h, i: (b * steps + i, h)),
        out_shape=jax.ShapeDtypeStruct((batch * seq, heads * V_MLA), BF16),
        scratch_shapes=[pltpu.VMEM((seq, 2 * V_MLA), BF16),
                        pltpu.VMEM((tq, tk), F32), pltpu.VMEM((tq, tk), F32),
                        pltpu.VMEM((tq, LANE), F32), pltpu.VMEM((tq, 2 * V_MLA), F32)],
        compiler_params=_params("arbitrary", "arbitrary", "arbitrary"),
        name="mla_attention",
    )(q, kv, kv, q, kv)


def _first_biased_chunk(i, tq, tk, nk, n_bias):
    return jnp.minimum(jnp.maximum(i * tq - (MAX_DISTANCE - 1), 0) // tk, nk - n_bias)


def _diff_attn_kernel(tab_ref, far_ref, ids_ref, q_ref, k_ref, v_ref, qn_ref, kn_ref, lq1_ref, lk1_ref, lq2_ref,
                      lk2_ref, g_ref, o_ref, q2_sc, vx_sc, bias_sc, sa_sc, sb_sc, m_sc, acc_sc, *, tq, tk,
                      step, off0, n_bias, lambda_init):
    h, i = pl.program_id(1), pl.program_id(2)
    seq = k_ref.shape[0]
    nk = seq // tk

    @pl.when(i == 0)
    def _():
        _extend_v(v_ref, vx_sc, tk)
        for t in range(bias_sc.shape[0]):
            ids = ids_ref[t]
            f = jnp.zeros(ids.shape, F32)
            for b in range(N_BUCKETS):
                f = jnp.where(ids == b, tab_ref[h, b], f)
            f = f * LOG2E
            bias_sc[t] = pltpu.roll(jnp.broadcast_to(f, (tq, tq + tk)), tk + 1, 1,
                                    stride=1, stride_axis=0)[:, :tk]

    def stacked(q):
        lane = lax.broadcasted_iota(jnp.int32, q.shape, 1)
        zero = jnp.zeros_like(q)
        return jnp.concatenate([jnp.where(lane < DH_DIFF, q, zero),
                                jnp.where(lane >= DH_DIFF, q, zero)], axis=0)

    rows = lambda j: _chunk_rows(j, tk)
    c_left = tab_ref[h, far_ref[0]] * LOG2E
    c_right = tab_ref[h, far_ref[1]] * LOG2E
    lam = (jnp.exp(jnp.sum(lq1_ref[...] * lk1_ref[...], axis=1, keepdims=True))
           - jnp.exp(jnp.sum(lq2_ref[...] * lk2_ref[...], axis=1, keepdims=True)) + lambda_init)
    s_bufs = (sa_sc, sb_sc)

    def tile(t, n_tiles):
        it = i * n_tiles + t
        q_rows = pl.ds(t * tq, tq)
        j0 = _first_biased_chunk(it, tq, tk, nk, n_bias)
        q2_sc[...] = stacked(q_ref[q_rows, :])
        _softmax_init(m_sc, acc_sc)
        scores = lambda j: _qk(q2_sc[...], k_ref[rows(j), :])

        def chunk_at(d):
            j = j0 + d
            return j if d < n_bias else jnp.where(j >= nk, j - nk, j)

        def consume(d, s_ref):
            j = chunk_at(d)
            vx = vx_sc[rows(j), :]
            if d < n_bias:
                bias = bias_sc[(j * tk - it * tq - off0) // step]
                s = s_ref[...] + jnp.concatenate([bias, bias], axis=0)
                _softmax_update(s, vx, m_sc, acc_sc)
            else:
                _softmax_update(s_ref[...], vx, m_sc, acc_sc, jnp.where(j < j0, c_left, c_right))

        if t == 0:
            _first_chunk(s_bufs, lambda _: scores(j0))
        for d in range(nk - 1):
            s_bufs[(d + 1) % 2][...] = scores(chunk_at(d + 1))
            consume(d, s_bufs[d % 2])
        if t + 1 < n_tiles:
            j0_next = _first_biased_chunk(it + 1, tq, tk, nk, n_bias)
            next_scores = lambda: _qk(stacked(q_ref[pl.ds((t + 1) * tq, tq), :]),
                                      k_ref[rows(j0_next), :])
        else:
            next_scores = lambda: _qk(stacked(qn_ref[...]), kn_ref[...])
        _last_chunk(nk, next_scores, s_bufs, consume)

        acc = acc_sc[...]
        o = acc[:, :V_DIFF] / acc[:, V_DIFF:]
        a = o[:tq] - lam * o[tq:]
        o_ref[q_rows, :] = (_rms(a, g_ref[...]) * (1.0 - lambda_init)).astype(o_ref.dtype)

    n_tiles = q_ref.shape[0] // tq
    for t in range(n_tiles):
        tile(t, n_tiles)


def _t5_bucket(rel):
    nb = N_BUCKETS // 2
    max_exact = nb // 2
    ret = (rel > 0).astype(jnp.int32) * nb
    n = jnp.abs(rel)
    nf = jnp.maximum(n, max_exact).astype(F32)
    large = max_exact + (jnp.log(nf / max_exact) / math.log(MAX_DISTANCE / max_exact)
                         * (nb - max_exact)).astype(jnp.int32)
    large = jnp.minimum(large, nb - 1)
    return ret + jnp.where(n < max_exact, n, large)


def diff_attention(qkv, rel_bias, lams, subln_g, lambda_init, batch, seq, heads):
    tq = _tile("diff_tq", seq)
    tk = _tile("diff_tk", seq)
    nq = seq // tq
    step = math.gcd(tq, tk)
    nk = seq // tk
    near_lo, near_hi = -(tk - 1 + MAX_DISTANCE), tq - 1 + MAX_DISTANCE
    n_bias = -(-(near_hi - near_lo - tk) // tk) + 1
    assert nk % 2 == 0 and nk >= n_bias
    off0 = min((near_lo // step + 1) * step, tq - n_bias * tk)
    offs = list(range(off0, max(near_hi - 1, (n_bias - 1) * tk) + 1, step))
    win = (jnp.array(offs, jnp.int32)[:, None] - (tq - 1)
           + jnp.arange(tq + tk, dtype=jnp.int32)[None, :])
    ids = _t5_bucket(win).reshape(len(offs), 1, tq + tk)
    far_ids = _t5_bucket(jnp.array([-MAX_DISTANCE, MAX_DISTANCE], jnp.int32))
    vec = lambda: pl.BlockSpec((1, DH_DIFF), lambda b, h, i: (0, 0))
    kern = functools.partial(_diff_attn_kernel, tq=tq, tk=tk, step=step, off0=off0, n_bias=n_bias,
                             lambda_init=lambda_init)
    per_step = _tiles_per_step(nq, nk, TILES["diff_chunks_per_step"], max_tiles=4)
    steps = nq // per_step
    nxt = lambda b, h, i: _next_step(b, h, i, batch, heads, steps)

    def q_next(b, h, i):
        b2, h2, i2 = nxt(b, h, i)
        return b2 * nq + i2 * per_step, h2

    def k_next(b, h, i):
        b2, h2, i2 = nxt(b, h, i)
        return b2 * nk + _first_biased_chunk(i2 * per_step, tq, tk, nk, n_bias), heads + h2

    return pl.pallas_call(
        kern,
        grid=(batch, heads, steps),
        in_specs=[pl.BlockSpec(memory_space=pltpu.SMEM),
                  pl.BlockSpec(memory_space=pltpu.SMEM),
                  pl.BlockSpec(ids.shape, lambda b, h, i: (0, 0, 0)),
                  pl.BlockSpec((per_step * tq, V_DIFF), lambda b, h, i: (b * steps + i, h)),
                  pl.BlockSpec((seq, V_DIFF), lambda b, h, i: (b, heads + h)),
                  pl.BlockSpec((seq, V_DIFF), lambda b, h, i: (b, 2 * heads + h)),
                  pl.BlockSpec((tq, V_DIFF), q_next),
                  pl.BlockSpec((tk, V_DIFF), k_next),
                  vec(), vec(), vec(), vec(),
                  pl.BlockSpec((1, V_DIFF), lambda b, h, i: (0, 0))],
        out_specs=pl.BlockSpec((per_step * tq, V_DIFF), lambda b, h, i: (b * steps + i, h)),
        out_shape=jax.ShapeDtypeStruct((batch * seq, heads * V_DIFF), BF16),
        scratch_shapes=[pltpu.VMEM((2 * tq, V_DIFF), BF16), pltpu.VMEM((seq, 2 * V_DIFF), BF16),
                        pltpu.VMEM((len(offs), tq, tk), F32),
                        pltpu.VMEM((2 * tq, tk), F32), pltpu.VMEM((2 * tq, tk), F32),
                        pltpu.VMEM((2 * tq, LANE), F32), pltpu.VMEM((2 * tq, 2 * V_DIFF), F32)],
        compiler_params=_params("arbitrary", "arbitrary", "arbitrary"),
        name="diff_attention",
    )(rel_bias.T.astype(F32), far_ids, ids, qkv, qkv, qkv, qkv, qkv,
      *[v.reshape(1, DH_DIFF).astype(F32) for v in lams], subln_g.reshape(1, V_DIFF).astype(F32))


def _ffn_up_kernel(a_ref, prev_ref, next_ref, wg_ref, wu_ref, cwg_ref, cwu_ref, cbg_ref, cbu_ref,
                   o_ref, a_sc, *, tm, tiles_per_seq):
    i = pl.program_id(0)

    @pl.when(pl.program_id(1) == 0)
    def _():
        pos = i % tiles_per_seq
        prev = prev_ref[...]
        nxt = next_ref[...]
        a_sc[:HALO] = jnp.where(pos == 0, jnp.zeros_like(prev), prev)
        a_sc[HALO:HALO + tm] = a_ref[...]
        a_sc[HALO + tm:] = jnp.where(pos == tiles_per_seq - 1, jnp.zeros_like(nxt), nxt)

    def conv(w_ref, cw_ref, cb_ref):
        r = jnp.dot(a_sc[...], w_ref[...], preferred_element_type=F32)
        rows = r.shape[0]
        r_prev = pltpu.roll(r, 1, 0)[HALO:HALO + tm]
        r_next = pltpu.roll(r, rows - 1, 0)[HALO:HALO + tm]
        cw = cw_ref[...]
        return (r_prev * cw[0:1] + r[HALO:HALO + tm] * cw[1:2] + r_next * cw[2:3]) + cb_ref[...]

    gate = conv(wg_ref, cwg_ref, cbg_ref)
    up = conv(wu_ref, cwu_ref, cbu_ref)
    o_ref[...] = (gate * (1.0 / (1.0 + jnp.exp(-gate))) * up).astype(o_ref.dtype)


def ffn_up(h, w_up, conv_w, conv_b, seq):
    m, d = h.shape
    d_ff = w_up.shape[1] // 2
    tm = _tile("ffn_tm", seq)
    tn = _tile("ffn_tn", d_ff)
    nb = d_ff // tn
    hb = tm // HALO
    last_hb = m // HALO - 1
    kern = functools.partial(_ffn_up_kernel, tm=tm, tiles_per_seq=seq // tm)
    return pl.pallas_call(
        kern,
        grid=(m // tm, nb),
        in_specs=[pl.BlockSpec((tm, d), lambda i, j: (i, 0)),
                  pl.BlockSpec((HALO, d), lambda i, j: (jnp.maximum(i * hb - 1, 0), 0)),
                  pl.BlockSpec((HALO, d), lambda i, j: (jnp.minimum((i + 1) * hb, last_hb), 0)),
                  pl.BlockSpec((d, tn), lambda i, j: (0, j)),
                  pl.BlockSpec((d, tn), lambda i, j: (0, j + nb)),
                  pl.BlockSpec((3, tn), lambda i, j: (0, j)),
                  pl.BlockSpec((3, tn), lambda i, j: (0, j + nb)),
                  pl.BlockSpec((1, tn), lambda i, j: (0, j)),
                  pl.BlockSpec((1, tn), lambda i, j: (0, j + nb))],
        out_specs=pl.BlockSpec((tm, tn), lambda i, j: (i, j)),
        out_shape=jax.ShapeDtypeStruct((m, d_ff), BF16),
        scratch_shapes=[pltpu.VMEM((tm + 2 * HALO, d), BF16)],
        compiler_params=_params("parallel", "arbitrary"),
        name="ffn_up_conv_silu",
    )(h, h, h, w_up, w_up, conv_w, conv_w, conv_b, conv_b)


def _prep_w_in(w_in, d_model, heads):
    dq = heads * 2 * DH_DIFF
    dv = heads * V_DIFF
    qkv_cols = 2 * dq + dv
    lat_cols = w_in.shape[1] - qkv_cols - 2 * d_model
    lat_pad = -lat_cols % LANE
    row_scale = jnp.concatenate([jnp.full((dq,), DH_DIFF ** -0.5 * LOG2E, F32),
                                 jnp.ones((w_in.shape[1] - dq,), F32)])
    wt = (jnp.swapaxes(w_in, 0, 1) * row_scale[:, None]).astype(BF16)
    w_qkv = wt[:qkv_cols]
    w_lat = jnp.pad(wt[qkv_cols:qkv_cols + lat_cols], ((0, lat_pad), (0, 0)))
    w_gate = wt[qkv_cols + lat_cols:]
    return w_qkv, w_lat, w_gate


def _prep_w_q_up(w, heads):
    k = w.shape[0]
    w = w.reshape(k, heads, QK_NOPE + QK_ROPE)
    w = jnp.pad(w, ((0, 0), (0, 0), (0, MLA_QK_PAD - QK_NOPE - QK_ROPE)))
    return w.reshape(k, heads * MLA_QK_PAD).astype(BF16)


def _prep_w_kv_up(w, heads):
    kv = w.shape[0]
    w = w.reshape(kv, heads, QK_NOPE + V_MLA)
    top_k = jnp.pad(w[:, :, :QK_NOPE], ((0, 0), (0, 0), (0, MLA_QK_PAD - QK_NOPE)))
    eye = jnp.pad(jnp.eye(QK_ROPE, dtype=F32), ((0, LANE - QK_ROPE), (QK_NOPE, MLA_QK_PAD - QK_NOPE - QK_ROPE)))
    bot_k = jnp.broadcast_to(eye[:, None, :], (LANE, heads, MLA_QK_PAD))
    w_k = jnp.concatenate([top_k, bot_k], axis=0).reshape(kv + LANE, heads * MLA_QK_PAD)
    w_v = jnp.pad(w[:, :, QK_NOPE:].reshape(kv, heads * V_MLA), ((0, LANE), (0, 0)))
    return jnp.concatenate([w_k, w_v], axis=1).astype(BF16)


def _rope_tables(seq):
    pos = jnp.arange(seq, dtype=F32)
    inv_freq = ROPE_THETA ** (-jnp.arange(0, QK_ROPE, 2, dtype=F32) / QK_ROPE)
    ang = pos[:, None] * inv_freq[None, :]
    cos, sin = jnp.cos(ang), jnp.sin(ang)
    half = QK_ROPE // 2
    z = lambda n: jnp.zeros((seq, n), F32)
    c = jnp.concatenate([cos, cos, z(LANE - QK_ROPE)], axis=1)
    s1 = jnp.concatenate([-sin, z(LANE - half)], axis=1)
    s2 = jnp.concatenate([z(half), sin, z(LANE - QK_ROPE)], axis=1)
    return c, s1, s2


def _encoder_layer(x, batch, seq, layer_idx, rel_bias, wts):
    (rms_attn_g, w_qkv, w_lat, w_gate, lams, diff_subln_g, mla_q_norm_g, w_q_up, mla_kv_norm_g,
     w_kv_ext, w_branch_a, w_branch_b, w_out, rms_ffn_g, w_ffn_up, conv_w, conv_b, w_ffn_down) = wts
    m, d_model = x.shape
    heads = w_branch_a.shape[0] // V_DIFF
    q_lora = mla_q_norm_g.shape[0]
    kv_lora = mla_kv_norm_g.shape[0]
    assert q_lora % kv_lora == 0 and (q_lora + kv_lora) % LANE == 0
    lambda_init = 0.8 - 0.6 * math.exp(-0.3 * layer_idx)
    tm = _tile("mm_tm", seq)

    h = rmsnorm(x, rms_attn_g, BF16)
    qkv = matmul_nt(h, w_qkv, BF16, tm, _tile("in_tn", w_qkv.shape[0]), "in_proj_qkv")
    lat = matmul_nt(h, w_lat, F32, _tile("lat_tm", seq), w_lat.shape[0], "in_proj_latent")
    gates = matmul_nt(h, w_gate, BF16, tm, _tile("in_tn", d_model), "in_proj_gates",
                      body=_mm_nt_sigmoid_kernel)

    a_out = diff_attention(qkv, rel_bias, lams, diff_subln_g, lambda_init, batch, seq, heads)

    rope_tabs = _rope_tables(seq)
    q_mla = mla_q_up(lat, mla_q_norm_g, w_q_up, rope_tabs, seq, q_lora)
    kv_mla = mla_kv_up(lat, mla_kv_norm_g, w_kv_ext, rope_tabs, seq, q_lora, kv_lora)
    b_out = mla_attention(q_mla, kv_mla, batch, seq, heads)

    merged = gated_merge(a_out, b_out, w_branch_a, w_branch_b, gates, tm, _tile("in_tn", d_model))
    x = matmul_residual(merged, w_out, x, tm, _tile("in_tn", d_model), "out_proj")

    h2 = rmsnorm(x, rms_ffn_g, BF16)
    act = ffn_up(h2, w_ffn_up, conv_w, conv_b, seq)
    return matmul_residual(act, w_ffn_down, x, _tile("down_tm", seq), _tile("down_tn", d_model),
                           "ffn_down")


def kernel(x_prompt, x_sample, rel_bias, final_norm_g, rms_attn_g, w_in, lambda_q1, lambda_k1,
           lambda_q2, lambda_k2, diff_subln_g, mla_q_norm_g, w_mla_q_up, mla_kv_norm_g,
           w_mla_kv_up, w_branch_a, w_branch_b, w_out, rms_ffn_g, w_ffn_up, conv_w, conv_b,
           w_ffn_down):
    depth = w_in.shape[0]
    d_model = x_prompt.shape[-1]
    heads = w_branch_a.shape[1] // V_DIFF

    layers = []
    for l in range(depth):
        w_qkv, w_lat, w_gate = _prep_w_in(w_in[l], d_model, heads)
        layers.append((
            rms_attn_g[l], w_qkv, w_lat, w_gate,
            (lambda_q1[l], lambda_k1[l], lambda_q2[l], lambda_k2[l]), diff_subln_g[l],
            mla_q_norm_g[l], _prep_w_q_up(w_mla_q_up[l], heads),
            mla_kv_norm_g[l], _prep_w_kv_up(w_mla_kv_up[l], heads),
            w_branch_a[l].astype(BF16), w_branch_b[l].astype(BF16), w_out[l].astype(BF16),
            rms_ffn_g[l], w_ffn_up[l].astype(BF16), conv_w[l].astype(F32),
            conv_b[l].reshape(1, -1).astype(F32), w_ffn_down[l].astype(BF16)))

    def trunk(x):
        batch, seq, _ = x.shape
        y = x.reshape(batch * seq, d_model)
        for l in range(depth):
            y = _encoder_layer(y, batch, seq, l, rel_bias, layers[l])
        return rmsnorm(y, final_norm_g, x.dtype).reshape(x.shape)

    return (trunk(x_prompt), trunk(x_sample))
```
